```python
import jax
import jax.numpy as jnp
from jax import lax
import numpy as np

D_MODEL = 1024
BATCH = 8
SEQ = 4096
DEPTH = 2

CTX_LEN = 256
GRID_W = 64
EPS = 1e-6
M_HEADS = 4
M_QK_DIM = 128
M_V_DIM = 256
M_CHUNK = 128
CONV_DIM = D_MODEL
CONV_WIDTH = 31
A_HEADS = 16
A_KV_HEADS = 4
A_GROUP = A_HEADS // A_KV_HEADS
A_HEAD_DIM = 64
A_BLOCK = 128
WINDOW = 128
ROPE_BASE = 10000.0
ROPE_FREQS = A_HEAD_DIM // 4
D_FF = 2816
N_EXPERTS = 8
TOP_K = 2
D_FF_EXPERT = 3584
M_QK_W = M_HEADS * M_QK_DIM
M_V_W = M_HEADS * M_V_DIM
A_Q_W = A_HEADS * A_HEAD_DIM
A_KV_W = A_KV_HEADS * A_HEAD_DIM
IN_SPLITS = (M_QK_W, M_QK_W, M_V_W, M_V_W, 4 * M_HEADS, 2 * CONV_DIM, A_Q_W, A_KV_W, A_KV_W, 3 * D_MODEL)
D_IN = sum(IN_SPLITS)
N_DENSE = (DEPTH + 1) // 2
N_MOE = DEPTH // 2

kernel_name = 'hybrid_mlstm_conformer_swa_moe_dit'


def rmsnorm(x, g):
    xf = x.astype(jnp.float32)
    y = xf * lax.rsqrt(jnp.mean(xf * xf, axis=-1, keepdims=True) + EPS)
    return (y * g.astype(jnp.float32)).astype(x.dtype)


def layernorm(x, g, b):
    xf = x.astype(jnp.float32)
    mu = jnp.mean(xf, axis=-1, keepdims=True)
    var = jnp.mean(jnp.square(xf - mu), axis=-1, keepdims=True)
    y = (xf - mu) * lax.rsqrt(var + EPS) * g.astype(jnp.float32) + b.astype(jnp.float32)
    return y.astype(x.dtype)


def adaln(cvec, w, b):
    mod = jax.nn.silu(cvec) @ w + b
    return [m[:, None, :] for m in jnp.split(mod, 6, axis=-1)]


def modulate(h, shift, scale):
    return h * (1.0 + scale) + shift


def split_in(h, w, b):
    z = h @ w + b
    offs = []
    acc = 0
    for s in IN_SPLITS[:-1]:
        acc += s
        offs.append(acc)
    return jnp.split(z, offs, axis=-1)


def mlstm_init_state(b):
    f32 = jnp.float32
    return (jnp.zeros((b, M_HEADS, M_V_DIM, M_QK_DIM), f32),
            jnp.zeros((b, M_HEADS, M_QK_DIM), f32),
            jnp.full((b, M_HEADS), -jnp.inf, f32))


def _chunks(a):
    return a.reshape(a.shape[:2] + (a.shape[2] // M_CHUNK, M_CHUNK) + a.shape[3:])


def mlstm_direction(q, k, v, i_pre, log_f, state0):
    B, H, T, _ = q.shape
    qc, kc, vc = _chunks(q), _chunks(k), _chunks(v)
    ic, fc = _chunks(i_pre), _chunks(log_f)
    b = jnp.cumsum(fc, axis=-1)
    b_tot = b[..., -1]
    a = b_tot[..., None] - b + ic
    m_loc = jnp.max(a, axis=-1)
    w = jnp.exp(a - m_loc[..., None])
    c_loc = jnp.einsum('bhnl,bhnlv,bhnlk->bhnvk', w, vc, kc)
    n_loc = jnp.einsum('bhnl,bhnlk->bhnk', w, kc)

    def step(carry, xs):
        C, n, m = carry
        bt, ml, cl, nl = xs
        m_new = jnp.maximum(bt + m, ml)
        s_old = jnp.exp(bt + m - m_new)
        s_new = jnp.exp(ml - m_new)
        C_new = s_old[..., None, None] * C + s_new[..., None, None] * cl
        n_new = s_old[..., None] * n + s_new[..., None] * nl
        return (C_new, n_new, m_new), (C, n, m)

    xs = (jnp.moveaxis(b_tot, 2, 0), jnp.moveaxis(m_loc, 2, 0),
          jnp.moveaxis(c_loc, 2, 0), jnp.moveaxis(n_loc, 2, 0))
    final, (C_prev, n_prev, m_prev) = lax.scan(step, state0, xs)
    C_prev = jnp.moveaxis(C_prev, 0, 2)
    n_prev = jnp.moveaxis(n_prev, 0, 2)
    m_prev = jnp.moveaxis(m_prev, 0, 2)

    g = b + m_prev[..., None]
    tri = jnp.tril(jnp.ones((M_CHUNK, M_CHUNK), dtype=bool))
    d = jnp.where(tri, b[..., :, None] - b[..., None, :] + ic[..., None, :], -jnp.inf)
    m_j = jnp.maximum(g, jnp.max(d, axis=-1))
    inter = jnp.exp(g - m_j)
    s = jnp.einsum('bhnjk,bhnsk->bhnjs', qc, kc) * jnp.exp(d - m_j[..., None])
    num = inter[..., None] * jnp.einsum('bhnjk,bhnvk->bhnjv', qc, C_prev) + jnp.einsum('bhnjs,bhnsv->bhnjv', s, vc)
    den = inter * jnp.einsum('bhnjk,bhnk->bhnj', qc, n_prev) + jnp.sum(s, axis=-1)
    h = num / jnp.maximum(jnp.abs(den), jnp.exp(-m_j))[..., None]
    return h.reshape(B, H, T, M_V_DIM), final


def mlstm_branch(q, k, v, gates, o, norm_g, state_f, state_b):
    B, T, _ = q.shape
    f32 = jnp.float32

    def heads(a, dim):
        return a.reshape(B, T, M_HEADS, dim).transpose(0, 2, 1, 3).astype(f32)

    qh = heads(q, M_QK_DIM) * (M_QK_DIM ** -0.5)
    kh = heads(k, M_QK_DIM)
    vh = heads(v, M_V_DIM)
    gt = gates.astype(f32).reshape(B, T, 2, 2, M_HEADS).transpose(2, 3, 0, 4, 1)
    flip = lambda a: jnp.flip(a, axis=2)
    h_f, st_f = mlstm_direction(qh, kh, vh, gt[0, 0], jax.nn.log_sigmoid(gt[0, 1]), state_f)
    h_b, st_b = mlstm_direction(flip(qh), flip(kh), flip(vh), flip(gt[1, 0]),
                                flip(jax.nn.log_sigmoid(gt[1, 1])), state_b)
    h = h_f + flip(h_b)
    h = h * lax.rsqrt(jnp.mean(h * h, axis=-1, keepdims=True) + EPS) * norm_g.astype(f32)[None, :, None, :]
    h = h.transpose(0, 2, 1, 3).reshape(B, T, M_V_W).astype(q.dtype)
    return h * jax.nn.sigmoid(o), st_f, st_b


def conv_branch(u, w, b, ln_g, ln_b):
    a, g = jnp.split(u, 2, axis=-1)
    y = a * jax.nn.sigmoid(g)
    half = CONV_WIDTH // 2
    y = lax.conv_general_dilated(y, w.reshape(CONV_WIDTH, 1, CONV_DIM), window_strides=(1,),
                                 padding=[(half, half)], dimension_numbers=('NWC', 'WIO', 'NWC'),
                                 feature_group_count=CONV_DIM) + b
    return jax.nn.silu(layernorm(y, ln_g, ln_b))


def rope_tables(T):
    rows = T // GRID_W
    r, col = jnp.meshgrid(jnp.arange(rows), jnp.arange(GRID_W), indexing='ij')
    pos = jnp.stack([r.reshape(-1), col.reshape(-1)], axis=-1).astype(jnp.float32)
    inv = 1.0 / (ROPE_BASE ** (jnp.arange(ROPE_FREQS, dtype=jnp.float32) * 2.0 / (A_HEAD_DIM // 2)))
    ang = pos[:, :, None] * inv
    return jnp.cos(ang), jnp.sin(ang)


def rope_2d(x, cos, sin):
    xs = x.reshape(x.shape[:-1] + (2, 2, ROPE_FREQS))
    x1, x2 = xs[..., 0, :], xs[..., 1, :]
    out = jnp.stack([x1 * cos - x2 * sin, x2 * cos + x1 * sin], axis=-2)
    return out.reshape(x.shape).astype(x.dtype)


def attn_heads(q, k, v, q_g, k_g):
    B, T, _ = q.shape
    qh = q.reshape(B, T, A_KV_HEADS, A_GROUP, A_HEAD_DIM).transpose(0, 2, 3, 1, 4)
    kh = k.reshape(B, T, A_KV_HEADS, A_HEAD_DIM).transpose(0, 2, 1, 3)
    vh = v.reshape(B, T, A_KV_HEADS, A_HEAD_DIM).transpose(0, 2, 1, 3)
    return rmsnorm(qh, q_g), rmsnorm(kh, k_g), vh


def heads_to_tokens(o):
    B, _, _, T, _ = o.shape
    return o.transpose(0, 3, 1, 2, 4).reshape(B, T, A_Q_W)


def softmax_with_sink(s, sink):
    sk = jnp.broadcast_to(sink.astype(jnp.float32).reshape(1, A_KV_HEADS, A_GROUP, 1, 1), s.shape[:-1] + (1,))
    p = jax.nn.softmax(jnp.concatenate([s, sk], axis=-1), axis=-1)
    return p[..., :-1]


def context_attention(q, k, v, sink):
    s = jnp.einsum('bkgqd,bkcd->bkgqc', q, k).astype(jnp.float32) * (A_HEAD_DIM ** -0.5)
    p = softmax_with_sink(s, sink).astype(v.dtype)
    return heads_to_tokens(jnp.einsum('bkgqc,bkcd->bkgqd', p, v))


def latent_attention(q, k, v, k_ctx, v_ctx, sink):
    B, _, _, T, _ = q.shape
    nb = T // A_BLOCK
    pad = [(0, 0), (0, 0), (A_BLOCK, A_BLOCK), (0, 0)]
    kp, vp = jnp.pad(k, pad), jnp.pad(v, pad)
    qb = jnp.moveaxis(q.reshape(B, A_KV_HEADS, A_GROUP, nb, A_BLOCK, A_HEAD_DIM), 3, 0)
    qa = jnp.arange(A_BLOCK)[:, None]
    kb = jnp.arange(3 * A_BLOCK)[None, :]
    near = jnp.abs(kb - A_BLOCK - qa) <= WINDOW
    scale = A_HEAD_DIM ** -0.5

    def block(args):
        n, qn = args
        start = n * A_BLOCK
        kn = lax.dynamic_slice_in_dim(kp, start, 3 * A_BLOCK, axis=2)
        vn = lax.dynamic_slice_in_dim(vp, start, 3 * A_BLOCK, axis=2)
        key_pos = start - A_BLOCK + kb
        mask = near & (key_pos >= 0) & (key_pos < T)
        s_loc = jnp.einsum('bkgqd,bkcd->bkgqc', qn, kn).astype(jnp.float32) * scale
        s_loc = jnp.where(mask, s_loc, -jnp.inf)
        s_ctx = jnp.einsum('bkgqd,bkcd->bkgqc', qn, k_ctx).astype(jnp.float32) * scale
        p = softmax_with_sink(jnp.concatenate([s_loc, s_ctx], axis=-1), sink).astype(v.dtype)
        return (jnp.einsum('bkgqc,bkcd->bkgqd', p[..., :3 * A_BLOCK], vn)
                + jnp.einsum('bkgqc,bkcd->bkgqd', p[..., 3 * A_BLOCK:], v_ctx))

    out = lax.map(block, (jnp.arange(nb), qb))
    out = jnp.moveaxis(out, 0, 3).reshape(B, A_KV_HEADS, A_GROUP, T, A_HEAD_DIM)
    return heads_to_tokens(out)


def merge_branches(y_a, y_b, y_c, gate_logits, w_a, w_b, w_c, w_o):
    g_a, g_b, g_c = jnp.split(jax.nn.sigmoid(gate_logits), 3, axis=-1)
    return (g_a * (y_a @ w_a) + g_b * (y_b @ w_b) + g_c * (y_c @ w_c)) @ w_o


def swiglu(h, w1, w3, w2):
    return (jax.nn.silu(h @ w1) * (h @ w3)) @ w2


def moe_swiglu(h, router_w, w1, w3, w2):
    logits = (h @ router_w).astype(jnp.float32)
    top_v, top_i = lax.top_k(logits, TOP_K)
    top_p = jax.nn.softmax(top_v, axis=-1)
    gates = jnp.sum(jax.nn.one_hot(top_i, N_EXPERTS, dtype=jnp.float32) * top_p[..., None], axis=-2)
    gates = gates.astype(h.dtype)
    out = jnp.zeros_like(h)
    for e in range(N_EXPERTS):
        out = out + gates[..., e:e + 1] * swiglu(h, w1[e], w3[e], w2[e])
    return out


def channel_mixer(l, h, ffn_w1, ffn_w3, ffn_w2, router_w, moe_w1, moe_w3, moe_w2):
    if l % 2 == 0:
        i = l // 2
        return swiglu(h, ffn_w1[i], ffn_w3[i], ffn_w2[i])
    i = l // 2
    return moe_swiglu(h, router_w[i], moe_w1[i], moe_w3[i], moe_w2[i])


def setup_inputs(seed: int = 0) -> dict:
    key = jax.random.key(seed)
    ks = iter(jax.random.split(key, 40))
    f32 = jnp.float32
    D = D_MODEL

    def nrm(shape, scale):
        return jax.random.normal(next(ks), shape, f32) * scale

    b_in = nrm((DEPTH, D_IN), 0.02)
    f_bias = jnp.linspace(3.0, 6.0, M_HEADS, dtype=f32)
    zero = jnp.zeros((M_HEADS,), f32)
    gate_off = sum(IN_SPLITS[:4])
    b_in = b_in.at[:, gate_off:gate_off + 4 * M_HEADS].add(jnp.concatenate([zero, f_bias, zero, f_bias]))
    return {
        'x': nrm((BATCH, SEQ, D), 1.0),
        'c': nrm((BATCH, D), 1.0),
        'ctx': nrm((BATCH, CTX_LEN, D), 1.0),
        'c_ctx': nrm((D,), 1.0),
        'w_mod': nrm((DEPTH, D, 6 * D), D ** -0.5),
        'b_mod': nrm((DEPTH, 6 * D), 0.02),
        'norm1_g': 1.0 + nrm((DEPTH, D), 0.02),
        'norm2_g': 1.0 + nrm((DEPTH, D), 0.02),
        'w_in': nrm((DEPTH, D, D_IN), D ** -0.5),
        'b_in': b_in,
        'm_norm_g': 1.0 + nrm((DEPTH, M_HEADS, M_V_DIM), 0.02),
        'conv_w': nrm((DEPTH, CONV_WIDTH, CONV_DIM), CONV_WIDTH ** -0.5),
        'conv_b': nrm((DEPTH, CONV_DIM), 0.02),
        'conv_ln_g': 1.0 + nrm((DEPTH, CONV_DIM), 0.02),
        'conv_ln_b': nrm((DEPTH, CONV_DIM), 0.02),
        'q_norm_g': 1.0 + nrm((DEPTH, A_HEAD_DIM), 0.02),
        'k_norm_g': 1.0 + nrm((DEPTH, A_HEAD_DIM), 0.02),
        'attn_sink': nrm((DEPTH, A_HEADS), 0.5),
        'w_branch_a': nrm((DEPTH, M_V_W, D), M_V_W ** -0.5),
        'w_branch_b': nrm((DEPTH, CONV_DIM, D), CONV_DIM ** -0.5),
        'w_branch_c': nrm((DEPTH, A_Q_W, D), A_Q_W ** -0.5),
        'w_out': nrm((DEPTH, D, D), D ** -0.5),
        'ffn_w1': nrm((N_DENSE, D, D_FF), D ** -0.5),
        'ffn_w3': nrm((N_DENSE, D, D_FF), D ** -0.5),
        'ffn_w2': nrm((N_DENSE, D_FF, D), D_FF ** -0.5),
        'router_w': nrm((N_MOE, D, N_EXPERTS), D ** -0.5),
        'moe_w1': nrm((N_MOE, N_EXPERTS, D, D_FF_EXPERT), D ** -0.5),
        'moe_w3': nrm((N_MOE, N_EXPERTS, D, D_FF_EXPERT), D ** -0.5),
        'moe_w2': nrm((N_MOE, N_EXPERTS, D_FF_EXPERT, D), D_FF_EXPERT ** -0.5),
    }


def reference(x, c, ctx, c_ctx, w_mod, b_mod, norm1_g, norm2_g, w_in, b_in, m_norm_g, conv_w, conv_b,
              conv_ln_g, conv_ln_b, q_norm_g, k_norm_g, attn_sink, w_branch_a, w_branch_b, w_branch_c,
              w_out, ffn_w1, ffn_w3, ffn_w2, router_w, moe_w1, moe_w3, moe_w2):
    B, T, _ = x.shape
    cos, sin = rope_tables(T)
    x_lat = x
    x_ctx = ctx
    for l in range(DEPTH):
        last = l == DEPTH - 1
        mod_lat = adaln(c, w_mod[l], b_mod[l])
        mod_ctx = adaln(c_ctx[None, :], w_mod[l], b_mod[l])

        hc = modulate(rmsnorm(x_ctx, norm1_g[l]), mod_ctx[0], mod_ctx[1])
        cq, ck, cv, co, cg, cconv, caq, cak, cav, cgate = split_in(hc, w_in[l], b_in[l])
        init = mlstm_init_state(x_ctx.shape[0])
        ya_c, st_f, st_b = mlstm_branch(cq, ck, cv, cg, co, m_norm_g[l], init, init)
        qa_c, ka_c, va_c = attn_heads(caq, cak, cav, q_norm_g[l], k_norm_g[l])

        hx = modulate(rmsnorm(x_lat, norm1_g[l]), mod_lat[0], mod_lat[1])
        xq, xk, xv, xo, xg, xconv, xaq, xak, xav, xgate = split_in(hx, w_in[l], b_in[l])
        ya_x, _, _ = mlstm_branch(xq, xk, xv, xg, xo, m_norm_g[l], st_f, st_b)
        yb_x = conv_branch(xconv, conv_w[l], conv_b[l], conv_ln_g[l], conv_ln_b[l])
        qa_x, ka_x, va_x = attn_heads(xaq, xak, xav, q_norm_g[l], k_norm_g[l])
        qa_x, ka_x = rope_2d(qa_x, cos, sin), rope_2d(ka_x, cos, sin)
        yc_x = latent_attention(qa_x, ka_x, va_x, ka_c, va_c, attn_sink[l])
        x_lat = x_lat + mod_lat[2] * merge_branches(ya_x, yb_x, yc_x, xgate, w_branch_a[l], w_branch_b[l],
                                                    w_branch_c[l], w_out[l])
        if not last:
            yb_c = conv_branch(cconv, conv_w[l], conv_b[l], conv_ln_g[l], conv_ln_b[l])
            yc_c = context_attention(qa_c, ka_c, va_c, attn_sink[l])
            x_ctx = x_ctx + mod_ctx[2] * merge_branches(ya_c, yb_c, yc_c, cgate, w_branch_a[l], w_branch_b[l],
                                                        w_branch_c[l], w_out[l])

        h2 = modulate(rmsnorm(x_lat, norm2_g[l]), mod_lat[3], mod_lat[4])
        x_lat = x_lat + mod_lat[5] * channel_mixer(l, h2, ffn_w1, ffn_w3, ffn_w2, router_w, moe_w1, moe_w3, moe_w2)
        if not last:
            h2c = modulate(rmsnorm(x_ctx, norm2_g[l]), mod_ctx[3], mod_ctx[4])
            x_ctx = x_ctx + mod_ctx[5] * channel_mixer(l, h2c, ffn_w1, ffn_w3, ffn_w2, router_w, moe_w1, moe_w3, moe_w2)
    return x_lat
```

```python
import functools

import jax
import jax.numpy as jnp
from jax import lax
from jax.experimental import pallas as pl
from jax.experimental.pallas import tpu as pltpu

F32 = jnp.float32
BF16 = jnp.bfloat16

EPS = 1e-6
GRID_W = 64
M_HEADS = 4
M_QK_DIM = 128
M_V_DIM = 256
M_CHUNK = 128
CONV_WIDTH = 31
CONV_HALO = 16
A_HEADS = 16
A_KV_HEADS = 4
A_GROUP = A_HEADS // A_KV_HEADS
A_HEAD_DIM = 64
A_BLOCK = 128
WINDOW = 128
ROPE_BASE = 10000.0
ROPE_FREQS = A_HEAD_DIM // 4
N_EXPERTS = 8
LANES = 128
VMEM_LIMIT = 56 * 1024 * 1024

Z_MQ, Z_MK, Z_MV, Z_MO = 0, 512, 1024, 2048
Z_CA, Z_CG, Z_AQ, Z_GATE, Z_AK, Z_AV = 3072, 4096, 5120, 6144, 9216, 9472
NZ = 9728


def _params(**kw):
    return pltpu.CompilerParams(vmem_limit_bytes=VMEM_LIMIT, **kw)


def _sigmoid(x):
    return 1.0 / (1.0 + jnp.exp(-x))


def _silu(x):
    return x * _sigmoid(x)


def _log_sigmoid(x):
    return jnp.minimum(x, 0.0) - jnp.log(1.0 + jnp.exp(-jnp.abs(x)))


def _norm_mod(x, g, shift, scale):
    ms = jnp.mean(x * x, axis=-1, keepdims=True)
    return (x * lax.rsqrt(ms + EPS) * g) * (1.0 + scale) + shift


def _chunks(n, w):
    out, c = [], 0
    while c < n:
        out.append((c, min(w, n - c)))
        c += w
    return out


def _mod_kernel(c_ref, w_ref, b_ref, o_ref):
    s = _silu(c_ref[...]).astype(BF16)
    o_ref[0] = jnp.dot(s, w_ref[0].astype(BF16), preferred_element_type=F32) + b_ref[0]


def _modulation(cvec, w_mod, b_mod):
    depth, d, n = w_mod.shape
    tn = 1536
    return pl.pallas_call(
        _mod_kernel,
        grid=(depth, n // tn),
        in_specs=[pl.BlockSpec((cvec.shape[0], d), lambda l, j: (0, 0)),
                  pl.BlockSpec((1, d, tn), lambda l, j: (l, 0, j)),
                  pl.BlockSpec((1, 1, tn), lambda l, j: (l, 0, j))],
        out_specs=pl.BlockSpec((1, cvec.shape[0], tn), lambda l, j: (l, 0, j)),
        out_shape=jax.ShapeDtypeStruct((depth, cvec.shape[0], n), F32),
        compiler_params=_params(),
        name="adaln_mod",
    )(cvec, w_mod, b_mod.reshape(depth, 1, n))


def _inproj_kernel(x_ref, mod_ref, g_ref, w_ref, b_ref, wg_ref, bg_ref, z_ref, gates_ref, h_scr, *, col_chunks):
    @pl.when(pl.program_id(1) == 0)
    def _():
        h = _norm_mod(x_ref[...], g_ref[...], mod_ref[0, 0:1, :], mod_ref[0, 1:2, :]).astype(BF16)
        h_scr[...] = h
        gates_ref[...] = jnp.dot(h, wg_ref[...], preferred_element_type=F32) + bg_ref[...]

    h = h_scr[...]
    for c0, cw in col_chunks:
        acc = jnp.dot(h, w_ref[:, c0:c0 + cw], preferred_element_type=F32) + b_ref[:, c0:c0 + cw]
        z_ref[:, c0:c0 + cw] = acc.astype(BF16)


def _mod_index(i, n_lat_tiles, tiles_per_batch, n_batch):
    return jnp.where(i < n_lat_tiles, i // tiles_per_batch, n_batch)


def _in_projection(x, modtab, g1, w_z, b_z, w_g, b_g, tm, n_lat_rows, t_lat):
    m, d = x.shape
    n_batch = modtab.shape[0] - 1
    tn = NZ // 4
    n_lat_tiles, tpb = n_lat_rows // tm, t_lat // tm
    mod_map = lambda i, j: (_mod_index(i, n_lat_tiles, tpb, n_batch), 0, 0)
    return pl.pallas_call(
        functools.partial(_inproj_kernel, col_chunks=_chunks(tn, 256)),
        grid=(m // tm, NZ // tn),
        in_specs=[pl.BlockSpec((tm, d), lambda i, j: (i, 0)),
                  pl.BlockSpec((1, 6, d), mod_map),
                  pl.BlockSpec((1, d), lambda i, j: (0, 0)),
                  pl.BlockSpec((d, tn), lambda i, j: (0, j)),
                  pl.BlockSpec((1, tn), lambda i, j: (0, j)),
                  pl.BlockSpec((d, LANES), lambda i, j: (0, 0)),
                  pl.BlockSpec((1, LANES), lambda i, j: (0, 0))],
        out_specs=[pl.BlockSpec((tm, tn), lambda i, j: (i, j)),
                   pl.BlockSpec((tm, LANES), lambda i, j: (i, 0))],
        out_shape=[jax.ShapeDtypeStruct((m, NZ), BF16), jax.ShapeDtypeStruct((m, LANES), F32)],
        scratch_shapes=[pltpu.VMEM((tm, d), BF16)],
        compiler_params=_params(),
        name="in_projection",
    )(x, modtab, g1, w_z, b_z, w_g, b_g)


def _gateprep_kernel(g_ref, o_ref, *, n_chunks):
    r = lax.broadcasted_iota(jnp.int32, (M_CHUNK, M_CHUNK), 0)
    c = lax.broadcasted_iota(jnp.int32, (M_CHUNK, M_CHUNK), 1)
    tri_lo = (c <= r).astype(F32)
    tri_up = (c >= r).astype(F32)
    lane = lax.broadcasted_iota(jnp.int32, (M_CHUNK, LANES), 1)
    for n in range(n_chunks):
        g = g_ref[n * M_CHUNK:(n + 1) * M_CHUNK, :]
        lf = _log_sigmoid(g)
        cf = jnp.dot(tri_lo, lf, preferred_element_type=F32, precision=lax.Precision.HIGHEST)
        cb = jnp.dot(tri_up, lf, preferred_element_type=F32, precision=lax.Precision.HIGHEST)
        fwd_f = (lane >= M_HEADS) & (lane < 2 * M_HEADS)
        bwd_f = (lane >= 3 * M_HEADS) & (lane < 4 * M_HEADS)
        o_ref[n * M_CHUNK:(n + 1) * M_CHUNK, :] = jnp.where(fwd_f, cf, jnp.where(bwd_f, cb, g))


def _gate_prep(gates, tm):
    m = gates.shape[0]
    return pl.pallas_call(
        functools.partial(_gateprep_kernel, n_chunks=tm // M_CHUNK),
        grid=(m // tm,),
        in_specs=[pl.BlockSpec((tm, LANES), lambda i: (i, 0))],
        out_specs=pl.BlockSpec((tm, LANES), lambda i: (i, 0)),
        out_shape=jax.ShapeDtypeStruct((m, LANES), F32),
        compiler_params=_params(),
        name="mlstm_gate_prep",
    )(gates)


def _mlstm_chunk(q, k, kt, v, i_col, b_col, i_row, b_row, btot, c_ref, n_ref, m_ref, mask):
    scale = M_QK_DIM ** -0.5
    m_prev = m_ref[...]
    c_prev = c_ref[...]
    n_prev = n_ref[...]
    g_col = b_col + m_prev
    dmat = jnp.where(mask, b_col - b_row + i_row, -jnp.inf)
    m_j = jnp.maximum(g_col, jnp.max(dmat, axis=-1, keepdims=True))
    inter = jnp.exp(g_col - m_j)
    s = jnp.dot(q, kt, preferred_element_type=F32) * scale * jnp.exp(dmat - m_j)
    qc = jnp.dot(q, c_prev.astype(BF16), preferred_element_type=F32) * scale
    num = inter * qc + jnp.dot(s.astype(BF16), v, preferred_element_type=F32)
    qn = jnp.sum(q.astype(F32) * n_prev, axis=-1, keepdims=True) * scale
    den = inter * qn + jnp.sum(s, axis=-1, keepdims=True)
    h = num / jnp.maximum(jnp.abs(den), jnp.exp(-m_j))

    a_col = btot - b_col + i_col
    a_row = btot - b_row + i_row
    m_loc = jnp.max(a_row, axis=-1, keepdims=True)
    w_col = jnp.exp(a_col - m_loc)
    w_row = jnp.exp(a_row - m_loc)
    m_new = jnp.maximum(btot + m_prev, m_loc)
    s_old = jnp.exp(btot + m_prev - m_new)
    s_new = jnp.exp(m_loc - m_new)
    c_loc = jnp.dot((kt.astype(F32) * w_row).astype(BF16), v, preferred_element_type=F32)
    n_loc = jnp.sum(k.astype(F32) * w_col, axis=0, keepdims=True)
    c_ref[...] = s_old * c_prev + s_new * c_loc
    n_ref[...] = s_old * n_prev + s_new * n_loc
    m_ref[...] = m_new
    return h


def _mlstm_kernel(ql, kl, ktl, vl, ol, acl, arl, qc, kc, ktc, vc, oc, acc_, arc, ng_ref, out_l, out_c,
                  hf_l, hf_c, cf_ref, nf_ref, mf_ref, cb_ref, nb_ref, mb_ref, *, n_lat, n_ctx):
    L = M_CHUNK
    cf_ref[...] = jnp.zeros_like(cf_ref)
    nf_ref[...] = jnp.zeros_like(nf_ref)
    mf_ref[...] = jnp.full_like(mf_ref, -jnp.inf)
    cb_ref[...] = jnp.zeros_like(cb_ref)
    nb_ref[...] = jnp.zeros_like(nb_ref)
    mb_ref[...] = jnp.full_like(mb_ref, -jnp.inf)
    r = lax.broadcasted_iota(jnp.int32, (L, L), 0)
    c = lax.broadcasted_iota(jnp.int32, (L, L), 1)
    mask_f = c <= r
    mask_b = c >= r
    ng = ng_ref[0]

    def fwd(q_r, k_r, kt_r, v_r, ac_r, ar_r, hf_r, n):
        rows = pl.ds(pl.multiple_of(n * L, L), L)
        ac = ac_r[0, rows, :]
        ar = ar_r[0, :, rows]
        h = _mlstm_chunk(q_r[rows, :], k_r[rows, :], kt_r[:, rows], v_r[rows, :],
                         ac[:, 0:1], ac[:, 1:2], ar[0:1, :], ar[1:2, :], ar[1:2, L - 1:L],
                         cf_ref, nf_ref, mf_ref, mask_f)
        hf_r[rows, :] = h

    def bwd(q_r, k_r, kt_r, v_r, o_r, ac_r, ar_r, hf_r, out_r, n):
        rows = pl.ds(pl.multiple_of(n * L, L), L)
        ac = ac_r[0, rows, :]
        ar = ar_r[0, :, rows]
        h = _mlstm_chunk(q_r[rows, :], k_r[rows, :], kt_r[:, rows], v_r[rows, :],
                         ac[:, 2:3], ac[:, 3:4], ar[2:3, :], ar[3:4, :], ar[3:4, 0:1],
                         cb_ref, nb_ref, mb_ref, mask_b)
        h = h + hf_r[rows, :]
        y = h * lax.rsqrt(jnp.mean(h * h, axis=-1, keepdims=True) + EPS) * ng
        out_r[rows, :] = (y * _sigmoid(o_r[rows, :].astype(F32))).astype(out_r.dtype)

    for n in range(n_ctx):
        fwd(qc, kc, ktc, vc, acc_, arc, hf_c, n)

    def fbody(n, carry):
        fwd(ql, kl, ktl, vl, acl, arl, hf_l, n)
        return carry
    lax.fori_loop(0, n_lat, fbody, 0)

    for n in range(n_ctx - 1, -1, -1):
        bwd(qc, kc, ktc, vc, oc, acc_, arc, hf_c, out_c, n)

    def bbody(t, carry):
        bwd(ql, kl, ktl, vl, ol, acl, arl, hf_l, out_l, n_lat - 1 - t)
        return carry
    lax.fori_loop(0, n_lat, bbody, 0)


def _mlstm(z, kt, auxc, auxr, norm_g, n_batch, t_lat, t_ctx):
    dk, dv, nh = M_QK_DIM, M_V_DIM, M_HEADS
    cb = n_batch * t_lat // t_ctx
    lat = lambda w, off: pl.BlockSpec((t_lat, w), lambda b, h: (b, off // w + h))
    ctx = lambda w, off: pl.BlockSpec((t_ctx, w), lambda b, h: (cb + b, off // w + h))
    in_specs = [
        lat(dk, Z_MQ), lat(dk, Z_MK), pl.BlockSpec((dk, t_lat), lambda b, h: (h, b)), lat(dv, Z_MV), lat(dv, Z_MO),
        pl.BlockSpec((1, t_lat, 8), lambda b, h: (h, b, 0)), pl.BlockSpec((1, 8, t_lat), lambda b, h: (h, 0, b)),
        ctx(dk, Z_MQ), ctx(dk, Z_MK), pl.BlockSpec((dk, t_ctx), lambda b, h: (h, cb + b)), ctx(dv, Z_MV),
        ctx(dv, Z_MO),
        pl.BlockSpec((1, t_ctx, 8), lambda b, h: (h, cb + b, 0)),
        pl.BlockSpec((1, 8, t_ctx), lambda b, h: (h, 0, cb + b)),
        pl.BlockSpec((1, 1, dv), lambda b, h: (h, 0, 0)),
    ]
    out_specs = [pl.BlockSpec((t_lat, dv), lambda b, h: (b, h)), pl.BlockSpec((t_ctx, dv), lambda b, h: (b, h))]
    out_shape = [jax.ShapeDtypeStruct((n_batch * t_lat, nh * dv), BF16),
                 jax.ShapeDtypeStruct((n_batch * t_ctx, nh * dv), BF16)]
    scratch = [pltpu.VMEM((t_lat, dv), F32), pltpu.VMEM((t_ctx, dv), F32),
               pltpu.VMEM((dk, dv), F32), pltpu.VMEM((1, dk), F32), pltpu.VMEM((1, 1), F32),
               pltpu.VMEM((dk, dv), F32), pltpu.VMEM((1, dk), F32), pltpu.VMEM((1, 1), F32)]
    return pl.pallas_call(
        functools.partial(_mlstm_kernel, n_lat=t_lat // M_CHUNK, n_ctx=t_ctx // M_CHUNK),
        grid=(n_batch, nh),
        in_specs=in_specs, out_specs=out_specs, out_shape=out_shape, scratch_shapes=scratch,
        compiler_params=_params(),
        name="mlstm",
    )(z, z, kt, z, z, auxc, auxr, z, z, kt, z, z, auxc, auxr, norm_g.reshape(nh, 1, dv))


def _conv_kernel(a_ref, g_ref, ap_ref, gp_ref, an_ref, gn_ref, w_ref, b_ref, lg_ref, lb_ref, o_ref, y_scr, c_scr,
                 *, tt, tiles_per_seq, row_sub, col_w):
    i = pl.program_id(1)
    hl = CONV_HALO
    glu = lambda a, g: a.astype(F32) * _sigmoid(g.astype(F32))
    prev_ok = (i > 0).astype(F32)
    next_ok = (i < tiles_per_seq - 1).astype(F32)
    y_scr[0:hl, :] = glu(ap_ref[...], gp_ref[...]) * prev_ok
    y_scr[hl:hl + tt, :] = glu(a_ref[...], g_ref[...])
    y_scr[hl + tt:hl + tt + hl, :] = glu(an_ref[...], gn_ref[...]) * next_ok
    half = CONV_WIDTH // 2
    d = a_ref.shape[1]
    for c0 in range(0, d, col_w):
        wc = w_ref[:, c0:c0 + col_w]
        for r0 in range(0, tt, row_sub):
            acc = jnp.zeros((row_sub, col_w), F32)
            for j in range(CONV_WIDTH):
                s0 = hl + r0 + j - half
                acc = acc + y_scr[s0:s0 + row_sub, c0:c0 + col_w] * wc[j:j + 1, :]
            c_scr[r0:r0 + row_sub, c0:c0 + col_w] = acc
    y = c_scr[...] + b_ref[...]
    mu = jnp.mean(y, axis=-1, keepdims=True)
    yc = y - mu
    var = jnp.mean(yc * yc, axis=-1, keepdims=True)
    o_ref[...] = _silu(yc * lax.rsqrt(var + EPS) * lg_ref[...] + lb_ref[...]).astype(o_ref.dtype)


def _conv_branch(z, row0, n_seq, t_seq, tt, conv_w, conv_b, ln_g, ln_b):
    d = conv_w.shape[1]
    hl = CONV_HALO
    tps = t_seq // tt
    rb0, hb0, hps = row0 // tt, row0 // hl, t_seq // hl
    ca, cg = Z_CA // d, Z_CG // d
    cur = lambda col: pl.BlockSpec((tt, d), lambda s, i: (rb0 + s * tps + i, col))
    prv = lambda col: pl.BlockSpec(
        (hl, d), lambda s, i: (hb0 + s * hps + jnp.maximum(i * (tt // hl) - 1, 0), col))
    nxt = lambda col: pl.BlockSpec(
        (hl, d), lambda s, i: (hb0 + s * hps + jnp.minimum((i + 1) * (tt // hl), hps - 1), col))
    vec = pl.BlockSpec((1, d), lambda s, i: (0, 0))
    wpad = jnp.zeros((32, d), F32).at[:CONV_WIDTH].set(conv_w)
    return pl.pallas_call(
        functools.partial(_conv_kernel, tt=tt, tiles_per_seq=tps, row_sub=64, col_w=256),
        grid=(n_seq, tps),
        in_specs=[cur(ca), cur(cg), prv(ca), prv(cg), nxt(ca), nxt(cg),
                  pl.BlockSpec((32, d), lambda s, i: (0, 0)), vec, vec, vec],
        out_specs=pl.BlockSpec((tt, d), lambda s, i: (s * tps + i, 0)),
        out_shape=jax.ShapeDtypeStruct((n_seq * t_seq, d), BF16),
        scratch_shapes=[pltpu.VMEM((tt + 2 * hl, d), F32), pltpu.VMEM((tt, d), F32)],
        compiler_params=_params(),
        name="conv_branch",
    )(z, z, z, z, z, z, wpad, conv_b.reshape(1, d), ln_g.reshape(1, d), ln_b.reshape(1, d))


def _headnorm_rope(x, bd, gain, cos, sin):
    x2 = x * x
    hi = x2.astype(BF16)
    lo = (x2 - hi.astype(F32)).astype(BF16)
    ss = jnp.dot(hi, bd, preferred_element_type=F32) + jnp.dot(lo, bd, preferred_element_type=F32)
    xn = x * lax.rsqrt(ss * (1.0 / A_HEAD_DIM) + EPS) * gain
    w = x.shape[1]
    lane = lax.broadcasted_iota(jnp.int32, x.shape, 1)
    first = (lane % (2 * ROPE_FREQS)) < ROPE_FREQS
    swapped = jnp.where(first, pltpu.roll(xn, w - ROPE_FREQS, 1), pltpu.roll(xn, ROPE_FREQS, 1))
    return xn * cos + swapped * sin


def _qkprep_kernel(q_ref, k_ref, cos_ref, sin_ref, qg_ref, kg_ref, qo_ref, ko_ref):
    gw = A_GROUP * A_HEAD_DIM
    r = lax.broadcasted_iota(jnp.int32, (gw, gw), 0) // A_HEAD_DIM
    c = lax.broadcasted_iota(jnp.int32, (gw, gw), 1) // A_HEAD_DIM
    bd = (r == c).astype(BF16)
    cos, sin = cos_ref[...], sin_ref[...]
    for g in range(q_ref.shape[1] // gw):
        x = q_ref[:, g * gw:(g + 1) * gw].astype(F32)
        qo_ref[:, g * gw:(g + 1) * gw] = _headnorm_rope(x, bd, qg_ref[...], cos, sin).astype(qo_ref.dtype)
    ko_ref[...] = _headnorm_rope(k_ref[...].astype(F32), bd, kg_ref[...], cos, sin).astype(ko_ref.dtype)


def _qk_prep(z, cos_tab, sin_tab, q_g, k_g, tm, n_lat_rows, t_lat):
    m = z.shape[0]
    qw, kw = A_HEADS * A_HEAD_DIM, A_KV_HEADS * A_HEAD_DIM
    n_lat_tiles, tpb = n_lat_rows // tm, t_lat // tm
    tab = pl.BlockSpec((tm, kw), lambda i: (jnp.where(i < n_lat_tiles, i % tpb, tpb), 0))
    vec = pl.BlockSpec((1, kw), lambda i: (0, 0))
    return pl.pallas_call(
        _qkprep_kernel,
        grid=(m // tm,),
        in_specs=[pl.BlockSpec((tm, qw), lambda i: (i, Z_AQ // qw)),
                  pl.BlockSpec((tm, kw), lambda i: (i, Z_AK // kw)), tab, tab, vec, vec],
        out_specs=[pl.BlockSpec((tm, qw), lambda i: (i, 0)), pl.BlockSpec((tm, kw), lambda i: (i, 0))],
        out_shape=[jax.ShapeDtypeStruct((m, qw), BF16), jax.ShapeDtypeStruct((m, kw), BF16)],
        compiler_params=_params(),
        name="attn_qk_prep",
    )(z, z, cos_tab, sin_tab, jnp.tile(q_g, A_KV_HEADS).reshape(1, kw), jnp.tile(k_g, A_KV_HEADS).reshape(1, kw))


def _attn_kernel(sink_ref, q_ref, k_ref, v_ref, kc_ref, vc_ref, o_ref, *, local, t_seq):
    nq = pl.program_id(1)
    blk, dh, grp = A_BLOCK, A_HEAD_DIM, A_GROUP
    scale = dh ** -0.5
    nt = (((1,), (1,)), ((), ()))
    if local:
        win = 3 * blk
        start = pl.multiple_of(jnp.clip(nq * blk - blk, 0, t_seq - win), blk)
        qpos = nq * blk + lax.broadcasted_iota(jnp.int32, (grp * blk, win), 0) % blk
        kpos = start + lax.broadcasted_iota(jnp.int32, (grp * blk, win), 1)
        near = jnp.abs(kpos - qpos) <= WINDOW
    rowgrp = lax.broadcasted_iota(jnp.int32, (grp * blk, 1), 0) // blk
    outs = []
    for h in range(A_KV_HEADS):
        cols = slice(h * dh, (h + 1) * dh)
        qs = jnp.concatenate([q_ref[:, (h * grp + g) * dh:(h * grp + g + 1) * dh] for g in range(grp)], axis=0)
        sink = jnp.zeros((grp * blk, 1), F32)
        for g in range(grp):
            sink = jnp.where(rowgrp == g, sink_ref[h * grp + g], sink)
        s_ctx = lax.dot_general(qs, kc_ref[:, cols], nt, preferred_element_type=F32) * scale
        mx = jnp.maximum(jnp.max(s_ctx, axis=-1, keepdims=True), sink)
        if local:
            kw = k_ref[pl.ds(start, win), cols]
            vw = v_ref[pl.ds(start, win), cols]
            s_loc = lax.dot_general(qs, kw, nt, preferred_element_type=F32) * scale
            s_loc = jnp.where(near, s_loc, -jnp.inf)
            mx = jnp.maximum(mx, jnp.max(s_loc, axis=-1, keepdims=True))
        p_ctx = jnp.exp(s_ctx - mx)
        den = jnp.sum(p_ctx, axis=-1, keepdims=True) + jnp.exp(sink - mx)
        if local:
            p_loc = jnp.exp(s_loc - mx)
            den = den + jnp.sum(p_loc, axis=-1, keepdims=True)
        inv = 1.0 / den
        o = jnp.dot((p_ctx * inv).astype(BF16), vc_ref[:, cols], preferred_element_type=F32)
        if local:
            o = o + jnp.dot((p_loc * inv).astype(BF16), vw, preferred_element_type=F32)
        outs.extend(o[g * blk:(g + 1) * blk, :] for g in range(grp))
    o_ref[...] = jnp.concatenate(outs, axis=1).astype(o_ref.dtype)


def _attention(sink, q, k, v, q_row0, n_batch, t_seq, t_ctx, ctx_row0, local):
    qw, kw = A_HEADS * A_HEAD_DIM, A_KV_HEADS * A_HEAD_DIM
    nqb = t_seq // A_BLOCK
    qb0, sb0, cb0 = q_row0 // A_BLOCK, q_row0 // t_seq, ctx_row0 // t_ctx
    zv, vcol = v
    return pl.pallas_call(
        functools.partial(_attn_kernel, local=local, t_seq=t_seq),
        grid_spec=pltpu.PrefetchScalarGridSpec(
            num_scalar_prefetch=1,
            grid=(n_batch, nqb),
            in_specs=[pl.BlockSpec((A_BLOCK, qw), lambda b, n, s: (qb0 + b * nqb + n, 0)),
                      pl.BlockSpec((t_seq, kw), lambda b, n, s: (sb0 + b, 0)),
                      pl.BlockSpec((t_seq, kw), lambda b, n, s: (sb0 + b, vcol)),
                      pl.BlockSpec((t_ctx, kw), lambda b, n, s: (cb0 + b, 0)),
                      pl.BlockSpec((t_ctx, kw), lambda b, n, s: (cb0 + b, vcol))],
            out_specs=pl.BlockSpec((A_BLOCK, qw), lambda b, n, s: (b * nqb + n, 0))),
        out_shape=jax.ShapeDtypeStruct((n_batch * t_seq, qw), BF16),
        compiler_params=_params(),
        name="attention_local" if local else "attention_ctx",
    )(sink, q, k, zv, k, zv)


def _merge_kernel(ya_ref, yb_ref, yc_ref, gl_ref, x_ref, mod_ref, wa_ref, wb_ref, wc_ref, wo_ref, xo_ref, m_scr):
    d = x_ref.shape[1]
    ya, yb, yc = ya_ref[...], yb_ref[...], yc_ref[...]
    for c0, cw in _chunks(d, 256):
        cs = slice(c0, c0 + cw)
        acc = _sigmoid(gl_ref[:, c0:c0 + cw].astype(F32)) * jnp.dot(ya, wa_ref[:, cs], preferred_element_type=F32)
        acc = acc + _sigmoid(gl_ref[:, d + c0:d + c0 + cw].astype(F32)) * jnp.dot(
            yb, wb_ref[:, cs], preferred_element_type=F32)
        acc = acc + _sigmoid(gl_ref[:, 2 * d + c0:2 * d + c0 + cw].astype(F32)) * jnp.dot(
            yc, wc_ref[:, cs], preferred_element_type=F32)
        m_scr[:, cs] = acc.astype(BF16)
    mm = m_scr[...]
    for c0, cw in _chunks(d, 256):
        cs = slice(c0, c0 + cw)
        xo_ref[:, cs] = x_ref[:, cs] + mod_ref[0, 2:3, cs] * jnp.dot(mm, wo_ref[:, cs], preferred_element_type=F32)


def _merge(ya, yb, yc, z, x, modtab, wa, wb, wc, wo, tm, n_rows, n_lat_rows, t_lat):
    d = x.shape[1]
    n_batch = modtab.shape[0] - 1
    n_lat_tiles, tpb = n_lat_rows // tm, t_lat // tm
    row = pl.BlockSpec((tm, d), lambda i: (i, 0))
    wsp = pl.BlockSpec((d, d), lambda i: (0, 0))
    return pl.pallas_call(
        _merge_kernel,
        grid=(n_rows // tm,),
        in_specs=[row, row, row, pl.BlockSpec((tm, 3 * d), lambda i: (i, Z_GATE // (3 * d))), row,
                  pl.BlockSpec((1, 6, d), lambda i: (_mod_index(i, n_lat_tiles, tpb, n_batch), 0, 0)),
                  wsp, wsp, wsp, wsp],
        out_specs=row,
        out_shape=jax.ShapeDtypeStruct(x.shape, F32),
        scratch_shapes=[pltpu.VMEM((tm, d), BF16)],
        input_output_aliases={4: 0},
        compiler_params=_params(),
        name="merge_out_proj",
    )(ya, yb, yc, z, x, modtab, wa, wb, wc, wo)


def _ffn_kernel(x_ref, mod_ref, g_ref, w1_ref, w3_ref, w2_ref, xo_ref, acc_scr, *, f_chunks):
    h = _norm_mod(x_ref[...], g_ref[...], mod_ref[0, 3:4, :], mod_ref[0, 4:5, :]).astype(BF16)
    for n, (c0, cw) in enumerate(f_chunks):
        a = jnp.dot(h, w1_ref[:, c0:c0 + cw], preferred_element_type=F32)
        b = jnp.dot(h, w3_ref[:, c0:c0 + cw], preferred_element_type=F32)
        u = (_silu(a) * b).astype(BF16)
        o = jnp.dot(u, w2_ref[c0:c0 + cw, :], preferred_element_type=F32)
        if n == 0:
            acc_scr[...] = o
        else:
            acc_scr[...] += o
    xo_ref[...] = x_ref[...] + mod_ref[0, 5:6, :] * acc_scr[...]


def _dense_ffn(x, modtab, g2, w1, w3, w2, tm, n_rows, n_lat_rows, t_lat):
    d, f = w1.shape
    n_batch = modtab.shape[0] - 1
    n_lat_tiles, tpb = n_lat_rows // tm, t_lat // tm
    row = pl.BlockSpec((tm, d), lambda i: (i, 0))
    return pl.pallas_call(
        functools.partial(_ffn_kernel, f_chunks=_chunks(f, 256)),
        grid=(n_rows // tm,),
        in_specs=[row, pl.BlockSpec((1, 6, d), lambda i: (_mod_index(i, n_lat_tiles, tpb, n_batch), 0, 0)),
                  pl.BlockSpec((1, d), lambda i: (0, 0)),
                  pl.BlockSpec((d, f), lambda i: (0, 0)), pl.BlockSpec((d, f), lambda i: (0, 0)),
                  pl.BlockSpec((f, d), lambda i: (0, 0))],
        out_specs=row,
        out_shape=jax.ShapeDtypeStruct(x.shape, F32),
        scratch_shapes=[pltpu.VMEM((tm, d), F32)],
        input_output_aliases={0: 0},
        compiler_params=_params(),
        name="dense_ffn",
    )(x, modtab, g2, w1, w3, w2)


R_E1, R_E2, R_R1, R_R2, R_P1, R_P2 = range(6)


def _router_kernel(x_ref, mod_ref, g_ref, rw_ref, h_ref, r_ref, cnt_ref, carry_scr):
    tm = x_ref.shape[0]

    @pl.when(pl.program_id(0) == 0)
    def _():
        carry_scr[...] = jnp.zeros_like(carry_scr)

    h = _norm_mod(x_ref[...], g_ref[...], mod_ref[0, 3:4, :], mod_ref[0, 4:5, :])
    h_ref[...] = h
    logits = jnp.dot(h, rw_ref[...], preferred_element_type=F32, precision=lax.Precision.HIGHEST)
    lane = lax.broadcasted_iota(jnp.int32, (tm, LANES), 1).astype(F32)
    logits = jnp.where(lane < N_EXPERTS, logits, -jnp.inf)
    v1 = jnp.max(logits, axis=-1, keepdims=True)
    e1 = jnp.min(jnp.where(logits == v1, lane, float(LANES)), axis=-1, keepdims=True)
    rest = jnp.where(lane == e1, -jnp.inf, logits)
    v2 = jnp.max(rest, axis=-1, keepdims=True)
    e2 = jnp.min(jnp.where(rest == v2, lane, float(LANES)), axis=-1, keepdims=True)
    t = jnp.exp(v2 - v1)
    p1 = 1.0 / (1.0 + t)
    p2 = t / (1.0 + t)
    hot1 = lane == e1
    hot2 = lane == e2
    hot = jnp.where(hot1 | hot2, 1.0, 0.0)
    rr = lax.broadcasted_iota(jnp.int32, (tm, tm), 0)
    cc = lax.broadcasted_iota(jnp.int32, (tm, tm), 1)
    before = jnp.where(cc < rr, 1.0, 0.0).astype(BF16)
    prefix = jnp.dot(before, hot.astype(BF16), preferred_element_type=F32) + carry_scr[...]
    r1 = jnp.sum(jnp.where(hot1, prefix, 0.0), axis=-1, keepdims=True)
    r2 = jnp.sum(jnp.where(hot2, prefix, 0.0), axis=-1, keepdims=True)
    carry_scr[...] += jnp.sum(hot, axis=0, keepdims=True)
    cnt_ref[...] = carry_scr[...]
    out = jnp.zeros((tm, LANES), F32)
    for idx, val in ((R_E1, e1), (R_E2, e2), (R_R1, r1), (R_R2, r2), (R_P1, p1), (R_P2, p2)):
        out = jnp.where(lane == float(idx), val, out)
    r_ref[...] = out


def _router(x, modtab, g2, rw, tm, n_rows, t_lat):
    d = x.shape[1]
    tpb = t_lat // tm
    return pl.pallas_call(
        _router_kernel,
        grid=(n_rows // tm,),
        in_specs=[pl.BlockSpec((tm, d), lambda i: (i, 0)), pl.BlockSpec((1, 6, d), lambda i: (i // tpb, 0, 0)),
                  pl.BlockSpec((1, d), lambda i: (0, 0)), pl.BlockSpec((d, LANES), lambda i: (0, 0))],
        out_specs=[pl.BlockSpec((tm, d), lambda i: (i, 0)), pl.BlockSpec((tm, LANES), lambda i: (i, 0)),
                   pl.BlockSpec((1, LANES), lambda i: (0, 0))],
        out_shape=[jax.ShapeDtypeStruct((n_rows, d), F32), jax.ShapeDtypeStruct((n_rows, LANES), F32),
                   jax.ShapeDtypeStruct((1, LANES), F32)],
        scratch_shapes=[pltpu.VMEM((1, LANES), F32)],
        compiler_params=_params(),
        name="moe_router",
    )(x, modtab, g2, rw)


def _expert_kernel(te_ref, nv_ref, src_ref, dst_ref, h_hbm, w1_ref, w3_ref, w2_ref, y_hbm,
                   xg_scr, xb_scr, acc_scr, gsem, ssem, *, tme, n_f):
    i, f = pl.program_id(0), pl.program_id(1)
    valid = i < nv_ref[0]

    def gather_copy(r):
        tok = src_ref[0, 0, r]
        return pltpu.make_async_copy(h_hbm.at[pl.ds(tok, 1), :], xg_scr.at[pl.ds(r, 1), :], gsem)

    def scatter_copy(r, dst):
        return pltpu.make_async_copy(acc_scr.at[pl.ds(r, 1), :], y_hbm.at[pl.ds(dst, 1), :], ssem)

    @pl.when(valid & (f == 0))
    def _():
        def start(r, c):
            gather_copy(r).start()
            return c
        lax.fori_loop(0, tme, start, 0)

        def wait(r, c):
            gather_copy(r).wait()
            return c
        lax.fori_loop(0, tme, wait, 0)
        xb_scr[...] = xg_scr[...].astype(BF16)

    @pl.when(valid)
    def _():
        xb = xb_scr[...]
        a = jnp.dot(xb, w1_ref[0], preferred_element_type=F32)
        b = jnp.dot(xb, w3_ref[0], preferred_element_type=F32)
        o = jnp.dot((_silu(a) * b).astype(BF16), w2_ref[0], preferred_element_type=F32)

        @pl.when(f == 0)
        def _():
            acc_scr[...] = o

        @pl.when(f > 0)
        def _():
            acc_scr[...] += o

    @pl.when(valid & (f == n_f - 1))
    def _():
        def start(r, c):
            dst = dst_ref[0, 0, r]

            @pl.when(dst >= 0)
            def _():
                scatter_copy(r, dst).start()
            return c
        lax.fori_loop(0, tme, start, 0)

        def wait(r, c):
            dst = dst_ref[0, 0, r]

            @pl.when(dst >= 0)
            def _():
                scatter_copy(r, dst).wait()
            return c
        lax.fori_loop(0, tme, wait, 0)


def _experts(tile_expert, n_valid, src_tok, dst_row, h, w1, w3, w2, tme, tf, n_out_rows):
    ne, d, f = w1.shape
    n_tiles = src_tok.shape[0] // tme
    n_f = f // tf
    idx_spec = pl.BlockSpec((1, 1, tme), lambda i, j, te, nv: (i, 0, 0), memory_space=pltpu.SMEM)
    return pl.pallas_call(
        functools.partial(_expert_kernel, tme=tme, n_f=n_f),
        grid_spec=pltpu.PrefetchScalarGridSpec(
            num_scalar_prefetch=2,
            grid=(n_tiles, n_f),
            in_specs=[idx_spec, idx_spec,
                      pl.BlockSpec(memory_space=pl.ANY),
                      pl.BlockSpec((1, d, tf), lambda i, j, te, nv: (te[i], 0, j)),
                      pl.BlockSpec((1, d, tf), lambda i, j, te, nv: (te[i], 0, j)),
                      pl.BlockSpec((1, tf, d), lambda i, j, te, nv: (te[i], j, 0))],
            out_specs=pl.BlockSpec(memory_space=pl.ANY),
            scratch_shapes=[pltpu.VMEM((tme, d), F32), pltpu.VMEM((tme, d), BF16), pltpu.VMEM((tme, d), F32),
                            pltpu.SemaphoreType.DMA(()), pltpu.SemaphoreType.DMA(())]),
        out_shape=jax.ShapeDtypeStruct((n_out_rows, d), F32),
        compiler_params=_params(),
        name="moe_experts",
    )(tile_expert, n_valid, src_tok.reshape(n_tiles, 1, tme), dst_row.reshape(n_tiles, 1, tme), h, w1, w3, w2)


def _combine_kernel(x_ref, y_ref, r_ref, mod_ref, xo_ref):
    d = x_ref.shape[1]
    r = r_ref[...]
    p1 = r[:, R_P1:R_P1 + 1]
    p2 = r[:, R_P2:R_P2 + 1]
    xo_ref[...] = x_ref[...] + mod_ref[0, 5:6, :] * (p1 * y_ref[:, 0:d] + p2 * y_ref[:, d:2 * d])


def _combine(x, y2, r, modtab, tm, n_rows, t_lat):
    d = x.shape[1]
    tpb = t_lat // tm
    return pl.pallas_call(
        _combine_kernel,
        grid=(n_rows // tm,),
        in_specs=[pl.BlockSpec((tm, d), lambda i: (i, 0)), pl.BlockSpec((tm, 2 * d), lambda i: (i, 0)),
                  pl.BlockSpec((tm, LANES), lambda i: (i, 0)), pl.BlockSpec((1, 6, d), lambda i: (i // tpb, 0, 0))],
        out_specs=pl.BlockSpec((tm, d), lambda i: (i, 0)),
        out_shape=jax.ShapeDtypeStruct((n_rows, d), F32),
        compiler_params=_params(),
        name="moe_combine",
    )(x, y2, r, modtab)


def _moe(x, modtab, g2, router_w, w1, w3, w2, n_rows, t_lat, tme=512, tf=512):
    d = x.shape[1]
    rw = jnp.zeros((d, LANES), F32).at[:, :N_EXPERTS].set(router_w)
    h, r, cnt = _router(x, modtab, g2, rw, 512, n_rows, t_lat)
    e1, e2 = r[:, R_E1].astype(jnp.int32), r[:, R_E2].astype(jnp.int32)
    r1, r2 = r[:, R_R1].astype(jnp.int32), r[:, R_R2].astype(jnp.int32)
    counts = cnt[0, :N_EXPERTS].astype(jnp.int32)
    padded = (counts + tme - 1) // tme * tme
    ends = jnp.cumsum(padded)
    offs = ends - padded
    n_sorted = 2 * n_rows + N_EXPERTS * tme
    tok = jnp.arange(n_rows, dtype=jnp.int32)
    d1, d2 = offs[e1] + r1, offs[e2] + r2
    src_tok = jnp.zeros((n_sorted,), jnp.int32).at[d1].set(tok).at[d2].set(tok)
    dst_row = jnp.full((n_sorted,), -1, jnp.int32).at[d1].set(2 * tok).at[d2].set(2 * tok + 1)
    tile_start = jnp.arange(n_sorted // tme, dtype=jnp.int32) * tme
    n_valid = (ends[-1] // tme).astype(jnp.int32).reshape(1)
    tile_expert = jnp.minimum(jnp.sum(tile_start[:, None] >= ends[None, :], axis=1), N_EXPERTS - 1).astype(jnp.int32)
    tile_expert = jnp.where(tile_start < ends[-1], tile_expert, tile_expert[jnp.maximum(n_valid[0] - 1, 0)])
    y = _experts(tile_expert, n_valid, src_tok, dst_row, h, w1, w3, w2, tme, tf, 2 * n_rows)
    return _combine(x, y.reshape(n_rows, 2 * d), r, modtab, 512, n_rows, t_lat)


def _rope_tables(t_lat, pad_rows):
    rows = t_lat // GRID_W
    pos_r = jnp.repeat(jnp.arange(rows, dtype=F32), GRID_W)
    pos_c = jnp.tile(jnp.arange(GRID_W, dtype=F32), rows)
    inv = 1.0 / (ROPE_BASE ** (jnp.arange(ROPE_FREQS, dtype=F32) * 2.0 / (A_HEAD_DIM // 2)))
    ar, ac = pos_r[:, None] * inv, pos_c[:, None] * inv
    cos = jnp.concatenate([jnp.cos(ar), jnp.cos(ar), jnp.cos(ac), jnp.cos(ac)], axis=1)
    sin = jnp.concatenate([-jnp.sin(ar), jnp.sin(ar), -jnp.sin(ac), jnp.sin(ac)], axis=1)
    cos = jnp.concatenate([jnp.tile(cos, (1, A_KV_HEADS)), jnp.ones((pad_rows, A_KV_HEADS * A_HEAD_DIM), F32)])
    sin = jnp.concatenate([jnp.tile(sin, (1, A_KV_HEADS)), jnp.zeros((pad_rows, A_KV_HEADS * A_HEAD_DIM), F32)])
    return cos, sin


def _split_w_in(w, b):
    o = [0, 512, 1024, 2048, 3072, 3088, 5136, 6160, 6416, 6672, 9744]
    order = [(o[0], o[4]), (o[5], o[6]), (o[6], o[7]), (o[9], o[10]), (o[7], o[8]), (o[8], o[9])]
    w_z = jnp.concatenate([w[:, a:e] for a, e in order], axis=1).astype(BF16)
    b_z = jnp.concatenate([b[a:e] for a, e in order]).reshape(1, NZ)
    ng = 4 * M_HEADS
    w_g = jnp.zeros((w.shape[0], LANES), BF16).at[:, :ng].set(w[:, o[4]:o[5]].astype(BF16))
    b_g = jnp.zeros((1, LANES), F32).at[0, :ng].set(b[o[4]:o[5]])
    return w_z, b_z, w_g, b_g


def kernel(x, c, ctx, c_ctx, w_mod, b_mod, norm1_g, norm2_g, w_in, b_in, m_norm_g, conv_w, conv_b, conv_ln_g,
           conv_ln_b, q_norm_g, k_norm_g, attn_sink, w_branch_a, w_branch_b, w_branch_c, w_out, ffn_w1, ffn_w3,
           ffn_w2, router_w, moe_w1, moe_w3, moe_w2):
    n_batch, t_lat, d = x.shape
    t_ctx = ctx.shape[1]
    depth = w_mod.shape[0]
    n_lat, n_ctx = n_batch * t_lat, n_batch * t_ctx
    m = n_lat + n_ctx
    tm = min(1024, n_ctx)
    assert t_lat % tm == 0 and n_ctx % tm == 0 and t_lat >= 3 * A_BLOCK

    xs = jnp.concatenate([x.reshape(n_lat, d), ctx.reshape(n_ctx, d)], axis=0)
    cvec = jnp.zeros((16, d), F32).at[:n_batch].set(c).at[n_batch].set(c_ctx)
    mod = _modulation(cvec, w_mod, b_mod)[:, :n_batch + 1].reshape(depth, n_batch + 1, 6, d)
    cos_tab, sin_tab = _rope_tables(t_lat, tm)

    for l in range(depth):
        last = l == depth - 1
        modtab = mod[l]
        w_z, b_z, w_g, b_g = _split_w_in(w_in[l], b_in[l])
        z, gates = _in_projection(xs, modtab, norm1_g[l].reshape(1, d), w_z, b_z, w_g, b_g, tm, n_lat, t_lat)

        aux = _gate_prep(gates, tm)[:, :4 * M_HEADS].reshape(m, 4, M_HEADS)
        auxc = jnp.pad(aux.transpose(2, 0, 1), ((0, 0), (0, 0), (0, 4)))
        auxr = jnp.pad(aux.transpose(2, 1, 0), ((0, 0), (0, 4), (0, 0)))
        kt = z[:, Z_MK:Z_MK + M_HEADS * M_QK_DIM].T
        ya_lat, ya_ctx = _mlstm(z, kt, auxc, auxr, m_norm_g[l], n_batch, t_lat, t_ctx)

        conv_args = (conv_w[l], conv_b[l], conv_ln_g[l], conv_ln_b[l])
        yb_lat = _conv_branch(z, 0, n_batch, t_lat, min(512, t_lat), *conv_args)

        qp, kp = _qk_prep(z, cos_tab, sin_tab, q_norm_g[l], k_norm_g[l], tm, n_lat, t_lat)
        vsrc = (z, Z_AV // (A_KV_HEADS * A_HEAD_DIM))
        yc_lat = _attention(attn_sink[l], qp, kp, vsrc, 0, n_batch, t_lat, t_ctx, n_lat, True)

        wa, wb, wc, wo = (w.astype(BF16) for w in (w_branch_a[l], w_branch_b[l], w_branch_c[l], w_out[l]))
        if not last:
            yb_ctx = _conv_branch(z, n_lat, n_batch, t_ctx, t_ctx, *conv_args)
            yc_ctx = _attention(attn_sink[l], qp, kp, vsrc, n_lat, n_batch, t_ctx, t_ctx, n_lat, False)
            ya = jnp.concatenate([ya_lat, ya_ctx], axis=0)
            yb = jnp.concatenate([yb_lat, yb_ctx], axis=0)
            yc = jnp.concatenate([yc_lat, yc_ctx], axis=0)
            n_rows = m
        else:
            ya, yb, yc, n_rows = ya_lat, yb_lat, yc_lat, n_lat
        tmm = min(512, tm)
        xs = _merge(ya, yb, yc, z, xs, modtab, wa, wb, wc, wo, tmm, n_rows, n_lat, t_lat)

        g2 = norm2_g[l].reshape(1, d)
        if l % 2 == 0:
            i = l // 2
            xs = _dense_ffn(xs, modtab, g2, ffn_w1[i].astype(BF16), ffn_w3[i].astype(BF16), ffn_w2[i].astype(BF16),
                            tmm, n_rows, n_lat, t_lat)
        else:
            i = l // 2
            assert last, "expert mixer is implemented for latent rows only"
            xs = _moe(xs, modtab, g2, router_w[i], moe_w1[i].astype(BF16), moe_w3[i].astype(BF16),
                      moe_w2[i].astype(BF16), n_lat, t_lat)
    return xs[:n_lat].reshape(n_batch, t_lat, d)
```

```python
import functools

import jax
import jax.numpy as jnp
from jax import lax
from jax.experimental import pallas as pl
from jax.experimental.pallas import tpu as pltpu

F32 = jnp.float32
BF16 = jnp.bfloat16

EPS = 1e-6
GRID_W = 64
M_HEADS = 4
M_QK_DIM = 128
M_V_DIM = 256
M_CHUNK = 128
CONV_WIDTH = 31
CONV_HALO = 16
A_HEADS = 16
A_KV_HEADS = 4
A_GROUP = A_HEADS // A_KV_HEADS
A_HEAD_DIM = 64
A_BLOCK = 128
WINDOW = 128
ROPE_BASE = 10000.0
ROPE_FREQS = A_HEAD_DIM // 4
N_EXPERTS = 8
LANES = 128
VMEM_LIMIT = 56 * 1024 * 1024

Z_MQ, Z_MK, Z_MV, Z_MO = 0, 512, 1024, 2048
Z_CA, Z_CG, Z_AQ, Z_GATE, Z_AK, Z_AV = 3072, 4096, 5120, 6144, 9216, 9472
NZ = 9728


def _params(**kw):
    return pltpu.CompilerParams(vmem_limit_bytes=VMEM_LIMIT, **kw)


def _sigmoid(x):
    return 1.0 / (1.0 + jnp.exp(-x))


def _silu(x):
    return x * _sigmoid(x)


def _log_sigmoid(x):
    return jnp.minimum(x, 0.0) - jnp.log(1.0 + jnp.exp(-jnp.abs(x)))


def _norm_mod(x, g, shift, scale):
    ms = jnp.mean(x * x, axis=-1, keepdims=True)
    return (x * lax.rsqrt(ms + EPS) * g) * (1.0 + scale) + shift


def _chunks(n, w):
    out, c = [], 0
    while c < n:
        out.append((c, min(w, n - c)))
        c += w
    return out


def _mod_kernel(c_ref, w_ref, b_ref, o_ref):
    s = _silu(c_ref[...]).astype(BF16)
    o_ref[0] = jnp.dot(s, w_ref[0].astype(BF16), preferred_element_type=F32) + b_ref[0]


def _modulation(cvec, w_mod, b_mod):
    depth, d, n = w_mod.shape
    tn = 1536
    return pl.pallas_call(
        _mod_kernel,
        grid=(depth, n // tn),
        in_specs=[pl.BlockSpec((cvec.shape[0], d), lambda l, j: (0, 0)),
                  pl.BlockSpec((1, d, tn), lambda l, j: (l, 0, j)),
                  pl.BlockSpec((1, 1, tn), lambda l, j: (l, 0, j))],
        out_specs=pl.BlockSpec((1, cvec.shape[0], tn), lambda l, j: (l, 0, j)),
        out_shape=jax.ShapeDtypeStruct((depth, cvec.shape[0], n), F32),
        compiler_params=_params(),
        name="adaln_mod",
    )(cvec, w_mod, b_mod.reshape(depth, 1, n))


def _inproj_kernel(x_ref, mod_ref, g_ref, w_ref, b_ref, wg_ref, bg_ref, z_ref, gates_ref, h_scr, *, col_chunks):
    @pl.when(pl.program_id(1) == 0)
    def _():
        h = _norm_mod(x_ref[...], g_ref[...], mod_ref[0, 0:1, :], mod_ref[0, 1:2, :]).astype(BF16)
        h_scr[...] = h
        gates_ref[...] = jnp.dot(h, wg_ref[...], preferred_element_type=F32) + bg_ref[...]

    h = h_scr[...]
    for c0, cw in col_chunks:
        acc = jnp.dot(h, w_ref[:, c0:c0 + cw], preferred_element_type=F32) + b_ref[:, c0:c0 + cw]
        z_ref[:, c0:c0 + cw] = acc.astype(BF16)


def _mod_index(i, n_lat_tiles, tiles_per_batch, n_batch):
    return jnp.where(i < n_lat_tiles, i // tiles_per_batch, n_batch)


def _in_projection(x, modtab, g1, w_z, b_z, w_g, b_g, tm, n_lat_rows, t_lat):
    m, d = x.shape
    n_batch = modtab.shape[0] - 1
    tn = NZ // 4
    n_lat_tiles, tpb = n_lat_rows // tm, t_lat // tm
    mod_map = lambda i, j: (_mod_index(i, n_lat_tiles, tpb, n_batch), 0, 0)
    return pl.pallas_call(
        functools.partial(_inproj_kernel, col_chunks=_chunks(tn, 256)),
        grid=(m // tm, NZ // tn),
        in_specs=[pl.BlockSpec((tm, d), lambda i, j: (i, 0)),
                  pl.BlockSpec((1, 6, d), mod_map),
                  pl.BlockSpec((1, d), lambda i, j: (0, 0)),
                  pl.BlockSpec((d, tn), lambda i, j: (0, j)),
                  pl.BlockSpec((1, tn), lambda i, j: (0, j)),
                  pl.BlockSpec((d, LANES), lambda i, j: (0, 0)),
                  pl.BlockSpec((1, LANES), lambda i, j: (0, 0))],
        out_specs=[pl.BlockSpec((tm, tn), lambda i, j: (i, j)),
                   pl.BlockSpec((tm, LANES), lambda i, j: (i, 0))],
        out_shape=[jax.ShapeDtypeStruct((m, NZ), BF16), jax.ShapeDtypeStruct((m, LANES), F32)],
        scratch_shapes=[pltpu.VMEM((tm, d), BF16)],
        compiler_params=_params(),
        name="in_projection",
    )(x, modtab, g1, w_z, b_z, w_g, b_g)


def _gateprep_kernel(g_ref, o_ref, *, n_chunks):
    r = lax.broadcasted_iota(jnp.int32, (M_CHUNK, M_CHUNK), 0)
    c = lax.broadcasted_iota(jnp.int32, (M_CHUNK, M_CHUNK), 1)
    tri_lo = (c <= r).astype(F32)
    tri_up = (c >= r).astype(F32)
    lane = lax.broadcasted_iota(jnp.int32, (M_CHUNK, LANES), 1)
    ng = 4 * M_HEADS
    for n in range(n_chunks):
        g = g_ref[n * M_CHUNK:(n + 1) * M_CHUNK, :]
        lf = _log_sigmoid(g)
        cf = jnp.dot(tri_lo, lf, preferred_element_type=F32, precision=lax.Precision.HIGHEST)
        cb = jnp.dot(tri_up, lf, preferred_element_type=F32, precision=lax.Precision.HIGHEST)
        fwd_f = (lane >= M_HEADS) & (lane < 2 * M_HEADS)
        bwd_f = (lane >= 3 * M_HEADS) & (lane < 4 * M_HEADS)
        res = jnp.where(fwd_f, cf, jnp.where(bwd_f, cb, g))
        o_ref[:, n * M_CHUNK:(n + 1) * M_CHUNK] = res.T[0:ng, :]


def _gate_prep(gates, tm):
    m = gates.shape[0]
    ng = 4 * M_HEADS
    return pl.pallas_call(
        functools.partial(_gateprep_kernel, n_chunks=tm // M_CHUNK),
        grid=(m // tm,),
        in_specs=[pl.BlockSpec((tm, LANES), lambda i: (i, 0))],
        out_specs=pl.BlockSpec((ng, tm), lambda i: (0, i)),
        out_shape=jax.ShapeDtypeStruct((ng, m), F32),
        compiler_params=_params(),
        name="mlstm_gate_prep",
    )(gates)


def _mlstm_chunk(q, k, kt, v, i_col, b_col, i_row, b_row, btot, c_ref, n_ref, m_ref, mask):
    scale = M_QK_DIM ** -0.5
    m_prev = m_ref[...]
    c_prev = c_ref[...]
    n_prev = n_ref[...]
    g_col = b_col + m_prev
    dmat = jnp.where(mask, b_col - b_row + i_row, -jnp.inf)
    m_j = jnp.maximum(g_col, jnp.max(dmat, axis=-1, keepdims=True))
    inter = jnp.exp(g_col - m_j)
    s = jnp.dot(q, kt, preferred_element_type=F32) * scale * jnp.exp(dmat - m_j)
    qc = jnp.dot(q, c_prev.astype(BF16), preferred_element_type=F32) * scale
    num = inter * qc + jnp.dot(s.astype(BF16), v, preferred_element_type=F32)
    qn = jnp.sum(q.astype(F32) * n_prev, axis=-1, keepdims=True) * scale
    den = inter * qn + jnp.sum(s, axis=-1, keepdims=True)
    h = num / jnp.maximum(jnp.abs(den), jnp.exp(-m_j))

    a_col = btot - b_col + i_col
    a_row = btot - b_row + i_row
    m_loc = jnp.max(a_row, axis=-1, keepdims=True)
    w_col = jnp.exp(a_col - m_loc)
    w_row = jnp.exp(a_row - m_loc)
    m_new = jnp.maximum(btot + m_prev, m_loc)
    s_old = jnp.exp(btot + m_prev - m_new)
    s_new = jnp.exp(m_loc - m_new)
    c_loc = jnp.dot((kt.astype(F32) * w_row).astype(BF16), v, preferred_element_type=F32)
    n_loc = jnp.sum(k.astype(F32) * w_col, axis=0, keepdims=True)
    c_ref[...] = s_old * c_prev + s_new * c_loc
    n_ref[...] = s_old * n_prev + s_new * n_loc
    m_ref[...] = m_new
    return h


def _mlstm_kernel(ql, kl, ktl, vl, ol, arl, qc, kc, ktc, vc, oc, arc, ng_ref, out_l, out_c,
                  hf_l, hb_l, hf_c, hb_c, cf_ref, nf_ref, mf_ref, cb_ref, nb_ref, mb_ref, *, n_lat, n_ctx):
    L = M_CHUNK
    cf_ref[...] = jnp.zeros_like(cf_ref)
    nf_ref[...] = jnp.zeros_like(nf_ref)
    mf_ref[...] = jnp.full_like(mf_ref, -jnp.inf)
    cb_ref[...] = jnp.zeros_like(cb_ref)
    nb_ref[...] = jnp.zeros_like(nb_ref)
    mb_ref[...] = jnp.full_like(mb_ref, -jnp.inf)
    r = lax.broadcasted_iota(jnp.int32, (L, L), 0)
    c = lax.broadcasted_iota(jnp.int32, (L, L), 1)
    mask_f = c <= r
    mask_b = c >= r
    ng = ng_ref[0]

    def one(q_r, k_r, kt_r, v_r, ar_r, h_r, n, first, tot_lane, state, mask):
        rows = pl.ds(pl.multiple_of(n * L, L), L)
        ar = ar_r[0, :, rows]
        ac = ar.T
        i_row, b_row = ar[first:first + 1, :], ar[first + 1:first + 2, :]
        h_r[rows, :] = _mlstm_chunk(q_r[rows, :], k_r[rows, :], kt_r[:, rows], v_r[rows, :],
                                    ac[:, first:first + 1], ac[:, first + 1:first + 2], i_row, b_row,
                                    b_row[:, tot_lane:tot_lane + 1], *state, mask)

    def pair(q_r, k_r, kt_r, v_r, ar_r, hf_r, hb_r, n_f, n_b):
        one(q_r, k_r, kt_r, v_r, ar_r, hf_r, n_f, 0, L - 1, (cf_ref, nf_ref, mf_ref), mask_f)
        one(q_r, k_r, kt_r, v_r, ar_r, hb_r, n_b, 2, 0, (cb_ref, nb_ref, mb_ref), mask_b)

    def finish(hf_r, hb_r, o_r, out_r, n):
        rows = pl.ds(pl.multiple_of(n * L, L), L)
        h = hf_r[rows, :] + hb_r[rows, :]
        y = h * lax.rsqrt(jnp.mean(h * h, axis=-1, keepdims=True) + EPS) * ng
        out_r[rows, :] = (y * _sigmoid(o_r[rows, :].astype(F32))).astype(out_r.dtype)

    for n in range(n_ctx):
        pair(qc, kc, ktc, vc, arc, hf_c, hb_c, n, n_ctx - 1 - n)

    def body(t, carry):
        pair(ql, kl, ktl, vl, arl, hf_l, hb_l, t, n_lat - 1 - t)
        return carry
    lax.fori_loop(0, n_lat, body, 0)

    for n in range(n_ctx):
        finish(hf_c, hb_c, oc, out_c, n)

    def fin_body(t, carry):
        finish(hf_l, hb_l, ol, out_l, t)
        return carry
    lax.fori_loop(0, n_lat, fin_body, 0)


def _mlstm(z, kt, auxr, norm_g, n_batch, t_lat, t_ctx):
    dk, dv, nh = M_QK_DIM, M_V_DIM, M_HEADS
    cb = n_batch * t_lat // t_ctx
    lat = lambda w, off: pl.BlockSpec((t_lat, w), lambda b, h: (b, off // w + h))
    ctx = lambda w, off: pl.BlockSpec((t_ctx, w), lambda b, h: (cb + b, off // w + h))
    in_specs = [
        lat(dk, Z_MQ), lat(dk, Z_MK), pl.BlockSpec((dk, t_lat), lambda b, h: (h, b)), lat(dv, Z_MV), lat(dv, Z_MO),
        pl.BlockSpec((1, 8, t_lat), lambda b, h: (h, 0, b)),
        ctx(dk, Z_MQ), ctx(dk, Z_MK), pl.BlockSpec((dk, t_ctx), lambda b, h: (h, cb + b)), ctx(dv, Z_MV),
        ctx(dv, Z_MO),
        pl.BlockSpec((1, 8, t_ctx), lambda b, h: (h, 0, cb + b)),
        pl.BlockSpec((1, 1, dv), lambda b, h: (h, 0, 0)),
    ]
    out_specs = [pl.BlockSpec((t_lat, dv), lambda b, h: (b, h)), pl.BlockSpec((t_ctx, dv), lambda b, h: (b, h))]
    out_shape = [jax.ShapeDtypeStruct((n_batch * t_lat, nh * dv), BF16),
                 jax.ShapeDtypeStruct((n_batch * t_ctx, nh * dv), BF16)]
    scratch = [pltpu.VMEM((t_lat, dv), F32), pltpu.VMEM((t_lat, dv), F32),
               pltpu.VMEM((t_ctx, dv), F32), pltpu.VMEM((t_ctx, dv), F32),
               pltpu.VMEM((dk, dv), F32), pltpu.VMEM((1, dk), F32), pltpu.VMEM((1, 1), F32),
               pltpu.VMEM((dk, dv), F32), pltpu.VMEM((1, dk), F32), pltpu.VMEM((1, 1), F32)]
    return pl.pallas_call(
        functools.partial(_mlstm_kernel, n_lat=t_lat // M_CHUNK, n_ctx=t_ctx // M_CHUNK),
        grid=(n_batch, nh),
        in_specs=in_specs, out_specs=out_specs, out_shape=out_shape, scratch_shapes=scratch,
        compiler_params=_params(),
        name="mlstm",
    )(z, z, kt, z, z, auxr, z, z, kt, z, z, auxr, norm_g.reshape(nh, 1, dv))


def _conv_kernel(a_ref, g_ref, ap_ref, gp_ref, an_ref, gn_ref, w_ref, b_ref, lg_ref, lb_ref, o_ref, y_scr, c_scr,
                 sh_scr, *, tt, tiles_per_seq, row_sub, col_w):
    i = pl.program_id(1)
    hl = CONV_HALO
    glu = lambda a, g: a.astype(F32) * _sigmoid(g.astype(F32))
    prev_ok = (i > 0).astype(F32)
    next_ok = (i < tiles_per_seq - 1).astype(F32)
    y_scr[0:hl, :] = glu(ap_ref[...], gp_ref[...]) * prev_ok
    y_scr[hl:hl + tt, :] = glu(a_ref[...], g_ref[...])
    y_scr[hl + tt:hl + tt + hl, :] = glu(an_ref[...], gn_ref[...]) * next_ok
    d = a_ref.shape[1]
    sub = 8
    taps = [[j for j in range(CONV_WIDTH) if (j + hl - CONV_WIDTH // 2) % sub == r] for r in range(sub)]
    span = row_sub + 2 * hl

    def col_body(cb, carry):
        cols = pl.ds(pl.multiple_of(cb * col_w, col_w), col_w)
        wc = w_ref[:, cols]
        for n, r0 in enumerate(range(0, tt, row_sub)):
            acc = jnp.zeros((row_sub, col_w), F32)
            for r in range(sub):
                slot = (n % 2) * sub + r
                sh_scr[slot] = y_scr[r0 + r:r0 + r + span - sub, cols]
                for j in taps[r]:
                    a0 = j + hl - CONV_WIDTH // 2 - r
                    acc = acc + sh_scr[slot, a0:a0 + row_sub, :] * wc[j:j + 1, :]
            c_scr[r0:r0 + row_sub, cols] = acc
        return carry
    lax.fori_loop(0, d // col_w, col_body, 0)
    y = c_scr[...] + b_ref[...]
    mu = jnp.mean(y, axis=-1, keepdims=True)
    yc = y - mu
    var = jnp.mean(yc * yc, axis=-1, keepdims=True)
    o_ref[...] = _silu(yc * lax.rsqrt(var + EPS) * lg_ref[...] + lb_ref[...]).astype(o_ref.dtype)


def _conv_branch(z, row0, n_seq, t_seq, tt, conv_w, conv_b, ln_g, ln_b):
    d = conv_w.shape[1]
    hl = CONV_HALO
    tps = t_seq // tt
    rb0, hb0, hps = row0 // tt, row0 // hl, t_seq // hl
    ca, cg = Z_CA // d, Z_CG // d
    cur = lambda col: pl.BlockSpec((tt, d), lambda s, i: (rb0 + s * tps + i, col))
    prv = lambda col: pl.BlockSpec(
        (hl, d), lambda s, i: (hb0 + s * hps + jnp.maximum(i * (tt // hl) - 1, 0), col))
    nxt = lambda col: pl.BlockSpec(
        (hl, d), lambda s, i: (hb0 + s * hps + jnp.minimum((i + 1) * (tt // hl), hps - 1), col))
    vec = pl.BlockSpec((1, d), lambda s, i: (0, 0))
    wpad = jnp.zeros((32, d), F32).at[:CONV_WIDTH].set(conv_w)
    row_sub, col_w = 128, LANES
    return pl.pallas_call(
        functools.partial(_conv_kernel, tt=tt, tiles_per_seq=tps, row_sub=row_sub, col_w=col_w),
        grid=(n_seq, tps),
        in_specs=[cur(ca), cur(cg), prv(ca), prv(cg), nxt(ca), nxt(cg),
                  pl.BlockSpec((32, d), lambda s, i: (0, 0)), vec, vec, vec],
        out_specs=pl.BlockSpec((tt, d), lambda s, i: (s * tps + i, 0)),
        out_shape=jax.ShapeDtypeStruct((n_seq * t_seq, d), BF16),
        scratch_shapes=[pltpu.VMEM((tt + 2 * hl, d), F32), pltpu.VMEM((tt, d), F32),
                        pltpu.VMEM((16, row_sub + 2 * hl - 8, col_w), F32)],
        compiler_params=_params(),
        name="conv_branch",
    )(z, z, z, z, z, z, wpad, conv_b.reshape(1, d), ln_g.reshape(1, d), ln_b.reshape(1, d))


def _headnorm_rope(x, bd, gain, cos, sin):
    x2 = x * x
    hi = x2.astype(BF16)
    lo = (x2 - hi.astype(F32)).astype(BF16)
    ss = jnp.dot(hi, bd, preferred_element_type=F32) + jnp.dot(lo, bd, preferred_element_type=F32)
    xn = x * lax.rsqrt(ss * (1.0 / A_HEAD_DIM) + EPS) * gain
    w = x.shape[1]
    lane = lax.broadcasted_iota(jnp.int32, x.shape, 1)
    first = (lane % (2 * ROPE_FREQS)) < ROPE_FREQS
    swapped = jnp.where(first, pltpu.roll(xn, w - ROPE_FREQS, 1), pltpu.roll(xn, ROPE_FREQS, 1))
    return xn * cos + swapped * sin


def _qkprep_kernel(q_ref, k_ref, cos_ref, sin_ref, qg_ref, kg_ref, qo_ref, ko_ref):
    gw = A_GROUP * A_HEAD_DIM
    r = lax.broadcasted_iota(jnp.int32, (gw, gw), 0) // A_HEAD_DIM
    c = lax.broadcasted_iota(jnp.int32, (gw, gw), 1) // A_HEAD_DIM
    bd = (r == c).astype(BF16)
    cos, sin = cos_ref[...], sin_ref[...]
    for g in range(q_ref.shape[1] // gw):
        x = q_ref[:, g * gw:(g + 1) * gw].astype(F32)
        qo_ref[:, g * gw:(g + 1) * gw] = _headnorm_rope(x, bd, qg_ref[...], cos, sin).astype(qo_ref.dtype)
    ko_ref[...] = _headnorm_rope(k_ref[...].astype(F32), bd, kg_ref[...], cos, sin).astype(ko_ref.dtype)


def _qk_prep(z, cos_tab, sin_tab, q_g, k_g, tm, n_lat_rows, t_lat):
    m = z.shape[0]
    qw, kw = A_HEADS * A_HEAD_DIM, A_KV_HEADS * A_HEAD_DIM
    n_lat_tiles, tpb = n_lat_rows // tm, t_lat // tm
    tab = pl.BlockSpec((tm, kw), lambda i: (jnp.where(i < n_lat_tiles, i % tpb, tpb), 0))
    vec = pl.BlockSpec((1, kw), lambda i: (0, 0))
    return pl.pallas_call(
        _qkprep_kernel,
        grid=(m // tm,),
        in_specs=[pl.BlockSpec((tm, qw), lambda i: (i, Z_AQ // qw)),
                  pl.BlockSpec((tm, kw), lambda i: (i, Z_AK // kw)), tab, tab, vec, vec],
        out_specs=[pl.BlockSpec((tm, qw), lambda i: (i, 0)), pl.BlockSpec((tm, kw), lambda i: (i, 0))],
        out_shape=[jax.ShapeDtypeStruct((m, qw), BF16), jax.ShapeDtypeStruct((m, kw), BF16)],
        compiler_params=_params(),
        name="attn_qk_prep",
    )(z, z, cos_tab, sin_tab, jnp.tile(q_g, A_KV_HEADS).reshape(1, kw), jnp.tile(k_g, A_KV_HEADS).reshape(1, kw))


def _attn_kernel(sink_ref, q_ref, k_ref, v_ref, kc_ref, vc_ref, o_ref, *, local, t_seq):
    nq = pl.program_id(1)
    blk, dh, grp = A_BLOCK, A_HEAD_DIM, A_GROUP
    scale = dh ** -0.5
    nt = (((1,), (1,)), ((), ()))
    if local:
        win = 3 * blk
        start = pl.multiple_of(jnp.clip(nq * blk - blk, 0, t_seq - win), blk)
        qpos = nq * blk + lax.broadcasted_iota(jnp.int32, (grp * blk, win), 0) % blk
        kpos = start + lax.broadcasted_iota(jnp.int32, (grp * blk, win), 1)
        near = jnp.abs(kpos - qpos) <= WINDOW
    rowgrp = lax.broadcasted_iota(jnp.int32, (grp * blk, 1), 0) // blk
    outs = []
    for h in range(A_KV_HEADS):
        cols = slice(h * dh, (h + 1) * dh)
        qs = jnp.concatenate([q_ref[:, (h * grp + g) * dh:(h * grp + g + 1) * dh] for g in range(grp)], axis=0)
        sink = jnp.zeros((grp * blk, 1), F32)
        for g in range(grp):
            sink = jnp.where(rowgrp == g, sink_ref[h * grp + g], sink)
        s_ctx = lax.dot_general(qs, kc_ref[:, cols], nt, preferred_element_type=F32) * scale
        mx = jnp.maximum(jnp.max(s_ctx, axis=-1, keepdims=True), sink)
        if local:
            kw = k_ref[pl.ds(start, win), cols]
            vw = v_ref[pl.ds(start, win), cols]
            s_loc = lax.dot_general(qs, kw, nt, preferred_element_type=F32) * scale
            s_loc = jnp.where(near, s_loc, -jnp.inf)
            mx = jnp.maximum(mx, jnp.max(s_loc, axis=-1, keepdims=True))
        p_ctx = jnp.exp(s_ctx - mx)
        den = jnp.sum(p_ctx, axis=-1, keepdims=True) + jnp.exp(sink - mx)
        if local:
            p_loc = jnp.exp(s_loc - mx)
            den = den + jnp.sum(p_loc, axis=-1, keepdims=True)
        inv = 1.0 / den
        o = jnp.dot((p_ctx * inv).astype(BF16), vc_ref[:, cols], preferred_element_type=F32)
        if local:
            o = o + jnp.dot((p_loc * inv).astype(BF16), vw, preferred_element_type=F32)
        outs.extend(o[g * blk:(g + 1) * blk, :] for g in range(grp))
    o_ref[...] = jnp.concatenate(outs, axis=1).astype(o_ref.dtype)


def _attention(sink, q, k, v, q_row0, n_batch, t_seq, t_ctx, ctx_row0, local):
    qw, kw = A_HEADS * A_HEAD_DIM, A_KV_HEADS * A_HEAD_DIM
    nqb = t_seq // A_BLOCK
    qb0, sb0, cb0 = q_row0 // A_BLOCK, q_row0 // t_seq, ctx_row0 // t_ctx
    zv, vcol = v
    return pl.pallas_call(
        functools.partial(_attn_kernel, local=local, t_seq=t_seq),
        grid_spec=pltpu.PrefetchScalarGridSpec(
            num_scalar_prefetch=1,
            grid=(n_batch, nqb),
            in_specs=[pl.BlockSpec((A_BLOCK, qw), lambda b, n, s: (qb0 + b * nqb + n, 0)),
                      pl.BlockSpec((t_seq, kw), lambda b, n, s: (sb0 + b, 0)),
                      pl.BlockSpec((t_seq, kw), lambda b, n, s: (sb0 + b, vcol)),
                      pl.BlockSpec((t_ctx, kw), lambda b, n, s: (cb0 + b, 0)),
                      pl.BlockSpec((t_ctx, kw), lambda b, n, s: (cb0 + b, vcol))],
            out_specs=pl.BlockSpec((A_BLOCK, qw), lambda b, n, s: (b * nqb + n, 0))),
        out_shape=jax.ShapeDtypeStruct((n_batch * t_seq, qw), BF16),
        compiler_params=_params(),
        name="attention_local" if local else "attention_ctx",
    )(sink, q, k, zv, k, zv)


def _merge_kernel(ya_ref, yb_ref, yc_ref, gl_ref, x_ref, mod_ref, wa_ref, wb_ref, wc_ref, wo_ref, xo_ref, m_scr):
    d = x_ref.shape[1]
    ya, yb, yc = ya_ref[...], yb_ref[...], yc_ref[...]
    for c0, cw in _chunks(d, 256):
        cs = slice(c0, c0 + cw)
        acc = _sigmoid(gl_ref[:, c0:c0 + cw].astype(F32)) * jnp.dot(ya, wa_ref[:, cs], preferred_element_type=F32)
        acc = acc + _sigmoid(gl_ref[:, d + c0:d + c0 + cw].astype(F32)) * jnp.dot(
            yb, wb_ref[:, cs], preferred_element_type=F32)
        acc = acc + _sigmoid(gl_ref[:, 2 * d + c0:2 * d + c0 + cw].astype(F32)) * jnp.dot(
            yc, wc_ref[:, cs], preferred_element_type=F32)
        m_scr[:, cs] = acc.astype(BF16)
    mm = m_scr[...]
    for c0, cw in _chunks(d, 256):
        cs = slice(c0, c0 + cw)
        xo_ref[:, cs] = x_ref[:, cs] + mod_ref[0, 2:3, cs] * jnp.dot(mm, wo_ref[:, cs], preferred_element_type=F32)


def _merge(ya, yb, yc, z, x, modtab, wa, wb, wc, wo, tm, n_rows, n_lat_rows, t_lat):
    d = x.shape[1]
    n_batch = modtab.shape[0] - 1
    n_lat_tiles, tpb = n_lat_rows // tm, t_lat // tm
    row = pl.BlockSpec((tm, d), lambda i: (i, 0))
    wsp = pl.BlockSpec((d, d), lambda i: (0, 0))
    return pl.pallas_call(
        _merge_kernel,
        grid=(n_rows // tm,),
        in_specs=[row, row, row, pl.BlockSpec((tm, 3 * d), lambda i: (i, Z_GATE // (3 * d))), row,
                  pl.BlockSpec((1, 6, d), lambda i: (_mod_index(i, n_lat_tiles, tpb, n_batch), 0, 0)),
                  wsp, wsp, wsp, wsp],
        out_specs=row,
        out_shape=jax.ShapeDtypeStruct(x.shape, F32),
        scratch_shapes=[pltpu.VMEM((tm, d), BF16)],
        input_output_aliases={4: 0},
        compiler_params=_params(),
        name="merge_out_proj",
    )(ya, yb, yc, z, x, modtab, wa, wb, wc, wo)


def _ffn_kernel(x_ref, mod_ref, g_ref, w1_ref, w3_ref, w2_ref, xo_ref, acc_scr, *, f_chunks):
    h = _norm_mod(x_ref[...], g_ref[...], mod_ref[0, 3:4, :], mod_ref[0, 4:5, :]).astype(BF16)
    for n, (c0, cw) in enumerate(f_chunks):
        a = jnp.dot(h, w1_ref[:, c0:c0 + cw], preferred_element_type=F32)
        b = jnp.dot(h, w3_ref[:, c0:c0 + cw], preferred_element_type=F32)
        u = (_silu(a) * b).astype(BF16)
        o = jnp.dot(u, w2_ref[c0:c0 + cw, :], preferred_element_type=F32)
        if n == 0:
            acc_scr[...] = o
        else:
            acc_scr[...] += o
    xo_ref[...] = x_ref[...] + mod_ref[0, 5:6, :] * acc_scr[...]


def _dense_ffn(x, modtab, g2, w1, w3, w2, tm, n_rows, n_lat_rows, t_lat):
    d, f = w1.shape
    n_batch = modtab.shape[0] - 1
    n_lat_tiles, tpb = n_lat_rows // tm, t_lat // tm
    row = pl.BlockSpec((tm, d), lambda i: (i, 0))
    return pl.pallas_call(
        functools.partial(_ffn_kernel, f_chunks=_chunks(f, 256)),
        grid=(n_rows // tm,),
        in_specs=[row, pl.BlockSpec((1, 6, d), lambda i: (_mod_index(i, n_lat_tiles, tpb, n_batch), 0, 0)),
                  pl.BlockSpec((1, d), lambda i: (0, 0)),
                  pl.BlockSpec((d, f), lambda i: (0, 0)), pl.BlockSpec((d, f), lambda i: (0, 0)),
                  pl.BlockSpec((f, d), lambda i: (0, 0))],
        out_specs=row,
        out_shape=jax.ShapeDtypeStruct(x.shape, F32),
        scratch_shapes=[pltpu.VMEM((tm, d), F32)],
        input_output_aliases={0: 0},
        compiler_params=_params(),
        name="dense_ffn",
    )(x, modtab, g2, w1, w3, w2)


R_E1, R_E2, R_R1, R_R2, R_P1, R_P2 = range(6)
HI16 = 0xFFFF0000


def _pack_bf16_pair(lo, hi):
    lo_bits = lax.bitcast_convert_type(lo.astype(BF16).astype(F32), jnp.uint32)
    hi_bits = lax.bitcast_convert_type(hi.astype(BF16).astype(F32), jnp.uint32)
    return (hi_bits & jnp.uint32(HI16)) | (lo_bits >> 16)


def _unpack_bf16_pair(u):
    lo = lax.bitcast_convert_type(u << 16, F32).astype(BF16)
    hi = lax.bitcast_convert_type(u & jnp.uint32(HI16), F32).astype(BF16)
    return lo, hi


def _router_kernel(x_ref, mod_ref, g_ref, rw_ref, hp_ref, r_ref, cnt_ref, carry_scr):
    tm, d = x_ref.shape

    @pl.when(pl.program_id(0) == 0)
    def _():
        carry_scr[...] = jnp.zeros_like(carry_scr)

    h = _norm_mod(x_ref[...], g_ref[...], mod_ref[0, 3:4, :], mod_ref[0, 4:5, :])
    hp_ref[...] = _pack_bf16_pair(h[:, :d // 2], h[:, d // 2:])
    logits = jnp.dot(h, rw_ref[...], preferred_element_type=F32, precision=lax.Precision.HIGHEST)
    lane = lax.broadcasted_iota(jnp.int32, (tm, LANES), 1).astype(F32)
    logits = jnp.where(lane < N_EXPERTS, logits, -jnp.inf)
    v1 = jnp.max(logits, axis=-1, keepdims=True)
    e1 = jnp.min(jnp.where(logits == v1, lane, float(LANES)), axis=-1, keepdims=True)
    rest = jnp.where(lane == e1, -jnp.inf, logits)
    v2 = jnp.max(rest, axis=-1, keepdims=True)
    e2 = jnp.min(jnp.where(rest == v2, lane, float(LANES)), axis=-1, keepdims=True)
    t = jnp.exp(v2 - v1)
    p1 = 1.0 / (1.0 + t)
    p2 = t / (1.0 + t)
    hot1 = lane == e1
    hot2 = lane == e2
    hot = jnp.where(hot1 | hot2, 1.0, 0.0)
    rr = lax.broadcasted_iota(jnp.int32, (tm, tm), 0)
    cc = lax.broadcasted_iota(jnp.int32, (tm, tm), 1)
    before = jnp.where(cc < rr, 1.0, 0.0).astype(BF16)
    prefix = jnp.dot(before, hot.astype(BF16), preferred_element_type=F32) + carry_scr[...]
    r1 = jnp.sum(jnp.where(hot1, prefix, 0.0), axis=-1, keepdims=True)
    r2 = jnp.sum(jnp.where(hot2, prefix, 0.0), axis=-1, keepdims=True)
    carry_scr[...] += jnp.sum(hot, axis=0, keepdims=True)
    cnt_ref[...] = carry_scr[...]
    out = jnp.zeros((tm, LANES), F32)
    for idx, val in ((R_E1, e1), (R_E2, e2), (R_R1, r1), (R_R2, r2), (R_P1, p1), (R_P2, p2)):
        out = jnp.where(lane == float(idx), val, out)
    r_ref[...] = out


def _router(x, modtab, g2, rw, tm, n_rows, t_lat):
    d = x.shape[1]
    tpb = t_lat // tm
    return pl.pallas_call(
        _router_kernel,
        grid=(n_rows // tm,),
        in_specs=[pl.BlockSpec((tm, d), lambda i: (i, 0)), pl.BlockSpec((1, 6, d), lambda i: (i // tpb, 0, 0)),
                  pl.BlockSpec((1, d), lambda i: (0, 0)), pl.BlockSpec((d, LANES), lambda i: (0, 0))],
        out_specs=[pl.BlockSpec((tm, d // 2), lambda i: (i, 0)), pl.BlockSpec((tm, LANES), lambda i: (i, 0)),
                   pl.BlockSpec((1, LANES), lambda i: (0, 0))],
        out_shape=[jax.ShapeDtypeStruct((n_rows, d // 2), jnp.uint32), jax.ShapeDtypeStruct((n_rows, LANES), F32),
                   jax.ShapeDtypeStruct((1, LANES), F32)],
        scratch_shapes=[pltpu.VMEM((1, LANES), F32)],
        compiler_params=_params(),
        name="moe_router",
    )(x, modtab, g2, rw)


def _row_copies(n, make, unroll=8):
    def start(r, c):
        for cp in make(r):
            cp.start()
        return c
    lax.fori_loop(0, n, start, 0, unroll=unroll)

    def wait(r, c):
        for cp in make(r):
            cp.wait()
        return c
    lax.fori_loop(0, n, wait, 0, unroll=unroll)


def _dispatch_kernel(d1_ref, d2_ref, hp_ref, xs_in, xs_hbm, sem):
    del xs_in

    def make(r):
        src = hp_ref.at[pl.ds(r, 1), :]
        return (pltpu.make_async_copy(src, xs_hbm.at[pl.ds(d1_ref[0, 0, r], 1), :], sem),
                pltpu.make_async_copy(src, xs_hbm.at[pl.ds(d2_ref[0, 0, r], 1), :], sem))
    _row_copies(hp_ref.shape[0], make)


def _dispatch(d1, d2, hp, n_sorted, tm):
    n_rows, w = hp.shape
    idx = pl.BlockSpec((1, 1, tm), lambda i: (i, 0, 0), memory_space=pltpu.SMEM)
    return pl.pallas_call(
        _dispatch_kernel,
        grid=(n_rows // tm,),
        in_specs=[idx, idx, pl.BlockSpec((tm, w), lambda i: (i, 0)), pl.BlockSpec(memory_space=pl.ANY)],
        out_specs=pl.BlockSpec(memory_space=pl.ANY),
        out_shape=jax.ShapeDtypeStruct((n_sorted, w), jnp.uint32),
        scratch_shapes=[pltpu.SemaphoreType.DMA(())],
        input_output_aliases={3: 0},
        compiler_params=_params(),
        name="moe_dispatch",
    )(d1.reshape(-1, 1, tm), d2.reshape(-1, 1, tm), hp, jnp.zeros((n_sorted, w), jnp.uint32))


def _expert_kernel(te_ref, nv_ref, xs_ref, w1_ref, w3_ref, w2_ref, y_ref, xb_scr, acc_scr, *, n_f):
    i, f = pl.program_id(0), pl.program_id(1)
    valid = i < nv_ref[0]
    half = xs_ref.shape[1]

    @pl.when(valid & (f == 0))
    def _():
        lo, hi = _unpack_bf16_pair(xs_ref[...])
        xb_scr[:, :half] = lo
        xb_scr[:, half:] = hi

    @pl.when(valid)
    def _():
        xb = xb_scr[...]
        a = jnp.dot(xb, w1_ref[0], preferred_element_type=F32)
        b = jnp.dot(xb, w3_ref[0], preferred_element_type=F32)
        o = jnp.dot((_silu(a) * b).astype(BF16), w2_ref[0], preferred_element_type=F32)

        @pl.when(f == 0)
        def _():
            acc_scr[...] = o

        @pl.when((f > 0) & (f < n_f - 1))
        def _():
            acc_scr[...] += o

        @pl.when(f == n_f - 1)
        def _():
            y_ref[...] = acc_scr[...] + o

    @pl.when(jnp.logical_not(valid) & (f == n_f - 1))
    def _():
        y_ref[...] = jnp.zeros_like(y_ref)


def _experts(tile_expert, n_valid, xs, w1, w3, w2, tme, tf):
    ne, d, f = w1.shape
    n_tiles = xs.shape[0] // tme
    n_f = f // tf
    assert n_f >= 2
    return pl.pallas_call(
        functools.partial(_expert_kernel, n_f=n_f),
        grid_spec=pltpu.PrefetchScalarGridSpec(
            num_scalar_prefetch=2,
            grid=(n_tiles, n_f),
            in_specs=[pl.BlockSpec((tme, d // 2), lambda i, j, te, nv: (i, 0)),
                      pl.BlockSpec((1, d, tf), lambda i, j, te, nv: (te[i], 0, j)),
                      pl.BlockSpec((1, d, tf), lambda i, j, te, nv: (te[i], 0, j)),
                      pl.BlockSpec((1, tf, d), lambda i, j, te, nv: (te[i], j, 0))],
            out_specs=pl.BlockSpec((tme, d), lambda i, j, te, nv: (i, 0)),
            scratch_shapes=[pltpu.VMEM((tme, d), BF16), pltpu.VMEM((tme, d), F32)]),
        out_shape=jax.ShapeDtypeStruct((xs.shape[0], d), F32),
        compiler_params=_params(),
        name="moe_experts",
    )(tile_expert, n_valid, xs, w1, w3, w2)


def _combine_kernel(d1_ref, d2_ref, x_ref, r_ref, mod_ref, y_hbm, xo_ref, y1_scr, y2_scr, sem):
    def make(r):
        return (pltpu.make_async_copy(y_hbm.at[pl.ds(d1_ref[0, 0, r], 1), :], y1_scr.at[pl.ds(r, 1), :], sem),
                pltpu.make_async_copy(y_hbm.at[pl.ds(d2_ref[0, 0, r], 1), :], y2_scr.at[pl.ds(r, 1), :], sem))
    _row_copies(x_ref.shape[0], make)
    r = r_ref[...]
    p1 = r[:, R_P1:R_P1 + 1]
    p2 = r[:, R_P2:R_P2 + 1]
    xo_ref[...] = x_ref[...] + mod_ref[0, 5:6, :] * (p1 * y1_scr[...] + p2 * y2_scr[...])


def _combine(d1, d2, x, y, r, modtab, tm, n_rows, t_lat):
    d = x.shape[1]
    tpb = t_lat // tm
    idx = pl.BlockSpec((1, 1, tm), lambda i: (i, 0, 0), memory_space=pltpu.SMEM)
    return pl.pallas_call(
        _combine_kernel,
        grid=(n_rows // tm,),
        in_specs=[idx, idx, pl.BlockSpec((tm, d), lambda i: (i, 0)), pl.BlockSpec((tm, LANES), lambda i: (i, 0)),
                  pl.BlockSpec((1, 6, d), lambda i: (i // tpb, 0, 0)), pl.BlockSpec(memory_space=pl.ANY)],
        out_specs=pl.BlockSpec((tm, d), lambda i: (i, 0)),
        out_shape=jax.ShapeDtypeStruct((n_rows, d), F32),
        scratch_shapes=[pltpu.VMEM((tm, d), F32), pltpu.VMEM((tm, d), F32), pltpu.SemaphoreType.DMA(())],
        compiler_params=_params(),
        name="moe_combine",
    )(d1.reshape(-1, 1, tm), d2.reshape(-1, 1, tm), x, r, modtab, y)


def _moe(x, modtab, g2, router_w, w1, w3, w2, n_rows, t_lat, tme=512, tf=512, tm=512):
    d = x.shape[1]
    rw = jnp.zeros((d, LANES), F32).at[:, :N_EXPERTS].set(router_w)
    hp, r, cnt = _router(x, modtab, g2, rw, tm, n_rows, t_lat)
    e1, e2 = r[:, R_E1].astype(jnp.int32), r[:, R_E2].astype(jnp.int32)
    r1, r2 = r[:, R_R1].astype(jnp.int32), r[:, R_R2].astype(jnp.int32)
    counts = cnt[0, :N_EXPERTS].astype(jnp.int32)
    padded = (counts + tme - 1) // tme * tme
    ends = jnp.cumsum(padded)
    offs = ends - padded
    n_sorted = 2 * n_rows + N_EXPERTS * tme
    d1, d2 = offs[e1] + r1, offs[e2] + r2
    tile_start = jnp.arange(n_sorted // tme, dtype=jnp.int32) * tme
    n_valid = (ends[-1] // tme).astype(jnp.int32).reshape(1)
    tile_expert = jnp.minimum(jnp.sum(tile_start[:, None] >= ends[None, :], axis=1), N_EXPERTS - 1).astype(jnp.int32)
    tile_expert = jnp.where(tile_start < ends[-1], tile_expert, tile_expert[jnp.maximum(n_valid[0] - 1, 0)])
    xs = _dispatch(d1, d2, hp, n_sorted, tm)
    y = _experts(tile_expert, n_valid, xs, w1, w3, w2, tme, tf)
    return _combine(d1, d2, x, y, r, modtab, tm, n_rows, t_lat)


def _rope_tables(t_lat, pad_rows):
    rows = t_lat // GRID_W
    pos_r = jnp.repeat(jnp.arange(rows, dtype=F32), GRID_W)
    pos_c = jnp.tile(jnp.arange(GRID_W, dtype=F32), rows)
    inv = 1.0 / (ROPE_BASE ** (jnp.arange(ROPE_FREQS, dtype=F32) * 2.0 / (A_HEAD_DIM // 2)))
    ar, ac = pos_r[:, None] * inv, pos_c[:, None] * inv
    cos = jnp.concatenate([jnp.cos(ar), jnp.cos(ar), jnp.cos(ac), jnp.cos(ac)], axis=1)
    sin = jnp.concatenate([-jnp.sin(ar), jnp.sin(ar), -jnp.sin(ac), jnp.sin(ac)], axis=1)
    cos = jnp.concatenate([jnp.tile(cos, (1, A_KV_HEADS)), jnp.ones((pad_rows, A_KV_HEADS * A_HEAD_DIM), F32)])
    sin = jnp.concatenate([jnp.tile(sin, (1, A_KV_HEADS)), jnp.zeros((pad_rows, A_KV_HEADS * A_HEAD_DIM), F32)])
    return cos, sin


def _split_w_in(w, b):
    o = [0, 512, 1024, 2048, 3072, 3088, 5136, 6160, 6416, 6672, 9744]
    order = [(o[0], o[4]), (o[5], o[6]), (o[6], o[7]), (o[9], o[10]), (o[7], o[8]), (o[8], o[9])]
    w_z = jnp.concatenate([w[:, a:e] for a, e in order], axis=1).astype(BF16)
    b_z = jnp.concatenate([b[a:e] for a, e in order]).reshape(1, NZ)
    ng = 4 * M_HEADS
    w_g = jnp.zeros((w.shape[0], LANES), BF16).at[:, :ng].set(w[:, o[4]:o[5]].astype(BF16))
    b_g = jnp.zeros((1, LANES), F32).at[0, :ng].set(b[o[4]:o[5]])
    return w_z, b_z, w_g, b_g


def kernel(x, c, ctx, c_ctx, w_mod, b_mod, norm1_g, norm2_g, w_in, b_in, m_norm_g, conv_w, conv_b, conv_ln_g,
           conv_ln_b, q_norm_g, k_norm_g, attn_sink, w_branch_a, w_branch_b, w_branch_c, w_out, ffn_w1, ffn_w3,
           ffn_w2, router_w, moe_w1, moe_w3, moe_w2):
    n_batch, t_lat, d = x.shape
    t_ctx = ctx.shape[1]
    depth = w_mod.shape[0]
    n_lat, n_ctx = n_batch * t_lat, n_batch * t_ctx
    m = n_lat + n_ctx
    tm = min(1024, n_ctx)
    assert t_lat % tm == 0 and n_ctx % tm == 0 and t_lat >= 3 * A_BLOCK

    xs = jnp.concatenate([x.reshape(n_lat, d), ctx.reshape(n_ctx, d)], axis=0)
    cvec = jnp.zeros((16, d), F32).at[:n_batch].set(c).at[n_batch].set(c_ctx)
    mod = _modulation(cvec, w_mod, b_mod)[:, :n_batch + 1].reshape(depth, n_batch + 1, 6, d)
    cos_tab, sin_tab = _rope_tables(t_lat, tm)

    for l in range(depth):
        last = l == depth - 1
        modtab = mod[l]
        w_z, b_z, w_g, b_g = _split_w_in(w_in[l], b_in[l])
        z, gates = _in_projection(xs, modtab, norm1_g[l].reshape(1, d), w_z, b_z, w_g, b_g, tm, n_lat, t_lat)

        aux = _gate_prep(gates, tm).reshape(4, M_HEADS, m)
        auxr = jnp.pad(aux.transpose(1, 0, 2), ((0, 0), (0, 4), (0, 0)))
        kt = z[:, Z_MK:Z_MK + M_HEADS * M_QK_DIM].T
        ya_lat, ya_ctx = _mlstm(z, kt, auxr, m_norm_g[l], n_batch, t_lat, t_ctx)

        conv_args = (conv_w[l], conv_b[l], conv_ln_g[l], conv_ln_b[l])
        yb_lat = _conv_branch(z, 0, n_batch, t_lat, min(512, t_lat), *conv_args)

        qp, kp = _qk_prep(z, cos_tab, sin_tab, q_norm_g[l], k_norm_g[l], tm, n_lat, t_lat)
        vsrc = (z, Z_AV // (A_KV_HEADS * A_HEAD_DIM))
        yc_lat = _attention(attn_sink[l], qp, kp, vsrc, 0, n_batch, t_lat, t_ctx, n_lat, True)

        wa, wb, wc, wo = (w.astype(BF16) for w in (w_branch_a[l], w_branch_b[l], w_branch_c[l], w_out[l]))
        if not last:
            yb_ctx = _conv_branch(z, n_lat, n_batch, t_ctx, t_ctx, *conv_args)
            yc_ctx = _attention(attn_sink[l], qp, kp, vsrc, n_lat, n_batch, t_ctx, t_ctx, n_lat, False)
            ya = jnp.concatenate([ya_lat, ya_ctx], axis=0)
            yb = jnp.concatenate([yb_lat, yb_ctx], axis=0)
            yc = jnp.concatenate([yc_lat, yc_ctx], axis=0)
            n_rows = m
        else:
            ya, yb, yc, n_rows = ya_lat, yb_lat, yc_lat, n_lat
        tmm = min(512, tm)
        xs = _merge(ya, yb, yc, z, xs, modtab, wa, wb, wc, wo, tmm, n_rows, n_lat, t_lat)

        g2 = norm2_g[l].reshape(1, d)
        if l % 2 == 0:
            i = l // 2
            xs = _dense_ffn(xs, modtab, g2, ffn_w1[i].astype(BF16), ffn_w3[i].astype(BF16), ffn_w2[i].astype(BF16),
                            tmm, n_rows, n_lat, t_lat)
        else:
            i = l // 2
            assert last, "expert mixer is implemented for latent rows only"
            xs = _moe(xs, modtab, g2, router_w[i], moe_w1[i].astype(BF16), moe_w3[i].astype(BF16),
                      moe_w2[i].astype(BF16), n_lat, t_lat)
    return xs[:n_lat].reshape(n_batch, t_lat, d)
```

```python
import functools

import jax
import jax.numpy as jnp
from jax import lax
from jax.experimental import pallas as pl
from jax.experimental.pallas import tpu as pltpu

F32 = jnp.float32
BF16 = jnp.bfloat16

EPS = 1e-6
GRID_W = 64
M_HEADS = 4
M_QK_DIM = 128
M_V_DIM = 256
M_CHUNK = 128
CONV_WIDTH = 31
CONV_HALO = 16
A_HEADS = 16
A_KV_HEADS = 4
A_GROUP = A_HEADS // A_KV_HEADS
A_HEAD_DIM = 64
A_BLOCK = 128
WINDOW = 128
ROPE_BASE = 10000.0
ROPE_FREQS = A_HEAD_DIM // 4
N_EXPERTS = 8
LANES = 128
VMEM_LIMIT = 56 * 1024 * 1024

Z_MQ, Z_MK, Z_MV, Z_MO = 0, 512, 1024, 2048
Z_CA, Z_CG, Z_AQ, Z_GATE, Z_AK, Z_AV = 3072, 4096, 5120, 6144, 9216, 9472
NZ = 9728


def _params(**kw):
    return pltpu.CompilerParams(vmem_limit_bytes=VMEM_LIMIT, **kw)


def _sigmoid(x):
    return 1.0 / (1.0 + jnp.exp(-x))


def _silu(x):
    return x * _sigmoid(x)


def _log_sigmoid(x):
    return jnp.minimum(x, 0.0) - jnp.log(1.0 + jnp.exp(-jnp.abs(x)))


def _norm_mod(x, g, shift, scale):
    ms = jnp.mean(x * x, axis=-1, keepdims=True)
    return (x * lax.rsqrt(ms + EPS) * g) * (1.0 + scale) + shift


def _chunks(n, w):
    out, c = [], 0
    while c < n:
        out.append((c, min(w, n - c)))
        c += w
    return out


def _mod_kernel(c_ref, w_ref, b_ref, o_ref):
    s = _silu(c_ref[...]).astype(BF16)
    o_ref[0] = jnp.dot(s, w_ref[0].astype(BF16), preferred_element_type=F32) + b_ref[0]


def _modulation(cvec, w_mod, b_mod):
    depth, d, n = w_mod.shape
    tn = 1536
    return pl.pallas_call(
        _mod_kernel,
        grid=(depth, n // tn),
        in_specs=[pl.BlockSpec((cvec.shape[0], d), lambda l, j: (0, 0)),
                  pl.BlockSpec((1, d, tn), lambda l, j: (l, 0, j)),
                  pl.BlockSpec((1, 1, tn), lambda l, j: (l, 0, j))],
        out_specs=pl.BlockSpec((1, cvec.shape[0], tn), lambda l, j: (l, 0, j)),
        out_shape=jax.ShapeDtypeStruct((depth, cvec.shape[0], n), F32),
        compiler_params=_params(),
        name="adaln_mod",
    )(cvec, w_mod, b_mod.reshape(depth, 1, n))


def _inproj_kernel(x_ref, mod_ref, g_ref, w_ref, b_ref, wg_ref, bg_ref, z_ref, gates_ref, h_scr, *, col_chunks):
    @pl.when(pl.program_id(1) == 0)
    def _():
        h = _norm_mod(x_ref[...], g_ref[...], mod_ref[0, 0:1, :], mod_ref[0, 1:2, :]).astype(BF16)
        h_scr[...] = h
        gates_ref[...] = jnp.dot(h, wg_ref[...], preferred_element_type=F32) + bg_ref[...]

    h = h_scr[...]
    for c0, cw in col_chunks:
        acc = jnp.dot(h, w_ref[:, c0:c0 + cw], preferred_element_type=F32) + b_ref[:, c0:c0 + cw]
        z_ref[:, c0:c0 + cw] = acc.astype(BF16)


def _mod_index(i, n_lat_tiles, tiles_per_batch, n_batch):
    return jnp.where(i < n_lat_tiles, i // tiles_per_batch, n_batch)


def _in_projection(x, modtab, g1, w_z, b_z, w_g, b_g, tm, n_lat_rows, t_lat):
    m, d = x.shape
    n_batch = modtab.shape[0] - 1
    tn = NZ // 4
    n_lat_tiles, tpb = n_lat_rows // tm, t_lat // tm
    mod_map = lambda i, j: (_mod_index(i, n_lat_tiles, tpb, n_batch), 0, 0)
    return pl.pallas_call(
        functools.partial(_inproj_kernel, col_chunks=_chunks(tn, 256)),
        grid=(m // tm, NZ // tn),
        in_specs=[pl.BlockSpec((tm, d), lambda i, j: (i, 0)),
                  pl.BlockSpec((1, 6, d), mod_map),
                  pl.BlockSpec((1, d), lambda i, j: (0, 0)),
                  pl.BlockSpec((d, tn), lambda i, j: (0, j)),
                  pl.BlockSpec((1, tn), lambda i, j: (0, j)),
                  pl.BlockSpec((d, LANES), lambda i, j: (0, 0)),
                  pl.BlockSpec((1, LANES), lambda i, j: (0, 0))],
        out_specs=[pl.BlockSpec((tm, tn), lambda i, j: (i, j)),
                   pl.BlockSpec((tm, LANES), lambda i, j: (i, 0))],
        out_shape=[jax.ShapeDtypeStruct((m, NZ), BF16), jax.ShapeDtypeStruct((m, LANES), F32)],
        scratch_shapes=[pltpu.VMEM((tm, d), BF16)],
        compiler_params=_params(),
        name="in_projection",
    )(x, modtab, g1, w_z, b_z, w_g, b_g)


def _gateprep_kernel(g_ref, o_ref, *, n_chunks):
    r = lax.broadcasted_iota(jnp.int32, (M_CHUNK, M_CHUNK), 0)
    c = lax.broadcasted_iota(jnp.int32, (M_CHUNK, M_CHUNK), 1)
    tri_lo = (c <= r).astype(F32)
    tri_up = (c >= r).astype(F32)
    lane = lax.broadcasted_iota(jnp.int32, (M_CHUNK, LANES), 1)
    ng = 4 * M_HEADS
    for n in range(n_chunks):
        g = g_ref[n * M_CHUNK:(n + 1) * M_CHUNK, :]
        lf = _log_sigmoid(g)
        cf = jnp.dot(tri_lo, lf, preferred_element_type=F32, precision=lax.Precision.HIGHEST)
        cb = jnp.dot(tri_up, lf, preferred_element_type=F32, precision=lax.Precision.HIGHEST)
        fwd_f = (lane >= M_HEADS) & (lane < 2 * M_HEADS)
        bwd_f = (lane >= 3 * M_HEADS) & (lane < 4 * M_HEADS)
        res = jnp.where(fwd_f, cf, jnp.where(bwd_f, cb, g))
        o_ref[:, n * M_CHUNK:(n + 1) * M_CHUNK] = res.T[0:ng, :]


def _gate_prep(gates, tm):
    m = gates.shape[0]
    ng = 4 * M_HEADS
    return pl.pallas_call(
        functools.partial(_gateprep_kernel, n_chunks=tm // M_CHUNK),
        grid=(m // tm,),
        in_specs=[pl.BlockSpec((tm, LANES), lambda i: (i, 0))],
        out_specs=pl.BlockSpec((ng, tm), lambda i: (0, i)),
        out_shape=jax.ShapeDtypeStruct((ng, m), F32),
        compiler_params=_params(),
        name="mlstm_gate_prep",
    )(gates)


M_AUG = 16


def _mlstm_chunk(k, qt, vta, i_row, b_row, c_col, btot, c_ref, m_ref, mask):
    scale = M_QK_DIM ** -0.5
    dv = M_V_DIM
    m_prev = m_ref[...]
    c_prev = c_ref[...]
    g_row = b_row + m_prev
    dmat = jnp.where(mask, b_row + c_col, -jnp.inf)
    m_j = jnp.maximum(g_row, jnp.max(dmat, axis=0, keepdims=True))
    inter = jnp.exp(g_row - m_j)
    st = jnp.dot(k, qt, preferred_element_type=F32) * scale * jnp.exp(dmat - m_j)
    num = (inter * scale) * jnp.dot(c_prev.astype(BF16), qt, preferred_element_type=F32)
    num = num + jnp.dot(vta, st.astype(BF16), preferred_element_type=F32)
    den = num[dv:dv + 1, :]
    h = num[0:dv, :] / jnp.maximum(jnp.abs(den), jnp.exp(-m_j))

    a_row = btot - b_row + i_row
    m_loc = jnp.max(a_row, axis=-1, keepdims=True)
    w_row = jnp.exp(a_row - m_loc)
    m_new = jnp.maximum(btot + m_prev, m_loc)
    s_old = jnp.exp(btot + m_prev - m_new)
    s_new = jnp.exp(m_loc - m_new)
    c_loc = jnp.dot((vta.astype(F32) * w_row).astype(BF16), k, preferred_element_type=F32)
    c_ref[...] = s_old * c_prev + s_new * c_loc
    m_ref[...] = m_new
    return h


def _mlstm_kernel(kl, qtl, vtl, arl, kc, qtc, vtc, arc, ng_ref, out_l, out_c,
                  hf_l, hb_l, hf_c, hb_c, cf_ref, mf_ref, cb_ref, mb_ref, *, n_lat, n_ctx):
    L = M_CHUNK
    dv = M_V_DIM
    cf_ref[...] = jnp.zeros_like(cf_ref)
    mf_ref[...] = jnp.full_like(mf_ref, -jnp.inf)
    cb_ref[...] = jnp.zeros_like(cb_ref)
    mb_ref[...] = jnp.full_like(mb_ref, -jnp.inf)
    r = lax.broadcasted_iota(jnp.int32, (L, L), 0)
    c = lax.broadcasted_iota(jnp.int32, (L, L), 1)
    mask_f = r <= c
    mask_b = r >= c
    aug = (lax.broadcasted_iota(jnp.int32, (M_AUG, L), 0) == 0).astype(BF16)
    ng = jnp.broadcast_to(ng_ref[0], (dv, L))

    def one(k_r, qt_r, vt_r, ar_r, h_r, n, first, tot_lane, c_ref, m_ref, mask):
        rows = pl.ds(pl.multiple_of(n * L, L), L)
        ar = ar_r[0, :, rows]
        i_row, b_row = ar[first:first + 1, :], ar[first + 1:first + 2, :]
        c_col = jnp.broadcast_to(i_row - b_row, (8, L)).T[:, 0:1]
        vta = jnp.concatenate([vt_r[:, rows], aug], axis=0)
        h_r[:, rows] = _mlstm_chunk(k_r[rows, :], qt_r[:, rows], vta, i_row, b_row, c_col,
                                    b_row[:, tot_lane:tot_lane + 1], c_ref, m_ref, mask)

    def pair(k_r, qt_r, vt_r, ar_r, hf_r, hb_r, n_f, n_b):
        one(k_r, qt_r, vt_r, ar_r, hf_r, n_f, 0, L - 1, cf_ref, mf_ref, mask_f)
        one(k_r, qt_r, vt_r, ar_r, hb_r, n_b, 2, 0, cb_ref, mb_ref, mask_b)

    def finish(hf_r, hb_r, out_r, n):
        rows = pl.ds(pl.multiple_of(n * L, L), L)
        h = hf_r[:, rows] + hb_r[:, rows]
        out_r[:, rows] = (h * lax.rsqrt(jnp.mean(h * h, axis=0, keepdims=True) + EPS) * ng).astype(out_r.dtype)

    for n in range(n_ctx):
        pair(kc, qtc, vtc, arc, hf_c, hb_c, n, n_ctx - 1 - n)

    def body(t, carry):
        pair(kl, qtl, vtl, arl, hf_l, hb_l, t, n_lat - 1 - t)
        return carry
    lax.fori_loop(0, n_lat, body, 0)

    for n in range(n_ctx):
        finish(hf_c, hb_c, out_c, n)

    def fin_body(t, carry):
        finish(hf_l, hb_l, out_l, t)
        return carry
    lax.fori_loop(0, n_lat, fin_body, 0)


def _mlstm(z, qt, vt, auxr, norm_g, n_batch, t_lat, t_ctx):
    dk, dv, nh = M_QK_DIM, M_V_DIM, M_HEADS
    cb = n_batch * t_lat // t_ctx
    in_specs = [
        pl.BlockSpec((t_lat, dk), lambda b, h: (b, Z_MK // dk + h)),
        pl.BlockSpec((dk, t_lat), lambda b, h: (h, b)), pl.BlockSpec((dv, t_lat), lambda b, h: (h, b)),
        pl.BlockSpec((1, 8, t_lat), lambda b, h: (h, 0, b)),
        pl.BlockSpec((t_ctx, dk), lambda b, h: (cb + b, Z_MK // dk + h)),
        pl.BlockSpec((dk, t_ctx), lambda b, h: (h, cb + b)), pl.BlockSpec((dv, t_ctx), lambda b, h: (h, cb + b)),
        pl.BlockSpec((1, 8, t_ctx), lambda b, h: (h, 0, cb + b)),
        pl.BlockSpec((1, dv, 1), lambda b, h: (h, 0, 0)),
    ]
    out_specs = [pl.BlockSpec((dv, t_lat), lambda b, h: (h, b)), pl.BlockSpec((dv, t_ctx), lambda b, h: (h, b))]
    out_shape = [jax.ShapeDtypeStruct((nh * dv, n_batch * t_lat), BF16),
                 jax.ShapeDtypeStruct((nh * dv, n_batch * t_ctx), BF16)]
    scratch = [pltpu.VMEM((dv, t_lat), F32), pltpu.VMEM((dv, t_lat), F32),
               pltpu.VMEM((dv, t_ctx), F32), pltpu.VMEM((dv, t_ctx), F32),
               pltpu.VMEM((dv + M_AUG, dk), F32), pltpu.VMEM((1, 1), F32),
               pltpu.VMEM((dv + M_AUG, dk), F32), pltpu.VMEM((1, 1), F32)]
    return pl.pallas_call(
        functools.partial(_mlstm_kernel, n_lat=t_lat // M_CHUNK, n_ctx=t_ctx // M_CHUNK),
        grid=(n_batch, nh),
        in_specs=in_specs, out_specs=out_specs, out_shape=out_shape, scratch_shapes=scratch,
        compiler_params=_params(),
        name="mlstm",
    )(z, qt, vt, auxr, z, qt, vt, auxr, norm_g.reshape(nh, dv, 1))


def _conv_kernel(a_ref, g_ref, ap_ref, gp_ref, an_ref, gn_ref, w_ref, b_ref, lg_ref, lb_ref, o_ref, y_scr, c_scr,
                 sh_scr, *, tt, tiles_per_seq, row_sub, col_w):
    i = pl.program_id(1)
    hl = CONV_HALO
    glu = lambda a, g: a.astype(F32) * _sigmoid(g.astype(F32))
    prev_ok = (i > 0).astype(F32)
    next_ok = (i < tiles_per_seq - 1).astype(F32)
    y_scr[0:hl, :] = glu(ap_ref[...], gp_ref[...]) * prev_ok
    y_scr[hl:hl + tt, :] = glu(a_ref[...], g_ref[...])
    y_scr[hl + tt:hl + tt + hl, :] = glu(an_ref[...], gn_ref[...]) * next_ok
    d = a_ref.shape[1]
    sub = 8
    taps = [[j for j in range(CONV_WIDTH) if (j + hl - CONV_WIDTH // 2) % sub == r] for r in range(sub)]
    span = row_sub + 2 * hl

    def col_body(cb, carry):
        cols = pl.ds(pl.multiple_of(cb * col_w, col_w), col_w)
        wc = w_ref[:, cols]
        for n, r0 in enumerate(range(0, tt, row_sub)):
            acc = jnp.zeros((row_sub, col_w), F32)
            for r in range(sub):
                slot = (n % 2) * sub + r
                sh_scr[slot] = y_scr[r0 + r:r0 + r + span - sub, cols]
                for j in taps[r]:
                    a0 = j + hl - CONV_WIDTH // 2 - r
                    acc = acc + sh_scr[slot, a0:a0 + row_sub, :] * wc[j:j + 1, :]
            c_scr[r0:r0 + row_sub, cols] = acc
        return carry
    lax.fori_loop(0, d // col_w, col_body, 0)
    y = c_scr[...] + b_ref[...]
    mu = jnp.mean(y, axis=-1, keepdims=True)
    yc = y - mu
    var = jnp.mean(yc * yc, axis=-1, keepdims=True)
    o_ref[...] = _silu(yc * lax.rsqrt(var + EPS) * lg_ref[...] + lb_ref[...]).astype(o_ref.dtype)


def _conv_branch(z, row0, n_seq, t_seq, tt, conv_w, conv_b, ln_g, ln_b):
    d = conv_w.shape[1]
    hl = CONV_HALO
    tps = t_seq // tt
    rb0, hb0, hps = row0 // tt, row0 // hl, t_seq // hl
    ca, cg = Z_CA // d, Z_CG // d
    cur = lambda col: pl.BlockSpec((tt, d), lambda s, i: (rb0 + s * tps + i, col))
    prv = lambda col: pl.BlockSpec(
        (hl, d), lambda s, i: (hb0 + s * hps + jnp.maximum(i * (tt // hl) - 1, 0), col))
    nxt = lambda col: pl.BlockSpec(
        (hl, d), lambda s, i: (hb0 + s * hps + jnp.minimum((i + 1) * (tt // hl), hps - 1), col))
    vec = pl.BlockSpec((1, d), lambda s, i: (0, 0))
    wpad = jnp.zeros((32, d), F32).at[:CONV_WIDTH].set(conv_w)
    row_sub, col_w = 128, LANES
    return pl.pallas_call(
        functools.partial(_conv_kernel, tt=tt, tiles_per_seq=tps, row_sub=row_sub, col_w=col_w),
        grid=(n_seq, tps),
        in_specs=[cur(ca), cur(cg), prv(ca), prv(cg), nxt(ca), nxt(cg),
                  pl.BlockSpec((32, d), lambda s, i: (0, 0)), vec, vec, vec],
        out_specs=pl.BlockSpec((tt, d), lambda s, i: (s * tps + i, 0)),
        out_shape=jax.ShapeDtypeStruct((n_seq * t_seq, d), BF16),
        scratch_shapes=[pltpu.VMEM((tt + 2 * hl, d), F32), pltpu.VMEM((tt, d), F32),
                        pltpu.VMEM((16, row_sub + 2 * hl - 8, col_w), F32)],
        compiler_params=_params(),
        name="conv_branch",
    )(z, z, z, z, z, z, wpad, conv_b.reshape(1, d), ln_g.reshape(1, d), ln_b.reshape(1, d))


def _headnorm_rope(x, bd, gain, cos, sin, out_scale=1.0):
    x2 = x * x
    hi = x2.astype(BF16)
    lo = (x2 - hi.astype(F32)).astype(BF16)
    ss = jnp.dot(hi, bd, preferred_element_type=F32) + jnp.dot(lo, bd, preferred_element_type=F32)
    xn = x * lax.rsqrt(ss * (1.0 / A_HEAD_DIM) + EPS) * gain
    w = x.shape[1]
    lane = lax.broadcasted_iota(jnp.int32, x.shape, 1)
    first = (lane % (2 * ROPE_FREQS)) < ROPE_FREQS
    swapped = jnp.where(first, pltpu.roll(xn, w - ROPE_FREQS, 1), pltpu.roll(xn, ROPE_FREQS, 1))
    return (xn * cos + swapped * sin) * out_scale


def _qkprep_kernel(q_ref, k_ref, cos_ref, sin_ref, qg_ref, kg_ref, qo_ref, ko_ref):
    gw = A_GROUP * A_HEAD_DIM
    r = lax.broadcasted_iota(jnp.int32, (gw, gw), 0) // A_HEAD_DIM
    c = lax.broadcasted_iota(jnp.int32, (gw, gw), 1) // A_HEAD_DIM
    bd = (r == c).astype(BF16)
    cos, sin = cos_ref[...], sin_ref[...]
    for g in range(q_ref.shape[1] // gw):
        x = q_ref[:, g * gw:(g + 1) * gw].astype(F32)
        qo_ref[:, g * gw:(g + 1) * gw] = _headnorm_rope(x, bd, qg_ref[...], cos, sin,
                                                        A_HEAD_DIM ** -0.5).astype(qo_ref.dtype)
    ko_ref[...] = _headnorm_rope(k_ref[...].astype(F32), bd, kg_ref[...], cos, sin).astype(ko_ref.dtype)


def _qk_prep(z, cos_tab, sin_tab, q_g, k_g, tm, n_lat_rows, t_lat):
    m = z.shape[0]
    qw, kw = A_HEADS * A_HEAD_DIM, A_KV_HEADS * A_HEAD_DIM
    n_lat_tiles, tpb = n_lat_rows // tm, t_lat // tm
    tab = pl.BlockSpec((tm, kw), lambda i: (jnp.where(i < n_lat_tiles, i % tpb, tpb), 0))
    vec = pl.BlockSpec((1, kw), lambda i: (0, 0))
    return pl.pallas_call(
        _qkprep_kernel,
        grid=(m // tm,),
        in_specs=[pl.BlockSpec((tm, qw), lambda i: (i, Z_AQ // qw)),
                  pl.BlockSpec((tm, kw), lambda i: (i, Z_AK // kw)), tab, tab, vec, vec],
        out_specs=[pl.BlockSpec((tm, qw), lambda i: (i, 0)), pl.BlockSpec((tm, kw), lambda i: (i, 0))],
        out_shape=[jax.ShapeDtypeStruct((m, qw), BF16), jax.ShapeDtypeStruct((m, kw), BF16)],
        compiler_params=_params(),
        name="attn_qk_prep",
    )(z, z, cos_tab, sin_tab, jnp.tile(q_g, A_KV_HEADS).reshape(1, kw), jnp.tile(k_g, A_KV_HEADS).reshape(1, kw))


def _attn_kernel(sink_ref, q_ref, k_ref, v_ref, kc_ref, vc_ref, o_ref, *, local, t_seq):
    nq = pl.program_id(1)
    blk, dh, grp = A_BLOCK, A_HEAD_DIM, A_GROUP
    nt = (((1,), (1,)), ((), ()))
    if local:
        win = 3 * blk
        start = pl.multiple_of(jnp.clip(nq * blk - blk, 0, t_seq - win), blk)
        qpos = nq * blk + lax.broadcasted_iota(jnp.int32, (grp * blk, win), 0) % blk
        kpos = start + lax.broadcasted_iota(jnp.int32, (grp * blk, win), 1)
        near = jnp.abs(kpos - qpos) <= WINDOW
    rowgrp = lax.broadcasted_iota(jnp.int32, (grp * blk, 1), 0) // blk

    def with_ones(v):
        col = lax.broadcasted_iota(jnp.int32, (v.shape[0], dh), 1)
        return jnp.concatenate([v, (col == 0).astype(v.dtype)], axis=1)

    outs = []
    for h in range(A_KV_HEADS):
        cols = slice(h * dh, (h + 1) * dh)
        qs = jnp.concatenate([q_ref[:, (h * grp + g) * dh:(h * grp + g + 1) * dh] for g in range(grp)], axis=0)
        sink = jnp.zeros((grp * blk, 1), F32)
        for g in range(grp):
            sink = jnp.where(rowgrp == g, sink_ref[h * grp + g], sink)
        s_ctx = lax.dot_general(qs, kc_ref[:, cols], nt, preferred_element_type=F32)
        mx = jnp.maximum(jnp.max(s_ctx, axis=-1, keepdims=True), sink)
        if local:
            s_loc = lax.dot_general(qs, k_ref[pl.ds(start, win), cols], nt, preferred_element_type=F32)
            s_loc = jnp.where(near, s_loc, -jnp.inf)
            mx = jnp.maximum(mx, jnp.max(s_loc, axis=-1, keepdims=True))
        o = jnp.dot(jnp.exp(s_ctx - mx).astype(BF16), with_ones(vc_ref[:, cols]), preferred_element_type=F32)
        if local:
            o = o + jnp.dot(jnp.exp(s_loc - mx).astype(BF16), with_ones(v_ref[pl.ds(start, win), cols]),
                            preferred_element_type=F32)
        o = o[:, 0:dh] / (o[:, dh:dh + 1] + jnp.exp(sink - mx))
        outs.extend(o[g * blk:(g + 1) * blk, :] for g in range(grp))
    o_ref[...] = jnp.concatenate(outs, axis=1).astype(o_ref.dtype)


def _attention(sink, q, k, v, q_row0, n_batch, t_seq, t_ctx, ctx_row0, local):
    qw, kw = A_HEADS * A_HEAD_DIM, A_KV_HEADS * A_HEAD_DIM
    nqb = t_seq // A_BLOCK
    qb0, sb0, cb0 = q_row0 // A_BLOCK, q_row0 // t_seq, ctx_row0 // t_ctx
    zv, vcol = v
    return pl.pallas_call(
        functools.partial(_attn_kernel, local=local, t_seq=t_seq),
        grid_spec=pltpu.PrefetchScalarGridSpec(
            num_scalar_prefetch=1,
            grid=(n_batch, nqb),
            in_specs=[pl.BlockSpec((A_BLOCK, qw), lambda b, n, s: (qb0 + b * nqb + n, 0)),
                      pl.BlockSpec((t_seq, kw), lambda b, n, s: (sb0 + b, 0)),
                      pl.BlockSpec((t_seq, kw), lambda b, n, s: (sb0 + b, vcol)),
                      pl.BlockSpec((t_ctx, kw), lambda b, n, s: (cb0 + b, 0)),
                      pl.BlockSpec((t_ctx, kw), lambda b, n, s: (cb0 + b, vcol))],
            out_specs=pl.BlockSpec((A_BLOCK, qw), lambda b, n, s: (b * nqb + n, 0))),
        out_shape=jax.ShapeDtypeStruct((n_batch * t_seq, qw), BF16),
        compiler_params=_params(),
        name="attention_local" if local else "attention_ctx",
    )(sink, q, k, zv, k, zv)


def _merge_kernel(*refs, n_lat_tiles, has_ctx):
    n_br = 6 if has_ctx else 3
    o_ref, gl_ref, x_ref, mod_ref, wa_ref, wb_ref, wc_ref, wo_ref, xo_ref, m_scr = refs[n_br:]
    if has_ctx:
        is_ctx = pl.program_id(0) >= n_lat_tiles
        ya, yb, yc = (jnp.where(is_ctx, refs[2 * n + 1][...], refs[2 * n][...]) for n in range(3))
    else:
        ya, yb, yc = (refs[n][...] for n in range(3))
    d = x_ref.shape[1]
    ya = (ya.astype(F32) * _sigmoid(o_ref[...].astype(F32))).astype(BF16)
    for c0, cw in _chunks(d, 256):
        cs = slice(c0, c0 + cw)
        acc = _sigmoid(gl_ref[:, c0:c0 + cw].astype(F32)) * jnp.dot(ya, wa_ref[:, cs], preferred_element_type=F32)
        acc = acc + _sigmoid(gl_ref[:, d + c0:d + c0 + cw].astype(F32)) * jnp.dot(
            yb, wb_ref[:, cs], preferred_element_type=F32)
        acc = acc + _sigmoid(gl_ref[:, 2 * d + c0:2 * d + c0 + cw].astype(F32)) * jnp.dot(
            yc, wc_ref[:, cs], preferred_element_type=F32)
        m_scr[:, cs] = acc.astype(BF16)
    mm = m_scr[...]
    for c0, cw in _chunks(d, 256):
        cs = slice(c0, c0 + cw)
        xo_ref[:, cs] = x_ref[:, cs] + mod_ref[0, 2:3, cs] * jnp.dot(mm, wo_ref[:, cs], preferred_element_type=F32)


def _merge(branches, z, x, modtab, wa, wb, wc, wo, tm, n_rows, n_lat_rows, t_lat):
    d = x.shape[1]
    n_batch = modtab.shape[0] - 1
    n_lat_tiles, tpb = n_lat_rows // tm, t_lat // tm
    has_ctx = branches[0][1] is not None
    row = pl.BlockSpec((tm, d), lambda i: (i, 0))
    lat = pl.BlockSpec((tm, d), lambda i: (jnp.minimum(i, n_lat_tiles - 1), 0))
    ctx = pl.BlockSpec((tm, d), lambda i: (jnp.maximum(i - n_lat_tiles, 0), 0))
    wsp = pl.BlockSpec((d, d), lambda i: (0, 0))
    br_args = [a for pair in branches for a in (pair if has_ctx else pair[:1])]
    br_specs = [lat, ctx] * 3 if has_ctx else [lat] * 3
    return pl.pallas_call(
        functools.partial(_merge_kernel, n_lat_tiles=n_lat_tiles, has_ctx=has_ctx),
        grid=(n_rows // tm,),
        in_specs=br_specs + [pl.BlockSpec((tm, d), lambda i: (i, Z_MO // d)),
                             pl.BlockSpec((tm, 3 * d), lambda i: (i, Z_GATE // (3 * d))), row,
                             pl.BlockSpec((1, 6, d), lambda i: (_mod_index(i, n_lat_tiles, tpb, n_batch), 0, 0)),
                             wsp, wsp, wsp, wsp],
        out_specs=row,
        out_shape=jax.ShapeDtypeStruct(x.shape, F32),
        scratch_shapes=[pltpu.VMEM((tm, d), BF16)],
        input_output_aliases={len(br_args) + 2: 0},
        compiler_params=_params(),
        name="merge_out_proj",
    )(*br_args, z, z, x, modtab, wa, wb, wc, wo)


def _ffn_kernel(x_ref, mod_ref, g_ref, w1_ref, w3_ref, w2_ref, xo_ref, acc_scr, *, f_chunks):
    h = _norm_mod(x_ref[...], g_ref[...], mod_ref[0, 3:4, :], mod_ref[0, 4:5, :]).astype(BF16)
    for n, (c0, cw) in enumerate(f_chunks):
        a = jnp.dot(h, w1_ref[:, c0:c0 + cw], preferred_element_type=F32)
        b = jnp.dot(h, w3_ref[:, c0:c0 + cw], preferred_element_type=F32)
        u = (_silu(a) * b).astype(BF16)
        o = jnp.dot(u, w2_ref[c0:c0 + cw, :], preferred_element_type=F32)
        if n == 0:
            acc_scr[...] = o
        else:
            acc_scr[...] += o
    xo_ref[...] = x_ref[...] + mod_ref[0, 5:6, :] * acc_scr[...]


def _dense_ffn(x, modtab, g2, w1, w3, w2, tm, n_rows, n_lat_rows, t_lat):
    d, f = w1.shape
    n_batch = modtab.shape[0] - 1
    n_lat_tiles, tpb = n_lat_rows // tm, t_lat // tm
    row = pl.BlockSpec((tm, d), lambda i: (i, 0))
    return pl.pallas_call(
        functools.partial(_ffn_kernel, f_chunks=_chunks(f, 256)),
        grid=(n_rows // tm,),
        in_specs=[row, pl.BlockSpec((1, 6, d), lambda i: (_mod_index(i, n_lat_tiles, tpb, n_batch), 0, 0)),
                  pl.BlockSpec((1, d), lambda i: (0, 0)),
                  pl.BlockSpec((d, f), lambda i: (0, 0)), pl.BlockSpec((d, f), lambda i: (0, 0)),
                  pl.BlockSpec((f, d), lambda i: (0, 0))],
        out_specs=row,
        out_shape=jax.ShapeDtypeStruct(x.shape, F32),
        scratch_shapes=[pltpu.VMEM((tm, d), F32)],
        input_output_aliases={0: 0},
        compiler_params=_params(),
        name="dense_ffn",
    )(x, modtab, g2, w1, w3, w2)


R_E1, R_E2, R_R1, R_R2, R_P1, R_P2 = range(6)
HI16 = 0xFFFF0000


def _pack_bf16_pair(lo, hi):
    lo_bits = lax.bitcast_convert_type(lo.astype(BF16).astype(F32), jnp.uint32)
    hi_bits = lax.bitcast_convert_type(hi.astype(BF16).astype(F32), jnp.uint32)
    return (hi_bits & jnp.uint32(HI16)) | (lo_bits >> 16)


def _unpack_bf16_pair(u):
    lo = lax.bitcast_convert_type(u << 16, F32).astype(BF16)
    hi = lax.bitcast_convert_type(u & jnp.uint32(HI16), F32).astype(BF16)
    return lo, hi


def _router_kernel(x_ref, mod_ref, g_ref, rw_ref, hp_ref, r_ref, cnt_ref, carry_scr):
    tm, d = x_ref.shape

    @pl.when(pl.program_id(0) == 0)
    def _():
        carry_scr[...] = jnp.zeros_like(carry_scr)

    h = _norm_mod(x_ref[...], g_ref[...], mod_ref[0, 3:4, :], mod_ref[0, 4:5, :])
    hp_ref[...] = _pack_bf16_pair(h[:, :d // 2], h[:, d // 2:])
    logits = jnp.dot(h, rw_ref[...], preferred_element_type=F32, precision=lax.Precision.HIGHEST)
    lane = lax.broadcasted_iota(jnp.int32, (tm, LANES), 1).astype(F32)
    logits = jnp.where(lane < N_EXPERTS, logits, -jnp.inf)
    v1 = jnp.max(logits, axis=-1, keepdims=True)
    e1 = jnp.min(jnp.where(logits == v1, lane, float(LANES)), axis=-1, keepdims=True)
    rest = jnp.where(lane == e1, -jnp.inf, logits)
    v2 = jnp.max(rest, axis=-1, keepdims=True)
    e2 = jnp.min(jnp.where(rest == v2, lane, float(LANES)), axis=-1, keepdims=True)
    t = jnp.exp(v2 - v1)
    p1 = 1.0 / (1.0 + t)
    p2 = t / (1.0 + t)
    hot1 = lane == e1
    hot2 = lane == e2
    hot = jnp.where(hot1 | hot2, 1.0, 0.0)
    rr = lax.broadcasted_iota(jnp.int32, (tm, tm), 0)
    cc = lax.broadcasted_iota(jnp.int32, (tm, tm), 1)
    before = jnp.where(cc < rr, 1.0, 0.0).astype(BF16)
    prefix = jnp.dot(before, hot.astype(BF16), preferred_element_type=F32) + carry_scr[...]
    r1 = jnp.sum(jnp.where(hot1, prefix, 0.0), axis=-1, keepdims=True)
    r2 = jnp.sum(jnp.where(hot2, prefix, 0.0), axis=-1, keepdims=True)
    carry_scr[...] += jnp.sum(hot, axis=0, keepdims=True)
    cnt_ref[...] = carry_scr[...]
    out = jnp.zeros((tm, LANES), F32)
    for idx, val in ((R_E1, e1), (R_E2, e2), (R_R1, r1), (R_R2, r2), (R_P1, p1), (R_P2, p2)):
        out = jnp.where(lane == float(idx), val, out)
    r_ref[...] = out


def _router(x, modtab, g2, rw, tm, n_rows, t_lat):
    d = x.shape[1]
    tpb = t_lat // tm
    return pl.pallas_call(
        _router_kernel,
        grid=(n_rows // tm,),
        in_specs=[pl.BlockSpec((tm, d), lambda i: (i, 0)), pl.BlockSpec((1, 6, d), lambda i: (i // tpb, 0, 0)),
                  pl.BlockSpec((1, d), lambda i: (0, 0)), pl.BlockSpec((d, LANES), lambda i: (0, 0))],
        out_specs=[pl.BlockSpec((tm, d // 2), lambda i: (i, 0)), pl.BlockSpec((tm, LANES), lambda i: (i, 0)),
                   pl.BlockSpec((1, LANES), lambda i: (0, 0))],
        out_shape=[jax.ShapeDtypeStruct((n_rows, d // 2), jnp.uint32), jax.ShapeDtypeStruct((n_rows, LANES), F32),
                   jax.ShapeDtypeStruct((1, LANES), F32)],
        scratch_shapes=[pltpu.VMEM((1, LANES), F32)],
        compiler_params=_params(),
        name="moe_router",
    )(x, modtab, g2, rw)


def _row_copies(n, make, unroll=8):
    def start(g, c):
        for u in range(unroll):
            for cp in make(g * unroll + u):
                cp.start(priority=u % 2)
        return c
    lax.fori_loop(0, n // unroll, start, 0)

    def wait(g, c):
        for u in range(unroll):
            for cp in make(g * unroll + u):
                cp.wait()
        return c
    lax.fori_loop(0, n // unroll, wait, 0)


def _dispatch_kernel(d1_ref, d2_ref, hp_ref, xs_in, xs_hbm, sem):
    del xs_in

    def make(r):
        src = hp_ref.at[pl.ds(r, 1), :]
        return (pltpu.make_async_copy(src, xs_hbm.at[pl.ds(d1_ref[0, 0, r], 1), :], sem),
                pltpu.make_async_copy(src, xs_hbm.at[pl.ds(d2_ref[0, 0, r], 1), :], sem))
    _row_copies(hp_ref.shape[0], make)


def _dispatch(d1, d2, hp, n_sorted, tm):
    n_rows, w = hp.shape
    idx = pl.BlockSpec((1, 1, tm), lambda i: (i, 0, 0), memory_space=pltpu.SMEM)
    return pl.pallas_call(
        _dispatch_kernel,
        grid=(n_rows // tm,),
        in_specs=[idx, idx, pl.BlockSpec((tm, w), lambda i: (i, 0)), pl.BlockSpec(memory_space=pl.ANY)],
        out_specs=pl.BlockSpec(memory_space=pl.ANY),
        out_shape=jax.ShapeDtypeStruct((n_sorted, w), jnp.uint32),
        scratch_shapes=[pltpu.SemaphoreType.DMA(())],
        input_output_aliases={3: 0},
        compiler_params=_params(),
        name="moe_dispatch",
    )(d1.reshape(-1, 1, tm), d2.reshape(-1, 1, tm), hp, jnp.zeros((n_sorted, w), jnp.uint32))


def _expert_kernel(te_ref, nv_ref, xs_ref, w1_ref, w3_ref, w2_ref, y_ref, xb_scr, acc_scr, *, n_f):
    i, f = pl.program_id(0), pl.program_id(1)
    valid = i < nv_ref[0]
    half = xs_ref.shape[1]

    @pl.when(valid & (f == 0))
    def _():
        lo, hi = _unpack_bf16_pair(xs_ref[...])
        xb_scr[:, :half] = lo
        xb_scr[:, half:] = hi

    @pl.when(valid)
    def _():
        xb = xb_scr[...]

        @pl.when(f == 0)
        def _():
            acc_scr[...] = jnp.zeros_like(acc_scr)

        for c0, cw in _chunks(w1_ref.shape[2], 256):
            a = jnp.dot(xb, w1_ref[0, :, c0:c0 + cw].astype(BF16), preferred_element_type=F32)
            b = jnp.dot(xb, w3_ref[0, :, c0:c0 + cw].astype(BF16), preferred_element_type=F32)
            acc_scr[...] += jnp.dot((_silu(a) * b).astype(BF16), w2_ref[0, c0:c0 + cw, :].astype(BF16),
                                    preferred_element_type=F32)

        @pl.when(f == n_f - 1)
        def _():
            y_ref[...] = acc_scr[...]

    @pl.when(jnp.logical_not(valid) & (f == n_f - 1))
    def _():
        y_ref[...] = jnp.zeros_like(y_ref)


def _experts(tile_expert, n_valid, xs, w1, w3, w2, tme, tf):
    ne, d, f = w1.shape
    n_tiles = xs.shape[0] // tme
    n_f = f // tf
    return pl.pallas_call(
        functools.partial(_expert_kernel, n_f=n_f),
        grid_spec=pltpu.PrefetchScalarGridSpec(
            num_scalar_prefetch=2,
            grid=(n_tiles, n_f),
            in_specs=[pl.BlockSpec((tme, d // 2), lambda i, j, te, nv: (i, 0)),
                      pl.BlockSpec((1, d, tf), lambda i, j, te, nv: (te[i], 0, j)),
                      pl.BlockSpec((1, d, tf), lambda i, j, te, nv: (te[i], 0, j)),
                      pl.BlockSpec((1, tf, d), lambda i, j, te, nv: (te[i], j, 0))],
            out_specs=pl.BlockSpec((tme, d), lambda i, j, te, nv: (i, 0)),
            scratch_shapes=[pltpu.VMEM((tme, d), BF16), pltpu.VMEM((tme, d), F32)]),
        out_shape=jax.ShapeDtypeStruct((xs.shape[0], d), F32),
        compiler_params=_params(),
        name="moe_experts",
    )(tile_expert, n_valid, xs, w1, w3, w2)


def _combine_kernel(d1_ref, d2_ref, x_ref, r_ref, mod_ref, y_hbm, xo_ref, y1_scr, y2_scr, sem):
    def make(r):
        return (pltpu.make_async_copy(y_hbm.at[pl.ds(d1_ref[0, 0, r], 1), :], y1_scr.at[pl.ds(r, 1), :], sem),
                pltpu.make_async_copy(y_hbm.at[pl.ds(d2_ref[0, 0, r], 1), :], y2_scr.at[pl.ds(r, 1), :], sem))
    _row_copies(x_ref.shape[0], make)
    r = r_ref[...]
    p1 = r[:, R_P1:R_P1 + 1]
    p2 = r[:, R_P2:R_P2 + 1]
    xo_ref[...] = x_ref[...] + mod_ref[0, 5:6, :] * (p1 * y1_scr[...] + p2 * y2_scr[...])


def _combine(d1, d2, x, y, r, modtab, tm, n_rows, t_lat):
    d = x.shape[1]
    tpb = t_lat // tm
    idx = pl.BlockSpec((1, 1, tm), lambda i: (i, 0, 0), memory_space=pltpu.SMEM)
    return pl.pallas_call(
        _combine_kernel,
        grid=(n_rows // tm,),
        in_specs=[idx, idx, pl.BlockSpec((tm, d), lambda i: (i, 0)), pl.BlockSpec((tm, LANES), lambda i: (i, 0)),
                  pl.BlockSpec((1, 6, d), lambda i: (i // tpb, 0, 0)), pl.BlockSpec(memory_space=pl.ANY)],
        out_specs=pl.BlockSpec((tm, d), lambda i: (i, 0)),
        out_shape=jax.ShapeDtypeStruct((n_rows, d), F32),
        scratch_shapes=[pltpu.VMEM((tm, d), F32), pltpu.VMEM((tm, d), F32), pltpu.SemaphoreType.DMA(())],
        compiler_params=_params(),
        name="moe_combine",
    )(d1.reshape(-1, 1, tm), d2.reshape(-1, 1, tm), x, r, modtab, y)


def _moe(x, modtab, g2, router_w, w1, w3, w2, n_rows, t_lat, tme=1024, tf=896, tm=512):
    d = x.shape[1]
    rw = jnp.zeros((d, LANES), F32).at[:, :N_EXPERTS].set(router_w)
    hp, r, cnt = _router(x, modtab, g2, rw, tm, n_rows, t_lat)
    e1, e2 = r[:, R_E1].astype(jnp.int32), r[:, R_E2].astype(jnp.int32)
    r1, r2 = r[:, R_R1].astype(jnp.int32), r[:, R_R2].astype(jnp.int32)
    counts = cnt[0, :N_EXPERTS].astype(jnp.int32)
    padded = (counts + tme - 1) // tme * tme
    ends = jnp.cumsum(padded)
    offs = ends - padded
    n_sorted = 2 * n_rows + N_EXPERTS * tme
    d1, d2 = offs[e1] + r1, offs[e2] + r2
    tile_start = jnp.arange(n_sorted // tme, dtype=jnp.int32) * tme
    n_valid = (ends[-1] // tme).astype(jnp.int32).reshape(1)
    tile_expert = jnp.minimum(jnp.sum(tile_start[:, None] >= ends[None, :], axis=1), N_EXPERTS - 1).astype(jnp.int32)
    tile_expert = jnp.where(tile_start < ends[-1], tile_expert, tile_expert[jnp.maximum(n_valid[0] - 1, 0)])
    xs = _dispatch(d1, d2, hp, n_sorted, tm)
    y = _experts(tile_expert, n_valid, xs, w1, w3, w2, tme, tf)
    return _combine(d1, d2, x, y, r, modtab, tm, n_rows, t_lat)


def _rope_tables(t_lat, pad_rows):
    rows = t_lat // GRID_W
    pos_r = jnp.repeat(jnp.arange(rows, dtype=F32), GRID_W)
    pos_c = jnp.tile(jnp.arange(GRID_W, dtype=F32), rows)
    inv = 1.0 / (ROPE_BASE ** (jnp.arange(ROPE_FREQS, dtype=F32) * 2.0 / (A_HEAD_DIM // 2)))
    ar, ac = pos_r[:, None] * inv, pos_c[:, None] * inv
    cos = jnp.concatenate([jnp.cos(ar), jnp.cos(ar), jnp.cos(ac), jnp.cos(ac)], axis=1)
    sin = jnp.concatenate([-jnp.sin(ar), jnp.sin(ar), -jnp.sin(ac), jnp.sin(ac)], axis=1)
    cos = jnp.concatenate([jnp.tile(cos, (1, A_KV_HEADS)), jnp.ones((pad_rows, A_KV_HEADS * A_HEAD_DIM), F32)])
    sin = jnp.concatenate([jnp.tile(sin, (1, A_KV_HEADS)), jnp.zeros((pad_rows, A_KV_HEADS * A_HEAD_DIM), F32)])
    return cos, sin


def _split_w_in(w, b):
    o = [0, 512, 1024, 2048, 3072, 3088, 5136, 6160, 6416, 6672, 9744]
    order = [(o[0], o[4]), (o[5], o[6]), (o[6], o[7]), (o[9], o[10]), (o[7], o[8]), (o[8], o[9])]
    w_z = jnp.concatenate([w[:, a:e] for a, e in order], axis=1).astype(BF16)
    b_z = jnp.concatenate([b[a:e] for a, e in order]).reshape(1, NZ)
    ng = 4 * M_HEADS
    w_g = jnp.zeros((w.shape[0], LANES), BF16).at[:, :ng].set(w[:, o[4]:o[5]].astype(BF16))
    b_g = jnp.zeros((1, LANES), F32).at[0, :ng].set(b[o[4]:o[5]])
    return w_z, b_z, w_g, b_g


def kernel(x, c, ctx, c_ctx, w_mod, b_mod, norm1_g, norm2_g, w_in, b_in, m_norm_g, conv_w, conv_b, conv_ln_g,
           conv_ln_b, q_norm_g, k_norm_g, attn_sink, w_branch_a, w_branch_b, w_branch_c, w_out, ffn_w1, ffn_w3,
           ffn_w2, router_w, moe_w1, moe_w3, moe_w2):
    n_batch, t_lat, d = x.shape
    t_ctx = ctx.shape[1]
    depth = w_mod.shape[0]
    n_lat, n_ctx = n_batch * t_lat, n_batch * t_ctx
    m = n_lat + n_ctx
    tm = min(1024, n_ctx)
    assert t_lat % tm == 0 and n_ctx % tm == 0 and t_lat >= 3 * A_BLOCK

    xs = jnp.concatenate([x.reshape(n_lat, d), ctx.reshape(n_ctx, d)], axis=0)
    cvec = jnp.zeros((16, d), F32).at[:n_batch].set(c).at[n_batch].set(c_ctx)
    mod = _modulation(cvec, w_mod, b_mod)[:, :n_batch + 1].reshape(depth, n_batch + 1, 6, d)
    cos_tab, sin_tab = _rope_tables(t_lat, tm)

    for l in range(depth):
        last = l == depth - 1
        modtab = mod[l]
        w_z, b_z, w_g, b_g = _split_w_in(w_in[l], b_in[l])
        z, gates = _in_projection(xs, modtab, norm1_g[l].reshape(1, d), w_z, b_z, w_g, b_g, tm, n_lat, t_lat)

        aux = _gate_prep(gates, tm).reshape(4, M_HEADS, m)
        auxr = jnp.pad(aux.transpose(1, 0, 2), ((0, 0), (0, 4), (0, 0)))
        qt = z[:, Z_MQ:Z_MQ + M_HEADS * M_QK_DIM].T
        vt = z[:, Z_MV:Z_MV + M_HEADS * M_V_DIM].T
        yat_lat, yat_ctx = _mlstm(z, qt, vt, auxr, m_norm_g[l], n_batch, t_lat, t_ctx)
        ya_lat, ya_ctx = yat_lat.T, yat_ctx.T

        conv_args = (conv_w[l], conv_b[l], conv_ln_g[l], conv_ln_b[l])
        yb_lat = _conv_branch(z, 0, n_batch, t_lat, min(512, t_lat), *conv_args)

        qp, kp = _qk_prep(z, cos_tab, sin_tab, q_norm_g[l], k_norm_g[l], tm, n_lat, t_lat)
        vsrc = (z, Z_AV // (A_KV_HEADS * A_HEAD_DIM))
        yc_lat = _attention(attn_sink[l], qp, kp, vsrc, 0, n_batch, t_lat, t_ctx, n_lat, True)

        wa, wb, wc, wo = (w.astype(BF16) for w in (w_branch_a[l], w_branch_b[l], w_branch_c[l], w_out[l]))
        if not last:
            yb_ctx = _conv_branch(z, n_lat, n_batch, t_ctx, t_ctx, *conv_args)
            yc_ctx = _attention(attn_sink[l], qp, kp, vsrc, n_lat, n_batch, t_ctx, t_ctx, n_lat, False)
            branches = ((ya_lat, ya_ctx), (yb_lat, yb_ctx), (yc_lat, yc_ctx))
            n_rows = m
        else:
            branches = ((ya_lat, None), (yb_lat, None), (yc_lat, None))
            n_rows = n_lat
        tmm = min(512, tm)
        xs = _merge(branches, z, xs, modtab, wa, wb, wc, wo, tmm, n_rows, n_lat, t_lat)

        g2 = norm2_g[l].reshape(1, d)
        if l % 2 == 0:
            i = l // 2
            xs = _dense_ffn(xs, modtab, g2, ffn_w1[i].astype(BF16), ffn_w3[i].astype(BF16), ffn_w2[i].astype(BF16),
                            tmm, n_rows, n_lat, t_lat)
        else:
            i = l // 2
            assert last, "expert mixer is implemented for latent rows only"
            xs = _moe(xs, modtab, g2, router_w[i], moe_w1[i], moe_w3[i], moe_w2[i], n_lat, t_lat)
    return xs[:n_lat].reshape(n_batch, t_lat, d)
```

```python
import functools

import jax
import jax.numpy as jnp
from jax import lax
from jax.experimental import pallas as pl
from jax.experimental.pallas import tpu as pltpu

F32 = jnp.float32
BF16 = jnp.bfloat16

EPS = 1e-6
GRID_W = 64
M_HEADS = 4
M_QK_DIM = 128
M_V_DIM = 256
M_CHUNK = 128
CONV_WIDTH = 31
CONV_HALO = 16
A_HEADS = 16
A_KV_HEADS = 4
A_GROUP = A_HEADS // A_KV_HEADS
A_HEAD_DIM = 64
A_BLOCK = 128
WINDOW = 128
ROPE_BASE = 10000.0
ROPE_FREQS = A_HEAD_DIM // 4
N_EXPERTS = 8
LANES = 128
VMEM_LIMIT = 56 * 1024 * 1024

Z_MQ, Z_MK, Z_MV, Z_MO = 0, 512, 1024, 2048
Z_CA, Z_CG, Z_AQ, Z_GATE, Z_AK, Z_AV = 3072, 4096, 5120, 6144, 9216, 9472
NZ = 9728


def _params(**kw):
    return pltpu.CompilerParams(vmem_limit_bytes=VMEM_LIMIT, **kw)


def _sigmoid(x):
    return 1.0 / (1.0 + jnp.exp(-x))


def _silu(x):
    return x * _sigmoid(x)


def _log_sigmoid(x):
    return jnp.minimum(x, 0.0) - jnp.log(1.0 + jnp.exp(-jnp.abs(x)))


def _norm_mod(x, g, shift, scale):
    ms = jnp.mean(x * x, axis=-1, keepdims=True)
    return (x * lax.rsqrt(ms + EPS) * g) * (1.0 + scale) + shift


def _chunks(n, w):
    out, c = [], 0
    while c < n:
        out.append((c, min(w, n - c)))
        c += w
    return out


def _mod_kernel(c_ref, w_ref, b_ref, o_ref):
    s = _silu(c_ref[...]).astype(BF16)
    o_ref[0] = jnp.dot(s, w_ref[0].astype(BF16), preferred_element_type=F32) + b_ref[0]


def _modulation(cvec, w_mod, b_mod):
    depth, d, n = w_mod.shape
    tn = 1536
    return pl.pallas_call(
        _mod_kernel,
        grid=(depth, n // tn),
        in_specs=[pl.BlockSpec((cvec.shape[0], d), lambda l, j: (0, 0)),
                  pl.BlockSpec((1, d, tn), lambda l, j: (l, 0, j)),
                  pl.BlockSpec((1, 1, tn), lambda l, j: (l, 0, j))],
        out_specs=pl.BlockSpec((1, cvec.shape[0], tn), lambda l, j: (l, 0, j)),
        out_shape=jax.ShapeDtypeStruct((depth, cvec.shape[0], n), F32),
        compiler_params=_params(),
        name="adaln_mod",
    )(cvec, w_mod, b_mod.reshape(depth, 1, n))


def _inproj_kernel(x_ref, mod_ref, g_ref, w_ref, b_ref, wg_ref, bg_ref, z_ref, gates_ref, qt_ref, vt_ref, h_scr,
                   *, col_chunks):
    def project(first_tile):
        h = h_scr[...]
        for c0, cw in col_chunks:
            acc = jnp.dot(h, w_ref[:, c0:c0 + cw], preferred_element_type=F32) + b_ref[:, c0:c0 + cw]
            z_ref[:, c0:c0 + cw] = acc.astype(BF16)
            if first_tile and Z_MQ <= c0 < Z_MK:
                qt_ref[c0 - Z_MQ:c0 - Z_MQ + cw, :] = acc.T.astype(BF16)
            if first_tile and Z_MV <= c0 < Z_MO:
                vt_ref[c0 - Z_MV:c0 - Z_MV + cw, :] = acc.T.astype(BF16)

    @pl.when(pl.program_id(1) == 0)
    def _():
        h = _norm_mod(x_ref[...], g_ref[...], mod_ref[0, 0:1, :], mod_ref[0, 1:2, :]).astype(BF16)
        h_scr[...] = h
        gates_ref[...] = jnp.dot(h, wg_ref[...], preferred_element_type=F32) + bg_ref[...]
        project(True)

    @pl.when(pl.program_id(1) > 0)
    def _():
        project(False)


def _mod_index(i, n_lat_tiles, tiles_per_batch, n_batch):
    return jnp.where(i < n_lat_tiles, i // tiles_per_batch, n_batch)


def _in_projection(x, modtab, g1, w_z, b_z, w_g, b_g, tm, n_lat_rows, t_lat):
    m, d = x.shape
    n_batch = modtab.shape[0] - 1
    tn = NZ // 4
    assert Z_MO <= tn, "the first column tile must hold the mLSTM q and v columns"
    n_lat_tiles, tpb = n_lat_rows // tm, t_lat // tm
    mod_map = lambda i, j: (_mod_index(i, n_lat_tiles, tpb, n_batch), 0, 0)
    return pl.pallas_call(
        functools.partial(_inproj_kernel, col_chunks=_chunks(tn, 256)),
        grid=(m // tm, NZ // tn),
        in_specs=[pl.BlockSpec((tm, d), lambda i, j: (i, 0)),
                  pl.BlockSpec((1, 6, d), mod_map),
                  pl.BlockSpec((1, d), lambda i, j: (0, 0)),
                  pl.BlockSpec((d, tn), lambda i, j: (0, j)),
                  pl.BlockSpec((1, tn), lambda i, j: (0, j)),
                  pl.BlockSpec((d, LANES), lambda i, j: (0, 0)),
                  pl.BlockSpec((1, LANES), lambda i, j: (0, 0))],
        out_specs=[pl.BlockSpec((tm, tn), lambda i, j: (i, j)),
                   pl.BlockSpec((tm, LANES), lambda i, j: (i, 0)),
                   pl.BlockSpec((Z_MK - Z_MQ, tm), lambda i, j: (0, i)),
                   pl.BlockSpec((Z_MO - Z_MV, tm), lambda i, j: (0, i))],
        out_shape=[jax.ShapeDtypeStruct((m, NZ), BF16), jax.ShapeDtypeStruct((m, LANES), F32),
                   jax.ShapeDtypeStruct((Z_MK - Z_MQ, m), BF16), jax.ShapeDtypeStruct((Z_MO - Z_MV, m), BF16)],
        scratch_shapes=[pltpu.VMEM((tm, d), BF16)],
        compiler_params=_params(),
        name="in_projection",
    )(x, modtab, g1, w_z, b_z, w_g, b_g)


def _gateprep_kernel(g_ref, o_ref, *, n_chunks):
    r = lax.broadcasted_iota(jnp.int32, (M_CHUNK, M_CHUNK), 0)
    c = lax.broadcasted_iota(jnp.int32, (M_CHUNK, M_CHUNK), 1)
    tri_lo = (c <= r).astype(F32)
    tri_up = (c >= r).astype(F32)
    lane = lax.broadcasted_iota(jnp.int32, (M_CHUNK, LANES), 1)
    ng = 4 * M_HEADS
    for n in range(n_chunks):
        g = g_ref[n * M_CHUNK:(n + 1) * M_CHUNK, :]
        lf = _log_sigmoid(g)
        cf = jnp.dot(tri_lo, lf, preferred_element_type=F32, precision=lax.Precision.HIGHEST)
        cb = jnp.dot(tri_up, lf, preferred_element_type=F32, precision=lax.Precision.HIGHEST)
        fwd_f = (lane >= M_HEADS) & (lane < 2 * M_HEADS)
        bwd_f = (lane >= 3 * M_HEADS) & (lane < 4 * M_HEADS)
        res = jnp.where(fwd_f, cf, jnp.where(bwd_f, cb, g))
        o_ref[:, n * M_CHUNK:(n + 1) * M_CHUNK] = res.T[0:ng, :]


def _gate_prep(gates, tm):
    m = gates.shape[0]
    ng = 4 * M_HEADS
    return pl.pallas_call(
        functools.partial(_gateprep_kernel, n_chunks=tm // M_CHUNK),
        grid=(m // tm,),
        in_specs=[pl.BlockSpec((tm, LANES), lambda i: (i, 0))],
        out_specs=pl.BlockSpec((ng, tm), lambda i: (0, i)),
        out_shape=jax.ShapeDtypeStruct((ng, m), F32),
        compiler_params=_params(),
        name="mlstm_gate_prep",
    )(gates)


M_AUG = 16


def _mlstm_chunk(k, qt, vta, i_row, b_row, c_col, btot, c_ref, m_ref, mask):
    scale = M_QK_DIM ** -0.5
    dv = M_V_DIM
    m_prev = m_ref[...]
    c_prev = c_ref[...]
    g_row = b_row + m_prev
    dmat = jnp.where(mask, b_row + c_col, -jnp.inf)
    m_j = jnp.maximum(g_row, jnp.max(dmat, axis=0, keepdims=True))
    inter = jnp.exp(g_row - m_j)
    st = jnp.dot(k, qt, preferred_element_type=F32) * scale * jnp.exp(dmat - m_j)
    num = (inter * scale) * jnp.dot(c_prev.astype(BF16), qt, preferred_element_type=F32)
    num = num + jnp.dot(vta, st.astype(BF16), preferred_element_type=F32)
    den = num[dv:dv + 1, :]
    h = num[0:dv, :] / jnp.maximum(jnp.abs(den), jnp.exp(-m_j))

    a_row = btot - b_row + i_row
    m_loc = jnp.max(a_row, axis=-1, keepdims=True)
    w_row = jnp.exp(a_row - m_loc)
    m_new = jnp.maximum(btot + m_prev, m_loc)
    s_old = jnp.exp(btot + m_prev - m_new)
    s_new = jnp.exp(m_loc - m_new)
    c_loc = jnp.dot((vta.astype(F32) * w_row).astype(BF16), k, preferred_element_type=F32)
    c_ref[...] = s_old * c_prev + s_new * c_loc
    m_ref[...] = m_new
    return h


def _mlstm_kernel(kl, qtl, vtl, arl, kc, qtc, vtc, arc, ng_ref, out_l, out_c,
                  hf_l, hb_l, hf_c, hb_c, cf_ref, mf_ref, cb_ref, mb_ref, *, n_lat, n_ctx):
    L = M_CHUNK
    dv = M_V_DIM
    cf_ref[...] = jnp.zeros_like(cf_ref)
    mf_ref[...] = jnp.full_like(mf_ref, -jnp.inf)
    cb_ref[...] = jnp.zeros_like(cb_ref)
    mb_ref[...] = jnp.full_like(mb_ref, -jnp.inf)
    r = lax.broadcasted_iota(jnp.int32, (L, L), 0)
    c = lax.broadcasted_iota(jnp.int32, (L, L), 1)
    mask_f = r <= c
    mask_b = r >= c
    aug = (lax.broadcasted_iota(jnp.int32, (M_AUG, L), 0) == 0).astype(BF16)
    ng = jnp.broadcast_to(ng_ref[0], (dv, L))

    def one(k_r, qt_r, vt_r, ar_r, h_r, n, first, tot_lane, c_ref, m_ref, mask):
        rows = pl.ds(pl.multiple_of(n * L, L), L)
        ar = ar_r[0, :, rows]
        i_row, b_row = ar[first:first + 1, :], ar[first + 1:first + 2, :]
        c_col = jnp.broadcast_to(i_row - b_row, (8, L)).T[:, 0:1]
        vta = jnp.concatenate([vt_r[:, rows], aug], axis=0)
        h_r[:, rows] = _mlstm_chunk(k_r[rows, :], qt_r[:, rows], vta, i_row, b_row, c_col,
                                    b_row[:, tot_lane:tot_lane + 1], c_ref, m_ref, mask)

    def pair(k_r, qt_r, vt_r, ar_r, hf_r, hb_r, n_f, n_b):
        one(k_r, qt_r, vt_r, ar_r, hf_r, n_f, 0, L - 1, cf_ref, mf_ref, mask_f)
        one(k_r, qt_r, vt_r, ar_r, hb_r, n_b, 2, 0, cb_ref, mb_ref, mask_b)

    def finish(hf_r, hb_r, out_r, n):
        rows = pl.ds(pl.multiple_of(n * L, L), L)
        h = hf_r[:, rows] + hb_r[:, rows]
        y = h * lax.rsqrt(jnp.mean(h * h, axis=0, keepdims=True) + EPS) * ng
        out_r[rows, :] = y.T.astype(out_r.dtype)

    for n in range(n_ctx):
        pair(kc, qtc, vtc, arc, hf_c, hb_c, n, n_ctx - 1 - n)

    def body(t, carry):
        pair(kl, qtl, vtl, arl, hf_l, hb_l, t, n_lat - 1 - t)
        return carry
    lax.fori_loop(0, n_lat, body, 0, unroll=4)

    for n in range(n_ctx):
        finish(hf_c, hb_c, out_c, n)

    def fin_body(t, carry):
        finish(hf_l, hb_l, out_l, t)
        return carry
    lax.fori_loop(0, n_lat, fin_body, 0, unroll=4)


def _mlstm(z, qt, vt, auxr, norm_g, n_batch, t_lat, t_ctx):
    dk, dv, nh = M_QK_DIM, M_V_DIM, M_HEADS
    cb = n_batch * t_lat // t_ctx
    in_specs = [
        pl.BlockSpec((t_lat, dk), lambda b, h: (b, Z_MK // dk + h)),
        pl.BlockSpec((dk, t_lat), lambda b, h: (h, b)), pl.BlockSpec((dv, t_lat), lambda b, h: (h, b)),
        pl.BlockSpec((1, 8, t_lat), lambda b, h: (h, 0, b)),
        pl.BlockSpec((t_ctx, dk), lambda b, h: (cb + b, Z_MK // dk + h)),
        pl.BlockSpec((dk, t_ctx), lambda b, h: (h, cb + b)), pl.BlockSpec((dv, t_ctx), lambda b, h: (h, cb + b)),
        pl.BlockSpec((1, 8, t_ctx), lambda b, h: (h, 0, cb + b)),
        pl.BlockSpec((1, dv, 1), lambda b, h: (h, 0, 0)),
    ]
    out_specs = [pl.BlockSpec((t_lat, dv), lambda b, h: (b, h)), pl.BlockSpec((t_ctx, dv), lambda b, h: (b, h))]
    out_shape = [jax.ShapeDtypeStruct((n_batch * t_lat, nh * dv), BF16),
                 jax.ShapeDtypeStruct((n_batch * t_ctx, nh * dv), BF16)]
    scratch = [pltpu.VMEM((dv, t_lat), F32), pltpu.VMEM((dv, t_lat), F32),
               pltpu.VMEM((dv, t_ctx), F32), pltpu.VMEM((dv, t_ctx), F32),
               pltpu.VMEM((dv + M_AUG, dk), F32), pltpu.VMEM((1, 1), F32),
               pltpu.VMEM((dv + M_AUG, dk), F32), pltpu.VMEM((1, 1), F32)]
    return pl.pallas_call(
        functools.partial(_mlstm_kernel, n_lat=t_lat // M_CHUNK, n_ctx=t_ctx // M_CHUNK),
        grid=(n_batch, nh),
        in_specs=in_specs, out_specs=out_specs, out_shape=out_shape, scratch_shapes=scratch,
        compiler_params=_params(),
        name="mlstm",
    )(z, qt, vt, auxr, z, qt, vt, auxr, norm_g.reshape(nh, dv, 1))


def _conv_kernel(a_ref, g_ref, ap_ref, gp_ref, an_ref, gn_ref, w_ref, b_ref, lg_ref, lb_ref, o_ref, y_scr, c_scr,
                 sh_scr, *, tt, tiles_per_seq, row_sub, col_w):
    i = pl.program_id(1)
    hl = CONV_HALO
    glu = lambda a, g: a.astype(F32) * _sigmoid(g.astype(F32))
    prev_ok = (i > 0).astype(F32)
    next_ok = (i < tiles_per_seq - 1).astype(F32)
    y_scr[0:hl, :] = glu(ap_ref[...], gp_ref[...]) * prev_ok
    y_scr[hl:hl + tt, :] = glu(a_ref[...], g_ref[...])
    y_scr[hl + tt:hl + tt + hl, :] = glu(an_ref[...], gn_ref[...]) * next_ok
    d = a_ref.shape[1]
    sub = 8
    taps = [[j for j in range(CONV_WIDTH) if (j + hl - CONV_WIDTH // 2) % sub == r] for r in range(sub)]
    span = row_sub + 2 * hl

    def col_body(cb, carry):
        cols = pl.ds(pl.multiple_of(cb * col_w, col_w), col_w)
        wc = w_ref[:, cols]
        for n, r0 in enumerate(range(0, tt, row_sub)):
            acc = jnp.zeros((row_sub, col_w), F32)
            for r in range(sub):
                slot = (n % 2) * sub + r
                sh_scr[slot] = y_scr[r0 + r:r0 + r + span - sub, cols]
                for j in taps[r]:
                    a0 = j + hl - CONV_WIDTH // 2 - r
                    acc = acc + sh_scr[slot, a0:a0 + row_sub, :] * wc[j:j + 1, :]
            c_scr[r0:r0 + row_sub, cols] = acc
        return carry
    lax.fori_loop(0, d // col_w, col_body, 0)
    y = c_scr[...] + b_ref[...]
    mu = jnp.mean(y, axis=-1, keepdims=True)
    yc = y - mu
    var = jnp.mean(yc * yc, axis=-1, keepdims=True)
    o_ref[...] = _silu(yc * lax.rsqrt(var + EPS) * lg_ref[...] + lb_ref[...]).astype(o_ref.dtype)


def _conv_branch(z, row0, n_seq, t_seq, tt, conv_w, conv_b, ln_g, ln_b):
    d = conv_w.shape[1]
    hl = CONV_HALO
    tps = t_seq // tt
    rb0, hb0, hps = row0 // tt, row0 // hl, t_seq // hl
    ca, cg = Z_CA // d, Z_CG // d
    cur = lambda col: pl.BlockSpec((tt, d), lambda s, i: (rb0 + s * tps + i, col))
    prv = lambda col: pl.BlockSpec(
        (hl, d), lambda s, i: (hb0 + s * hps + jnp.maximum(i * (tt // hl) - 1, 0), col))
    nxt = lambda col: pl.BlockSpec(
        (hl, d), lambda s, i: (hb0 + s * hps + jnp.minimum((i + 1) * (tt // hl), hps - 1), col))
    vec = pl.BlockSpec((1, d), lambda s, i: (0, 0))
    wpad = jnp.zeros((32, d), F32).at[:CONV_WIDTH].set(conv_w)
    row_sub, col_w = 128, LANES
    return pl.pallas_call(
        functools.partial(_conv_kernel, tt=tt, tiles_per_seq=tps, row_sub=row_sub, col_w=col_w),
        grid=(n_seq, tps),
        in_specs=[cur(ca), cur(cg), prv(ca), prv(cg), nxt(ca), nxt(cg),
                  pl.BlockSpec((32, d), lambda s, i: (0, 0)), vec, vec, vec],
        out_specs=pl.BlockSpec((tt, d), lambda s, i: (s * tps + i, 0)),
        out_shape=jax.ShapeDtypeStruct((n_seq * t_seq, d), BF16),
        scratch_shapes=[pltpu.VMEM((tt + 2 * hl, d), F32), pltpu.VMEM((tt, d), F32),
                        pltpu.VMEM((16, row_sub + 2 * hl - 8, col_w), F32)],
        compiler_params=_params(),
        name="conv_branch",
    )(z, z, z, z, z, z, wpad, conv_b.reshape(1, d), ln_g.reshape(1, d), ln_b.reshape(1, d))


def _headnorm_rope(x, bd, gain, cos, sin, out_scale=1.0):
    x2 = x * x
    hi = x2.astype(BF16)
    lo = (x2 - hi.astype(F32)).astype(BF16)
    ss = jnp.dot(hi, bd, preferred_element_type=F32) + jnp.dot(lo, bd, preferred_element_type=F32)
    xn = x * lax.rsqrt(ss * (1.0 / A_HEAD_DIM) + EPS) * gain
    w = x.shape[1]
    lane = lax.broadcasted_iota(jnp.int32, x.shape, 1)
    first = (lane % (2 * ROPE_FREQS)) < ROPE_FREQS
    swapped = jnp.where(first, pltpu.roll(xn, w - ROPE_FREQS, 1), pltpu.roll(xn, ROPE_FREQS, 1))
    return (xn * cos + swapped * sin) * out_scale


def _qkprep_kernel(q_ref, k_ref, cos_ref, sin_ref, qg_ref, kg_ref, qo_ref, ko_ref):
    gw = A_GROUP * A_HEAD_DIM
    r = lax.broadcasted_iota(jnp.int32, (gw, gw), 0) // A_HEAD_DIM
    c = lax.broadcasted_iota(jnp.int32, (gw, gw), 1) // A_HEAD_DIM
    bd = (r == c).astype(BF16)
    cos, sin = cos_ref[...], sin_ref[...]
    for g in range(q_ref.shape[1] // gw):
        x = q_ref[:, g * gw:(g + 1) * gw].astype(F32)
        qo_ref[:, g * gw:(g + 1) * gw] = _headnorm_rope(x, bd, qg_ref[...], cos, sin,
                                                        A_HEAD_DIM ** -0.5).astype(qo_ref.dtype)
    ko_ref[...] = _headnorm_rope(k_ref[...].astype(F32), bd, kg_ref[...], cos, sin).astype(ko_ref.dtype)


def _qk_prep(z, cos_tab, sin_tab, q_g, k_g, tm, n_lat_rows, t_lat):
    m = z.shape[0]
    qw, kw = A_HEADS * A_HEAD_DIM, A_KV_HEADS * A_HEAD_DIM
    n_lat_tiles, tpb = n_lat_rows // tm, t_lat // tm
    tab = pl.BlockSpec((tm, kw), lambda i: (jnp.where(i < n_lat_tiles, i % tpb, tpb), 0))
    vec = pl.BlockSpec((1, kw), lambda i: (0, 0))
    return pl.pallas_call(
        _qkprep_kernel,
        grid=(m // tm,),
        in_specs=[pl.BlockSpec((tm, qw), lambda i: (i, Z_AQ // qw)),
                  pl.BlockSpec((tm, kw), lambda i: (i, Z_AK // kw)), tab, tab, vec, vec],
        out_specs=[pl.BlockSpec((tm, qw), lambda i: (i, 0)), pl.BlockSpec((tm, kw), lambda i: (i, 0))],
        out_shape=[jax.ShapeDtypeStruct((m, qw), BF16), jax.ShapeDtypeStruct((m, kw), BF16)],
        compiler_params=_params(),
        name="attn_qk_prep",
    )(z, z, cos_tab, sin_tab, jnp.tile(q_g, A_KV_HEADS).reshape(1, kw), jnp.tile(k_g, A_KV_HEADS).reshape(1, kw))


def _attn_kernel(sink_ref, q_ref, k_ref, v_ref, kc_ref, vc_ref, o_ref, *, local, t_seq):
    nq = pl.program_id(1)
    blk, dh, grp = A_BLOCK, A_HEAD_DIM, A_GROUP
    nt = (((1,), (1,)), ((), ()))
    if local:
        win = 3 * blk
        start = pl.multiple_of(jnp.clip(nq * blk - blk, 0, t_seq - win), blk)
        qpos = nq * blk + lax.broadcasted_iota(jnp.int32, (grp * blk, win), 0) % blk
        kpos = start + lax.broadcasted_iota(jnp.int32, (grp * blk, win), 1)
        near = jnp.abs(kpos - qpos) <= WINDOW
    rowgrp = lax.broadcasted_iota(jnp.int32, (grp * blk, 1), 0) // blk

    def with_ones(v):
        col = lax.broadcasted_iota(jnp.int32, (v.shape[0], dh), 1)
        return jnp.concatenate([v, (col == 0).astype(v.dtype)], axis=1)

    outs = []
    for h in range(A_KV_HEADS):
        cols = slice(h * dh, (h + 1) * dh)
        qs = jnp.concatenate([q_ref[:, (h * grp + g) * dh:(h * grp + g + 1) * dh] for g in range(grp)], axis=0)
        sink = jnp.zeros((grp * blk, 1), F32)
        for g in range(grp):
            sink = jnp.where(rowgrp == g, sink_ref[h * grp + g], sink)
        s_ctx = lax.dot_general(qs, kc_ref[:, cols], nt, preferred_element_type=F32)
        mx = jnp.maximum(jnp.max(s_ctx, axis=-1, keepdims=True), sink)
        if local:
            s_loc = lax.dot_general(qs, k_ref[pl.ds(start, win), cols], nt, preferred_element_type=F32)
            s_loc = jnp.where(near, s_loc, -jnp.inf)
            mx = jnp.maximum(mx, jnp.max(s_loc, axis=-1, keepdims=True))
        o = jnp.dot(jnp.exp(s_ctx - mx).astype(BF16), with_ones(vc_ref[:, cols]), preferred_element_type=F32)
        if local:
            o = o + jnp.dot(jnp.exp(s_loc - mx).astype(BF16), with_ones(v_ref[pl.ds(start, win), cols]),
                            preferred_element_type=F32)
        o = o[:, 0:dh] / (o[:, dh:dh + 1] + jnp.exp(sink - mx))
        outs.extend(o[g * blk:(g + 1) * blk, :] for g in range(grp))
    o_ref[...] = jnp.concatenate(outs, axis=1).astype(o_ref.dtype)


def _attention(sink, q, k, v, q_row0, n_batch, t_seq, t_ctx, ctx_row0, local):
    qw, kw = A_HEADS * A_HEAD_DIM, A_KV_HEADS * A_HEAD_DIM
    nqb = t_seq // A_BLOCK
    qb0, sb0, cb0 = q_row0 // A_BLOCK, q_row0 // t_seq, ctx_row0 // t_ctx
    zv, vcol = v
    return pl.pallas_call(
        functools.partial(_attn_kernel, local=local, t_seq=t_seq),
        grid_spec=pltpu.PrefetchScalarGridSpec(
            num_scalar_prefetch=1,
            grid=(n_batch, nqb),
            in_specs=[pl.BlockSpec((A_BLOCK, qw), lambda b, n, s: (qb0 + b * nqb + n, 0)),
                      pl.BlockSpec((t_seq, kw), lambda b, n, s: (sb0 + b, 0)),
                      pl.BlockSpec((t_seq, kw), lambda b, n, s: (sb0 + b, vcol)),
                      pl.BlockSpec((t_ctx, kw), lambda b, n, s: (cb0 + b, 0)),
                      pl.BlockSpec((t_ctx, kw), lambda b, n, s: (cb0 + b, vcol))],
            out_specs=pl.BlockSpec((A_BLOCK, qw), lambda b, n, s: (b * nqb + n, 0))),
        out_shape=jax.ShapeDtypeStruct((n_batch * t_seq, qw), BF16),
        compiler_params=_params(),
        name="attention_local" if local else "attention_ctx",
    )(sink, q, k, zv, k, zv)


def _merge_kernel(*refs, n_lat_tiles, has_ctx):
    n_br = 6 if has_ctx else 3
    o_ref, gl_ref, x_ref, mod_ref, wa_ref, wb_ref, wc_ref, wo_ref, xo_ref, m_scr = refs[n_br:]
    if has_ctx:
        is_ctx = pl.program_id(0) >= n_lat_tiles
        ya, yb, yc = (jnp.where(is_ctx, refs[2 * n + 1][...], refs[2 * n][...]) for n in range(3))
    else:
        ya, yb, yc = (refs[n][...] for n in range(3))
    d = x_ref.shape[1]
    ya = (ya.astype(F32) * _sigmoid(o_ref[...].astype(F32))).astype(BF16)
    for c0, cw in _chunks(d, 256):
        cs = slice(c0, c0 + cw)
        acc = _sigmoid(gl_ref[:, c0:c0 + cw].astype(F32)) * jnp.dot(ya, wa_ref[:, cs], preferred_element_type=F32)
        acc = acc + _sigmoid(gl_ref[:, d + c0:d + c0 + cw].astype(F32)) * jnp.dot(
            yb, wb_ref[:, cs], preferred_element_type=F32)
        acc = acc + _sigmoid(gl_ref[:, 2 * d + c0:2 * d + c0 + cw].astype(F32)) * jnp.dot(
            yc, wc_ref[:, cs], preferred_element_type=F32)
        m_scr[:, cs] = acc.astype(BF16)
    mm = m_scr[...]
    for c0, cw in _chunks(d, 256):
        cs = slice(c0, c0 + cw)
        xo_ref[:, cs] = x_ref[:, cs] + mod_ref[0, 2:3, cs] * jnp.dot(mm, wo_ref[:, cs], preferred_element_type=F32)


def _merge(branches, z, x, modtab, wa, wb, wc, wo, tm, n_rows, n_lat_rows, t_lat):
    d = x.shape[1]
    n_batch = modtab.shape[0] - 1
    n_lat_tiles, tpb = n_lat_rows // tm, t_lat // tm
    has_ctx = branches[0][1] is not None
    row = pl.BlockSpec((tm, d), lambda i: (i, 0))
    lat = pl.BlockSpec((tm, d), lambda i: (jnp.minimum(i, n_lat_tiles - 1), 0))
    ctx = pl.BlockSpec((tm, d), lambda i: (jnp.maximum(i - n_lat_tiles, 0), 0))
    wsp = pl.BlockSpec((d, d), lambda i: (0, 0))
    br_args = [a for pair in branches for a in (pair if has_ctx else pair[:1])]
    br_specs = [lat, ctx] * 3 if has_ctx else [lat] * 3
    return pl.pallas_call(
        functools.partial(_merge_kernel, n_lat_tiles=n_lat_tiles, has_ctx=has_ctx),
        grid=(n_rows // tm,),
        in_specs=br_specs + [pl.BlockSpec((tm, d), lambda i: (i, Z_MO // d)),
                             pl.BlockSpec((tm, 3 * d), lambda i: (i, Z_GATE // (3 * d))), row,
                             pl.BlockSpec((1, 6, d), lambda i: (_mod_index(i, n_lat_tiles, tpb, n_batch), 0, 0)),
                             wsp, wsp, wsp, wsp],
        out_specs=row,
        out_shape=jax.ShapeDtypeStruct(x.shape, F32),
        scratch_shapes=[pltpu.VMEM((tm, d), BF16)],
        input_output_aliases={len(br_args) + 2: 0},
        compiler_params=_params(),
        name="merge_out_proj",
    )(*br_args, z, z, x, modtab, wa, wb, wc, wo)


def _ffn_kernel(x_ref, mod_ref, g_ref, w1_ref, w3_ref, w2_ref, xo_ref, acc_scr, *, f_chunks):
    h = _norm_mod(x_ref[...], g_ref[...], mod_ref[0, 3:4, :], mod_ref[0, 4:5, :]).astype(BF16)
    for n, (c0, cw) in enumerate(f_chunks):
        a = jnp.dot(h, w1_ref[:, c0:c0 + cw], preferred_element_type=F32)
        b = jnp.dot(h, w3_ref[:, c0:c0 + cw], preferred_element_type=F32)
        u = (_silu(a) * b).astype(BF16)
        o = jnp.dot(u, w2_ref[c0:c0 + cw, :], preferred_element_type=F32)
        if n == 0:
            acc_scr[...] = o
        else:
            acc_scr[...] += o
    xo_ref[...] = x_ref[...] + mod_ref[0, 5:6, :] * acc_scr[...]


def _dense_ffn(x, modtab, g2, w1, w3, w2, tm, n_rows, n_lat_rows, t_lat):
    d, f = w1.shape
    n_batch = modtab.shape[0] - 1
    n_lat_tiles, tpb = n_lat_rows // tm, t_lat // tm
    row = pl.BlockSpec((tm, d), lambda i: (i, 0))
    return pl.pallas_call(
        functools.partial(_ffn_kernel, f_chunks=_chunks(f, 256)),
        grid=(n_rows // tm,),
        in_specs=[row, pl.BlockSpec((1, 6, d), lambda i: (_mod_index(i, n_lat_tiles, tpb, n_batch), 0, 0)),
                  pl.BlockSpec((1, d), lambda i: (0, 0)),
                  pl.BlockSpec((d, f), lambda i: (0, 0)), pl.BlockSpec((d, f), lambda i: (0, 0)),
                  pl.BlockSpec((f, d), lambda i: (0, 0))],
        out_specs=row,
        out_shape=jax.ShapeDtypeStruct(x.shape, F32),
        scratch_shapes=[pltpu.VMEM((tm, d), F32)],
        input_output_aliases={0: 0},
        compiler_params=_params(),
        name="dense_ffn",
    )(x, modtab, g2, w1, w3, w2)


R_E1, R_E2, R_R1, R_R2, R_P1, R_P2 = range(6)
HI16 = 0xFFFF0000


def _pack_bf16_pair(lo, hi):
    lo_bits = lax.bitcast_convert_type(lo.astype(BF16).astype(F32), jnp.uint32)
    hi_bits = lax.bitcast_convert_type(hi.astype(BF16).astype(F32), jnp.uint32)
    return (hi_bits & jnp.uint32(HI16)) | (lo_bits >> 16)


def _unpack_bf16_pair(u):
    lo = lax.bitcast_convert_type(u << 16, F32).astype(BF16)
    hi = lax.bitcast_convert_type(u & jnp.uint32(HI16), F32).astype(BF16)
    return lo, hi


def _router_kernel(x_ref, mod_ref, g_ref, rw_ref, hp_ref, r_ref, cnt_ref, carry_scr):
    tm, d = x_ref.shape

    @pl.when(pl.program_id(0) == 0)
    def _():
        carry_scr[...] = jnp.zeros_like(carry_scr)

    h = _norm_mod(x_ref[...], g_ref[...], mod_ref[0, 3:4, :], mod_ref[0, 4:5, :])
    hp_ref[...] = _pack_bf16_pair(h[:, :d // 2], h[:, d // 2:])
    logits = jnp.dot(h, rw_ref[...], preferred_element_type=F32, precision=lax.Precision.HIGHEST)
    lane = lax.broadcasted_iota(jnp.int32, (tm, LANES), 1).astype(F32)
    logits = jnp.where(lane < N_EXPERTS, logits, -jnp.inf)
    v1 = jnp.max(logits, axis=-1, keepdims=True)
    e1 = jnp.min(jnp.where(logits == v1, lane, float(LANES)), axis=-1, keepdims=True)
    rest = jnp.where(lane == e1, -jnp.inf, logits)
    v2 = jnp.max(rest, axis=-1, keepdims=True)
    e2 = jnp.min(jnp.where(rest == v2, lane, float(LANES)), axis=-1, keepdims=True)
    t = jnp.exp(v2 - v1)
    p1 = 1.0 / (1.0 + t)
    p2 = t / (1.0 + t)
    hot1 = lane == e1
    hot2 = lane == e2
    hot = jnp.where(hot1 | hot2, 1.0, 0.0)
    rr = lax.broadcasted_iota(jnp.int32, (tm, tm), 0)
    cc = lax.broadcasted_iota(jnp.int32, (tm, tm), 1)
    before = jnp.where(cc < rr, 1.0, 0.0).astype(BF16)
    prefix = jnp.dot(before, hot.astype(BF16), preferred_element_type=F32) + carry_scr[...]
    r1 = jnp.sum(jnp.where(hot1, prefix, 0.0), axis=-1, keepdims=True)
    r2 = jnp.sum(jnp.where(hot2, prefix, 0.0), axis=-1, keepdims=True)
    carry_scr[...] += jnp.sum(hot, axis=0, keepdims=True)
    cnt_ref[...] = carry_scr[...]
    out = jnp.zeros((tm, LANES), F32)
    for idx, val in ((R_E1, e1), (R_E2, e2), (R_R1, r1), (R_R2, r2), (R_P1, p1), (R_P2, p2)):
        out = jnp.where(lane == float(idx), val, out)
    r_ref[...] = out


def _router(x, modtab, g2, rw, tm, n_rows, t_lat):
    d = x.shape[1]
    tpb = t_lat // tm
    return pl.pallas_call(
        _router_kernel,
        grid=(n_rows // tm,),
        in_specs=[pl.BlockSpec((tm, d), lambda i: (i, 0)), pl.BlockSpec((1, 6, d), lambda i: (i // tpb, 0, 0)),
                  pl.BlockSpec((1, d), lambda i: (0, 0)), pl.BlockSpec((d, LANES), lambda i: (0, 0))],
        out_specs=[pl.BlockSpec((tm, d // 2), lambda i: (i, 0)), pl.BlockSpec((tm, LANES), lambda i: (i, 0)),
                   pl.BlockSpec((1, LANES), lambda i: (0, 0))],
        out_shape=[jax.ShapeDtypeStruct((n_rows, d // 2), jnp.uint32), jax.ShapeDtypeStruct((n_rows, LANES), F32),
                   jax.ShapeDtypeStruct((1, LANES), F32)],
        scratch_shapes=[pltpu.VMEM((1, LANES), F32)],
        compiler_params=_params(),
        name="moe_router",
    )(x, modtab, g2, rw)


def _row_copies(n, make, unroll=8):
    def start(g, c):
        for u in range(unroll):
            for cp in make(g * unroll + u):
                cp.start(priority=u % 2)
        return c
    lax.fori_loop(0, n // unroll, start, 0)

    def wait(g, c):
        for u in range(unroll):
            for cp in make(g * unroll + u):
                cp.wait()
        return c
    lax.fori_loop(0, n // unroll, wait, 0)


def _dispatch_kernel(d1_ref, d2_ref, hp_ref, xs_in, xs_hbm, sem):
    del xs_in

    def make(r):
        src = hp_ref.at[pl.ds(r, 1), :]
        return (pltpu.make_async_copy(src, xs_hbm.at[pl.ds(d1_ref[0, 0, r], 1), :], sem),
                pltpu.make_async_copy(src, xs_hbm.at[pl.ds(d2_ref[0, 0, r], 1), :], sem))
    _row_copies(hp_ref.shape[0], make)


def _dispatch(d1, d2, hp, n_sorted, tm):
    n_rows, w = hp.shape
    idx = pl.BlockSpec((1, 1, tm), lambda i: (i, 0, 0), memory_space=pltpu.SMEM)
    return pl.pallas_call(
        _dispatch_kernel,
        grid=(n_rows // tm,),
        in_specs=[idx, idx, pl.BlockSpec((tm, w), lambda i: (i, 0)), pl.BlockSpec(memory_space=pl.ANY)],
        out_specs=pl.BlockSpec(memory_space=pl.ANY),
        out_shape=jax.ShapeDtypeStruct((n_sorted, w), jnp.uint32),
        scratch_shapes=[pltpu.SemaphoreType.DMA(())],
        input_output_aliases={3: 0},
        compiler_params=_params(),
        name="moe_dispatch",
    )(d1.reshape(-1, 1, tm), d2.reshape(-1, 1, tm), hp, jnp.zeros((n_sorted, w), jnp.uint32))


def _expert_kernel(te_ref, nv_ref, xs_ref, w1_ref, w3_ref, w2_ref, y_ref, xb_scr, acc_scr, *, n_f):
    i, f = pl.program_id(0), pl.program_id(1)
    valid = i < nv_ref[0]
    half = xs_ref.shape[1]

    @pl.when(valid & (f == 0))
    def _():
        lo, hi = _unpack_bf16_pair(xs_ref[...])
        xb_scr[:, :half] = lo
        xb_scr[:, half:] = hi

    @pl.when(valid)
    def _():
        xb = xb_scr[...]

        @pl.when(f == 0)
        def _():
            acc_scr[...] = jnp.zeros_like(acc_scr)

        for c0, cw in _chunks(w1_ref.shape[2], 256):
            a = jnp.dot(xb, w1_ref[0, :, c0:c0 + cw].astype(BF16), preferred_element_type=F32)
            b = jnp.dot(xb, w3_ref[0, :, c0:c0 + cw].astype(BF16), preferred_element_type=F32)
            acc_scr[...] += jnp.dot((_silu(a) * b).astype(BF16), w2_ref[0, c0:c0 + cw, :].astype(BF16),
                                    preferred_element_type=F32)

        @pl.when(f == n_f - 1)
        def _():
            y_ref[...] = acc_scr[...]

    @pl.when(jnp.logical_not(valid) & (f == n_f - 1))
    def _():
        y_ref[...] = jnp.zeros_like(y_ref)


def _experts(tile_expert, n_valid, xs, w1, w3, w2, tme, tf):
    ne, d, f = w1.shape
    n_tiles = xs.shape[0] // tme
    n_f = f // tf
    return pl.pallas_call(
        functools.partial(_expert_kernel, n_f=n_f),
        grid_spec=pltpu.PrefetchScalarGridSpec(
            num_scalar_prefetch=2,
            grid=(n_tiles, n_f),
            in_specs=[pl.BlockSpec((tme, d // 2), lambda i, j, te, nv: (i, 0)),
                      pl.BlockSpec((1, d, tf), lambda i, j, te, nv: (te[i], 0, j)),
                      pl.BlockSpec((1, d, tf), lambda i, j, te, nv: (te[i], 0, j)),
                      pl.BlockSpec((1, tf, d), lambda i, j, te, nv: (te[i], j, 0))],
            out_specs=pl.BlockSpec((tme, d), lambda i, j, te, nv: (i, 0)),
            scratch_shapes=[pltpu.VMEM((tme, d), BF16), pltpu.VMEM((tme, d), F32)]),
        out_shape=jax.ShapeDtypeStruct((xs.shape[0], d), F32),
        compiler_params=_params(),
        name="moe_experts",
    )(tile_expert, n_valid, xs, w1, w3, w2)


def _combine_kernel(d1_ref, d2_ref, x_ref, r_ref, mod_ref, y_hbm, xo_ref, y1_scr, y2_scr, sem):
    def make(r):
        return (pltpu.make_async_copy(y_hbm.at[pl.ds(d1_ref[0, 0, r], 1), :], y1_scr.at[pl.ds(r, 1), :], sem),
                pltpu.make_async_copy(y_hbm.at[pl.ds(d2_ref[0, 0, r], 1), :], y2_scr.at[pl.ds(r, 1), :], sem))
    _row_copies(x_ref.shape[0], make)
    r = r_ref[...]
    p1 = r[:, R_P1:R_P1 + 1]
    p2 = r[:, R_P2:R_P2 + 1]
    xo_ref[...] = x_ref[...] + mod_ref[0, 5:6, :] * (p1 * y1_scr[...] + p2 * y2_scr[...])


def _combine(d1, d2, x, y, r, modtab, tm, n_rows, t_lat):
    d = x.shape[1]
    tpb = t_lat // tm
    idx = pl.BlockSpec((1, 1, tm), lambda i: (i, 0, 0), memory_space=pltpu.SMEM)
    return pl.pallas_call(
        _combine_kernel,
        grid=(n_rows // tm,),
        in_specs=[idx, idx, pl.BlockSpec((tm, d), lambda i: (i, 0)), pl.BlockSpec((tm, LANES), lambda i: (i, 0)),
                  pl.BlockSpec((1, 6, d), lambda i: (i // tpb, 0, 0)), pl.BlockSpec(memory_space=pl.ANY)],
        out_specs=pl.BlockSpec((tm, d), lambda i: (i, 0)),
        out_shape=jax.ShapeDtypeStruct((n_rows, d), F32),
        scratch_shapes=[pltpu.VMEM((tm, d), F32), pltpu.VMEM((tm, d), F32), pltpu.SemaphoreType.DMA(())],
        compiler_params=_params(),
        name="moe_combine",
    )(d1.reshape(-1, 1, tm), d2.reshape(-1, 1, tm), x, r, modtab, y)


def _moe(x, modtab, g2, router_w, w1, w3, w2, n_rows, t_lat, tme=1024, tf=896, tm=512):
    d = x.shape[1]
    rw = jnp.zeros((d, LANES), F32).at[:, :N_EXPERTS].set(router_w)
    hp, r, cnt = _router(x, modtab, g2, rw, tm, n_rows, t_lat)
    e1, e2 = r[:, R_E1].astype(jnp.int32), r[:, R_E2].astype(jnp.int32)
    r1, r2 = r[:, R_R1].astype(jnp.int32), r[:, R_R2].astype(jnp.int32)
    counts = cnt[0, :N_EXPERTS].astype(jnp.int32)
    padded = (counts + tme - 1) // tme * tme
    ends = jnp.cumsum(padded)
    offs = ends - padded
    n_sorted = 2 * n_rows + N_EXPERTS * tme
    d1, d2 = offs[e1] + r1, offs[e2] + r2
    tile_start = jnp.arange(n_sorted // tme, dtype=jnp.int32) * tme
    n_valid = (ends[-1] // tme).astype(jnp.int32).reshape(1)
    tile_expert = jnp.minimum(jnp.sum(tile_start[:, None] >= ends[None, :], axis=1), N_EXPERTS - 1).astype(jnp.int32)
    tile_expert = jnp.where(tile_start < ends[-1], tile_expert, tile_expert[jnp.maximum(n_valid[0] - 1, 0)])
    xs = _dispatch(d1, d2, hp, n_sorted, tm)
    y = _experts(tile_expert, n_valid, xs, w1, w3, w2, tme, tf)
    return _combine(d1, d2, x, y, r, modtab, tm, n_rows, t_lat)


def _rope_tables(t_lat, pad_rows):
    rows = t_lat // GRID_W
    pos_r = jnp.repeat(jnp.arange(rows, dtype=F32), GRID_W)
    pos_c = jnp.tile(jnp.arange(GRID_W, dtype=F32), rows)
    inv = 1.0 / (ROPE_BASE ** (jnp.arange(ROPE_FREQS, dtype=F32) * 2.0 / (A_HEAD_DIM // 2)))
    ar, ac = pos_r[:, None] * inv, pos_c[:, None] * inv
    cos = jnp.concatenate([jnp.cos(ar), jnp.cos(ar), jnp.cos(ac), jnp.cos(ac)], axis=1)
    sin = jnp.concatenate([-jnp.sin(ar), jnp.sin(ar), -jnp.sin(ac), jnp.sin(ac)], axis=1)
    cos = jnp.concatenate([jnp.tile(cos, (1, A_KV_HEADS)), jnp.ones((pad_rows, A_KV_HEADS * A_HEAD_DIM), F32)])
    sin = jnp.concatenate([jnp.tile(sin, (1, A_KV_HEADS)), jnp.zeros((pad_rows, A_KV_HEADS * A_HEAD_DIM), F32)])
    return cos, sin


def _split_w_in(w, b):
    o = [0, 512, 1024, 2048, 3072, 3088, 5136, 6160, 6416, 6672, 9744]
    order = [(o[0], o[4]), (o[5], o[6]), (o[6], o[7]), (o[9], o[10]), (o[7], o[8]), (o[8], o[9])]
    w_z = jnp.concatenate([w[:, a:e] for a, e in order], axis=1).astype(BF16)
    b_z = jnp.concatenate([b[a:e] for a, e in order]).reshape(1, NZ)
    ng = 4 * M_HEADS
    w_g = jnp.zeros((w.shape[0], LANES), BF16).at[:, :ng].set(w[:, o[4]:o[5]].astype(BF16))
    b_g = jnp.zeros((1, LANES), F32).at[0, :ng].set(b[o[4]:o[5]])
    return w_z, b_z, w_g, b_g


def kernel(x, c, ctx, c_ctx, w_mod, b_mod, norm1_g, norm2_g, w_in, b_in, m_norm_g, conv_w, conv_b, conv_ln_g,
           conv_ln_b, q_norm_g, k_norm_g, attn_sink, w_branch_a, w_branch_b, w_branch_c, w_out, ffn_w1, ffn_w3,
           ffn_w2, router_w, moe_w1, moe_w3, moe_w2):
    n_batch, t_lat, d = x.shape
    t_ctx = ctx.shape[1]
    depth = w_mod.shape[0]
    n_lat, n_ctx = n_batch * t_lat, n_batch * t_ctx
    m = n_lat + n_ctx
    tm = min(1024, n_ctx)
    assert t_lat % tm == 0 and n_ctx % tm == 0 and t_lat >= 3 * A_BLOCK

    xs = jnp.concatenate([x.reshape(n_lat, d), ctx.reshape(n_ctx, d)], axis=0)
    cvec = jnp.zeros((16, d), F32).at[:n_batch].set(c).at[n_batch].set(c_ctx)
    mod = _modulation(cvec, w_mod, b_mod)[:, :n_batch + 1].reshape(depth, n_batch + 1, 6, d)
    cos_tab, sin_tab = _rope_tables(t_lat, tm)

    for l in range(depth):
        last = l == depth - 1
        modtab = mod[l]
        w_z, b_z, w_g, b_g = _split_w_in(w_in[l], b_in[l])
        z, gates, qt, vt = _in_projection(xs, modtab, norm1_g[l].reshape(1, d), w_z, b_z, w_g, b_g, tm, n_lat, t_lat)

        aux = _gate_prep(gates, tm).reshape(4, M_HEADS, m)
        auxr = jnp.pad(aux.transpose(1, 0, 2), ((0, 0), (0, 4), (0, 0)))
        ya_lat, ya_ctx = _mlstm(z, qt, vt, auxr, m_norm_g[l], n_batch, t_lat, t_ctx)

        conv_args = (conv_w[l], conv_b[l], conv_ln_g[l], conv_ln_b[l])
        yb_lat = _conv_branch(z, 0, n_batch, t_lat, min(512, t_lat), *conv_args)

        qp, kp = _qk_prep(z, cos_tab, sin_tab, q_norm_g[l], k_norm_g[l], tm, n_lat, t_lat)
        vsrc = (z, Z_AV // (A_KV_HEADS * A_HEAD_DIM))
        yc_lat = _attention(attn_sink[l], qp, kp, vsrc, 0, n_batch, t_lat, t_ctx, n_lat, True)

        wa, wb, wc, wo = (w.astype(BF16) for w in (w_branch_a[l], w_branch_b[l], w_branch_c[l], w_out[l]))
        if not last:
            yb_ctx = _conv_branch(z, n_lat, n_batch, t_ctx, t_ctx, *conv_args)
            yc_ctx = _attention(attn_sink[l], qp, kp, vsrc, n_lat, n_batch, t_ctx, t_ctx, n_lat, False)
            branches = ((ya_lat, ya_ctx), (yb_lat, yb_ctx), (yc_lat, yc_ctx))
            n_rows = m
        else:
            branches = ((ya_lat, None), (yb_lat, None), (yc_lat, None))
            n_rows = n_lat
        tmm = min(512, tm)
        xs = _merge(branches, z, xs, modtab, wa, wb, wc, wo, tmm, n_rows, n_lat, t_lat)

        g2 = norm2_g[l].reshape(1, d)
        if l % 2 == 0:
            i = l // 2
            xs = _dense_ffn(xs, modtab, g2, ffn_w1[i].astype(BF16), ffn_w3[i].astype(BF16), ffn_w2[i].astype(BF16),
                            tmm, n_rows, n_lat, t_lat)
        else:
            i = l // 2
            assert last, "expert mixer is implemented for latent rows only"
            xs = _moe(xs, modtab, g2, router_w[i], moe_w1[i], moe_w3[i], moe_w2[i], n_lat, t_lat)
    return xs[:n_lat].reshape(n_batch, t_lat, d)
```

```python
import functools

import jax
import jax.numpy as jnp
from jax import lax
from jax.experimental import pallas as pl
from jax.experimental.pallas import tpu as pltpu

F32 = jnp.float32
BF16 = jnp.bfloat16

EPS = 1e-6
GRID_W = 64
M_HEADS = 4
M_QK_DIM = 128
M_V_DIM = 256
M_CHUNK = 128
CONV_WIDTH = 31
CONV_HALO = 16
A_HEADS = 16
A_KV_HEADS = 4
A_GROUP = A_HEADS // A_KV_HEADS
A_HEAD_DIM = 64
A_BLOCK = 128
WINDOW = 128
ROPE_BASE = 10000.0
ROPE_FREQS = A_HEAD_DIM // 4
N_EXPERTS = 8
LANES = 128
VMEM_LIMIT = 56 * 1024 * 1024

Z_MQ, Z_MK, Z_MV, Z_MO = 0, 512, 1024, 2048
Z_CA, Z_CG, Z_AQ, Z_GATE, Z_AK, Z_AV = 3072, 4096, 5120, 6144, 9216, 9472
NZ = 9728


def _params(**kw):
    return pltpu.CompilerParams(vmem_limit_bytes=VMEM_LIMIT, **kw)


def _sigmoid(x):
    return 1.0 / (1.0 + jnp.exp(-x))


def _silu(x):
    return x * _sigmoid(x)


def _log_sigmoid(x):
    return jnp.minimum(x, 0.0) - jnp.log(1.0 + jnp.exp(-jnp.abs(x)))


def _norm_mod(x, g, shift, scale):
    ms = jnp.mean(x * x, axis=-1, keepdims=True)
    return (x * lax.rsqrt(ms + EPS) * g) * (1.0 + scale) + shift


def _chunks(n, w):
    out, c = [], 0
    while c < n:
        out.append((c, min(w, n - c)))
        c += w
    return out


def _mod_kernel(c_ref, w_ref, b_ref, o_ref):
    s = _silu(c_ref[...]).astype(BF16)
    o_ref[0] = jnp.dot(s, w_ref[0].astype(BF16), preferred_element_type=F32) + b_ref[0]


def _modulation(cvec, w_mod, b_mod):
    depth, d, n = w_mod.shape
    tn = 1536
    return pl.pallas_call(
        _mod_kernel,
        grid=(depth, n // tn),
        in_specs=[pl.BlockSpec((cvec.shape[0], d), lambda l, j: (0, 0)),
                  pl.BlockSpec((1, d, tn), lambda l, j: (l, 0, j)),
                  pl.BlockSpec((1, 1, tn), lambda l, j: (l, 0, j))],
        out_specs=pl.BlockSpec((1, cvec.shape[0], tn), lambda l, j: (l, 0, j)),
        out_shape=jax.ShapeDtypeStruct((depth, cvec.shape[0], n), F32),
        compiler_params=_params(),
        name="adaln_mod",
    )(cvec, w_mod, b_mod.reshape(depth, 1, n))


def _inproj_kernel(x0_ref, xn_ref, mod0_ref, modn_ref, g_ref, w_ref, b_ref, wg_ref, bg_ref, cos_ref, sin_ref, qg_ref,
                   kg_ref, z_ref, gates_ref, qt_ref, vt_ref, h_scr, *, tn, n_tiles):
    gw = A_GROUP * A_HEAD_DIM
    slot = pl.program_id(0) % 2

    def normalise(x_ref, mod_ref, dst_slot):
        h_scr[dst_slot] = _norm_mod(x_ref[...], g_ref[...], mod_ref[0, 0:1, :], mod_ref[0, 1:2, :]).astype(BF16)

    @pl.when((pl.program_id(0) == 0) & (pl.program_id(1) == 0))
    def _():
        normalise(x0_ref, mod0_ref, 0)

    def project(tile):
        h = h_scr[slot]
        if tile == 0:
            gates_ref[...] = jnp.dot(h, wg_ref[...], preferred_element_type=F32) + bg_ref[...]
        if tile == n_tiles - 1:
            normalise(xn_ref, modn_ref, 1 - slot)
        for c0, cw in _chunks(tn, gw):
            col = tile * tn + c0
            acc = jnp.dot(h, w_ref[:, c0:c0 + cw], preferred_element_type=F32) + b_ref[:, c0:c0 + cw]
            if Z_AQ <= col < Z_GATE:
                acc = _headnorm_rope(acc, _head_block_diag(), qg_ref[...], cos_ref[...], sin_ref[...],
                                     A_HEAD_DIM ** -0.5)
            elif Z_AK <= col < Z_AV:
                acc = _headnorm_rope(acc, _head_block_diag(), kg_ref[...], cos_ref[...], sin_ref[...])
            z_ref[:, c0:c0 + cw] = acc.astype(BF16)
            if Z_MQ <= col < Z_MK:
                qt_ref[col - Z_MQ:col - Z_MQ + cw, :] = acc.T.astype(BF16)
            if Z_MV <= col < Z_MO:
                vt_ref[col - Z_MV:col - Z_MV + cw, :] = acc.T.astype(BF16)

    for tile in range(n_tiles):
        @pl.when(pl.program_id(1) == tile)
        def _(tile=tile):
            project(tile)


def _mod_index(i, n_lat_tiles, tiles_per_batch, n_batch):
    return jnp.where(i < n_lat_tiles, i // tiles_per_batch, n_batch)


def _in_projection(x, modtab, g1, w_z, b_z, w_g, b_g, cos_tab, sin_tab, q_g, k_g, tm, n_lat_rows, t_lat):
    m, d = x.shape
    n_batch = modtab.shape[0] - 1
    n_tiles = 2
    tn = NZ // n_tiles
    gw = A_GROUP * A_HEAD_DIM
    assert tn % gw == 0 and Z_MO <= tn, "chunks must align with head groups; tile 0 holds the mLSTM q and v"
    n_lat_tiles, tpb = n_lat_rows // tm, t_lat // tm
    nxt = lambda i: jnp.minimum(i + 1, m // tm - 1)
    tab = pl.BlockSpec((tm, gw), lambda i, j: (jnp.where(i < n_lat_tiles, i % tpb, tpb), 0))
    vec = pl.BlockSpec((1, gw), lambda i, j: (0, 0))
    return pl.pallas_call(
        functools.partial(_inproj_kernel, tn=tn, n_tiles=n_tiles),
        grid=(m // tm, n_tiles),
        in_specs=[pl.BlockSpec((tm, d), lambda i, j: (0, 0)),
                  pl.BlockSpec((tm, d), lambda i, j: (nxt(i), 0)),
                  pl.BlockSpec((1, 6, d), lambda i, j: (_mod_index(0, n_lat_tiles, tpb, n_batch), 0, 0)),
                  pl.BlockSpec((1, 6, d), lambda i, j: (_mod_index(nxt(i), n_lat_tiles, tpb, n_batch), 0, 0)),
                  pl.BlockSpec((1, d), lambda i, j: (0, 0)),
                  pl.BlockSpec((d, tn), lambda i, j: (0, j)),
                  pl.BlockSpec((1, tn), lambda i, j: (0, j)),
                  pl.BlockSpec((d, LANES), lambda i, j: (0, 0)),
                  pl.BlockSpec((1, LANES), lambda i, j: (0, 0)),
                  tab, tab, vec, vec],
        out_specs=[pl.BlockSpec((tm, tn), lambda i, j: (i, j)),
                   pl.BlockSpec((tm, LANES), lambda i, j: (i, 0)),
                   pl.BlockSpec((Z_MK - Z_MQ, tm), lambda i, j: (0, i)),
                   pl.BlockSpec((Z_MO - Z_MV, tm), lambda i, j: (0, i))],
        out_shape=[jax.ShapeDtypeStruct((m, NZ), BF16), jax.ShapeDtypeStruct((m, LANES), F32),
                   jax.ShapeDtypeStruct((Z_MK - Z_MQ, m), BF16), jax.ShapeDtypeStruct((Z_MO - Z_MV, m), BF16)],
        scratch_shapes=[pltpu.VMEM((2, tm, d), BF16)],
        compiler_params=_params(),
        name="in_projection",
    )(x, x, modtab, modtab, g1, w_z, b_z, w_g, b_g, cos_tab, sin_tab, jnp.tile(q_g, A_GROUP).reshape(1, gw),
      jnp.tile(k_g, A_GROUP).reshape(1, gw))


def _gateprep_kernel(g_ref, o_ref, *, n_chunks):
    r = lax.broadcasted_iota(jnp.int32, (M_CHUNK, M_CHUNK), 0)
    c = lax.broadcasted_iota(jnp.int32, (M_CHUNK, M_CHUNK), 1)
    tri_lo = (c <= r).astype(F32)
    tri_up = (c >= r).astype(F32)
    lane = lax.broadcasted_iota(jnp.int32, (M_CHUNK, LANES), 1)
    ng = 4 * M_HEADS
    for n in range(n_chunks):
        g = g_ref[n * M_CHUNK:(n + 1) * M_CHUNK, :]
        lf = _log_sigmoid(g)
        cf = jnp.dot(tri_lo, lf, preferred_element_type=F32, precision=lax.Precision.HIGHEST)
        cb = jnp.dot(tri_up, lf, preferred_element_type=F32, precision=lax.Precision.HIGHEST)
        fwd_f = (lane >= M_HEADS) & (lane < 2 * M_HEADS)
        bwd_f = (lane >= 3 * M_HEADS) & (lane < 4 * M_HEADS)
        res = jnp.where(fwd_f, cf, jnp.where(bwd_f, cb, g))
        o_ref[:, n * M_CHUNK:(n + 1) * M_CHUNK] = res.T[0:ng, :]


def _gate_prep(gates, tm):
    m = gates.shape[0]
    ng = 4 * M_HEADS
    return pl.pallas_call(
        functools.partial(_gateprep_kernel, n_chunks=tm // M_CHUNK),
        grid=(m // tm,),
        in_specs=[pl.BlockSpec((tm, LANES), lambda i: (i, 0))],
        out_specs=pl.BlockSpec((ng, tm), lambda i: (0, i)),
        out_shape=jax.ShapeDtypeStruct((ng, m), F32),
        compiler_params=_params(),
        name="mlstm_gate_prep",
    )(gates)


M_AUG = 16


def _mlstm_chunk(k, qt, vta, i_row, b_row, c_col, btot, c_prev, m_prev, mask):
    scale = M_QK_DIM ** -0.5
    dv = M_V_DIM
    g_row = b_row + m_prev
    dmat = jnp.where(mask, b_row + c_col, -jnp.inf)
    m_j = jnp.maximum(g_row, jnp.max(dmat, axis=0, keepdims=True))
    inter = jnp.exp(g_row - m_j)
    st = jnp.dot(k, qt, preferred_element_type=F32) * scale * jnp.exp(dmat - m_j)
    lhs = jnp.concatenate([c_prev.astype(BF16), vta], axis=1)
    rhs = jnp.concatenate([(qt.astype(F32) * (inter * scale)).astype(BF16), st.astype(BF16)], axis=0)
    num = jnp.dot(lhs, rhs, preferred_element_type=F32)
    den = num[dv:dv + 1, :]
    h = num[0:dv, :] / jnp.maximum(jnp.abs(den), jnp.exp(-m_j))

    a_row = btot - b_row + i_row
    m_loc = jnp.max(a_row, axis=-1, keepdims=True)
    w_row = jnp.exp(a_row - m_loc)
    m_new = jnp.maximum(btot + m_prev, m_loc)
    s_old = jnp.exp(btot + m_prev - m_new)
    s_new = jnp.exp(m_loc - m_new)
    c_loc = jnp.dot((vta.astype(F32) * w_row).astype(BF16), k, preferred_element_type=F32)
    return h, s_old * c_prev + s_new * c_loc, m_new


M_HEAD_GROUP = 1


def _mlstm_kernel(kl, qtl, vtl, arl, kc, qtc, vtc, arc, ng_ref, out_l, out_c, hf_l, hb_l, hf_c, hb_c,
                  *, n_lat, n_ctx):
    L = M_CHUNK
    dk, dv, hg = M_QK_DIM, M_V_DIM, M_HEAD_GROUP
    r = lax.broadcasted_iota(jnp.int32, (L, L), 0)
    c = lax.broadcasted_iota(jnp.int32, (L, L), 1)
    mask_f = r <= c
    mask_b = r >= c
    aug = (lax.broadcasted_iota(jnp.int32, (M_AUG, L), 0) == 0).astype(BF16)
    ng = [jnp.broadcast_to(ng_ref[hh], (dv, L)) for hh in range(hg)]

    def one(hh, k_r, qt_r, vt_r, ar_r, h_r, n, first, tot_lane, state, mask):
        rows = pl.ds(pl.multiple_of(n * L, L), L)
        ar = ar_r[hh, :, rows]
        i_row, b_row = ar[first:first + 1, :], ar[first + 1:first + 2, :]
        c_col = jnp.broadcast_to(i_row - b_row, (8, L)).T[:, 0:1]
        vta = jnp.concatenate([vt_r[hh * dv:(hh + 1) * dv, rows], aug], axis=0)
        h, c_new, m_new = _mlstm_chunk(k_r[rows, hh * dk:(hh + 1) * dk], qt_r[hh * dk:(hh + 1) * dk, rows], vta,
                                       i_row, b_row, c_col, b_row[:, tot_lane:tot_lane + 1], *state, mask)
        h_r[hh, :, rows] = h
        return c_new, m_new

    def pair(k_r, qt_r, vt_r, ar_r, hf_r, hb_r, n_f, n_b, states):
        return tuple((one(hh, k_r, qt_r, vt_r, ar_r, hf_r, n_f, 0, L - 1, st_f, mask_f),
                      one(hh, k_r, qt_r, vt_r, ar_r, hb_r, n_b, 2, 0, st_b, mask_b))
                     for hh, (st_f, st_b) in enumerate(states))

    def finish(hf_r, hb_r, out_r, n):
        rows = pl.ds(pl.multiple_of(n * L, L), L)
        for hh in range(hg):
            h = hf_r[hh, :, rows] + hb_r[hh, :, rows]
            y = h * lax.rsqrt(jnp.mean(h * h, axis=0, keepdims=True) + EPS) * ng[hh]
            out_r[rows, hh * dv:(hh + 1) * dv] = y.T.astype(out_r.dtype)

    empty = (jnp.zeros((dv + M_AUG, dk), F32), jnp.full((1, 1), -jnp.inf, F32))
    states = tuple((empty, empty) for _ in range(hg))
    for n in range(n_ctx):
        states = pair(kc, qtc, vtc, arc, hf_c, hb_c, n, n_ctx - 1 - n, states)

    def body(t, states):
        return pair(kl, qtl, vtl, arl, hf_l, hb_l, t, n_lat - 1 - t, states)
    lax.fori_loop(0, n_lat, body, states, unroll=max(1, 4 // hg))

    for n in range(n_ctx):
        finish(hf_c, hb_c, out_c, n)

    def fin_body(t, carry):
        finish(hf_l, hb_l, out_l, t)
        return carry
    lax.fori_loop(0, n_lat, fin_body, 0, unroll=max(1, 4 // hg))


def _mlstm(z, qt, vt, auxr, norm_g, n_batch, t_lat, t_ctx):
    dk, dv, nh, hg = M_QK_DIM, M_V_DIM, M_HEADS, M_HEAD_GROUP
    gk, gv = hg * dk, hg * dv
    cb = n_batch * t_lat // t_ctx
    in_specs = [
        pl.BlockSpec((t_lat, gk), lambda b, h: (b, Z_MK // gk + h)),
        pl.BlockSpec((gk, t_lat), lambda b, h: (h, b)), pl.BlockSpec((gv, t_lat), lambda b, h: (h, b)),
        pl.BlockSpec((hg, 8, t_lat), lambda b, h: (h, 0, b)),
        pl.BlockSpec((t_ctx, gk), lambda b, h: (cb + b, Z_MK // gk + h)),
        pl.BlockSpec((gk, t_ctx), lambda b, h: (h, cb + b)), pl.BlockSpec((gv, t_ctx), lambda b, h: (h, cb + b)),
        pl.BlockSpec((hg, 8, t_ctx), lambda b, h: (h, 0, cb + b)),
        pl.BlockSpec((hg, dv, 1), lambda b, h: (h, 0, 0)),
    ]
    out_specs = [pl.BlockSpec((t_lat, gv), lambda b, h: (b, h)), pl.BlockSpec((t_ctx, gv), lambda b, h: (b, h))]
    out_shape = [jax.ShapeDtypeStruct((n_batch * t_lat, nh * dv), BF16),
                 jax.ShapeDtypeStruct((n_batch * t_ctx, nh * dv), BF16)]
    scratch = [pltpu.VMEM((hg, dv, t_lat), F32), pltpu.VMEM((hg, dv, t_lat), F32),
               pltpu.VMEM((hg, dv, t_ctx), F32), pltpu.VMEM((hg, dv, t_ctx), F32)]
    return pl.pallas_call(
        functools.partial(_mlstm_kernel, n_lat=t_lat // M_CHUNK, n_ctx=t_ctx // M_CHUNK),
        grid=(n_batch, nh // hg),
        in_specs=in_specs, out_specs=out_specs, out_shape=out_shape, scratch_shapes=scratch,
        compiler_params=_params(),
        name="mlstm",
    )(z, qt, vt, auxr, z, qt, vt, auxr, norm_g.reshape(nh, dv, 1))


def _conv_kernel(a_ref, g_ref, ap_ref, gp_ref, an_ref, gn_ref, w_ref, b_ref, lg_ref, lb_ref, o_ref, y_scr, c_scr,
                 sh_scr, *, tt, tiles_per_seq, row_sub, col_w):
    i = pl.program_id(1)
    hl = CONV_HALO
    glu = lambda a, g: a.astype(F32) * _sigmoid(g.astype(F32))
    prev_ok = (i > 0).astype(F32)
    next_ok = (i < tiles_per_seq - 1).astype(F32)
    y_scr[0:hl, :] = glu(ap_ref[...], gp_ref[...]) * prev_ok
    y_scr[hl:hl + tt, :] = glu(a_ref[...], g_ref[...])
    y_scr[hl + tt:hl + tt + hl, :] = glu(an_ref[...], gn_ref[...]) * next_ok
    d = a_ref.shape[1]
    sub = 8
    taps = [[j for j in range(CONV_WIDTH) if (j + hl - CONV_WIDTH // 2) % sub == r] for r in range(sub)]
    span = row_sub + 2 * hl

    def col_body(cb, carry):
        cols = pl.ds(pl.multiple_of(cb * col_w, col_w), col_w)
        wc = w_ref[:, cols]
        for n, r0 in enumerate(range(0, tt, row_sub)):
            acc = jnp.zeros((row_sub, col_w), F32)
            for r in range(sub):
                slot = (n % 2) * sub + r
                sh_scr[slot] = y_scr[r0 + r:r0 + r + span - sub, cols]
                for j in taps[r]:
                    a0 = j + hl - CONV_WIDTH // 2 - r
                    acc = acc + sh_scr[slot, a0:a0 + row_sub, :] * wc[j:j + 1, :]
            c_scr[r0:r0 + row_sub, cols] = acc
        return carry
    lax.fori_loop(0, d // col_w, col_body, 0)
    y = c_scr[...] + b_ref[...]
    mu = jnp.mean(y, axis=-1, keepdims=True)
    yc = y - mu
    var = jnp.mean(yc * yc, axis=-1, keepdims=True)
    o_ref[...] = _silu(yc * lax.rsqrt(var + EPS) * lg_ref[...] + lb_ref[...]).astype(o_ref.dtype)


def _conv_branch(z, row0, n_seq, t_seq, tt, conv_w, conv_b, ln_g, ln_b):
    d = conv_w.shape[1]
    hl = CONV_HALO
    tps = t_seq // tt
    rb0, hb0, hps = row0 // tt, row0 // hl, t_seq // hl
    ca, cg = Z_CA // d, Z_CG // d
    cur = lambda col: pl.BlockSpec((tt, d), lambda s, i: (rb0 + s * tps + i, col))
    prv = lambda col: pl.BlockSpec(
        (hl, d), lambda s, i: (hb0 + s * hps + jnp.maximum(i * (tt // hl) - 1, 0), col))
    nxt = lambda col: pl.BlockSpec(
        (hl, d), lambda s, i: (hb0 + s * hps + jnp.minimum((i + 1) * (tt // hl), hps - 1), col))
    vec = pl.BlockSpec((1, d), lambda s, i: (0, 0))
    wpad = jnp.zeros((32, d), F32).at[:CONV_WIDTH].set(conv_w)
    row_sub, col_w = 128, LANES
    return pl.pallas_call(
        functools.partial(_conv_kernel, tt=tt, tiles_per_seq=tps, row_sub=row_sub, col_w=col_w),
        grid=(n_seq, tps),
        in_specs=[cur(ca), cur(cg), prv(ca), prv(cg), nxt(ca), nxt(cg),
                  pl.BlockSpec((32, d), lambda s, i: (0, 0)), vec, vec, vec],
        out_specs=pl.BlockSpec((tt, d), lambda s, i: (s * tps + i, 0)),
        out_shape=jax.ShapeDtypeStruct((n_seq * t_seq, d), BF16),
        scratch_shapes=[pltpu.VMEM((tt + 2 * hl, d), F32), pltpu.VMEM((tt, d), F32),
                        pltpu.VMEM((16, row_sub + 2 * hl - 8, col_w), F32)],
        compiler_params=_params(),
        name="conv_branch",
    )(z, z, z, z, z, z, wpad, conv_b.reshape(1, d), ln_g.reshape(1, d), ln_b.reshape(1, d))


def _head_block_diag():
    gw = A_GROUP * A_HEAD_DIM
    r = lax.broadcasted_iota(jnp.int32, (gw, gw), 0) // A_HEAD_DIM
    c = lax.broadcasted_iota(jnp.int32, (gw, gw), 1) // A_HEAD_DIM
    return (r == c).astype(BF16)


def _headnorm_rope(x, bd, gain, cos, sin, out_scale=1.0):
    ss = jnp.dot((x * x).astype(BF16), bd, preferred_element_type=F32)
    xn = x * lax.rsqrt(ss * (1.0 / A_HEAD_DIM) + EPS) * gain
    w = x.shape[1]
    lane = lax.broadcasted_iota(jnp.int32, x.shape, 1)
    first = (lane % (2 * ROPE_FREQS)) < ROPE_FREQS
    swapped = jnp.where(first, pltpu.roll(xn, w - ROPE_FREQS, 1), pltpu.roll(xn, ROPE_FREQS, 1))
    return (xn * cos + swapped * sin) * out_scale


def _attn_kernel(sink_ref, q_ref, k_ref, v_ref, kc_ref, vc_ref, o_ref, *, local, t_seq):
    nq = pl.program_id(1)
    blk, dh, grp = A_BLOCK, A_HEAD_DIM, A_GROUP
    nt = (((1,), (1,)), ((), ()))
    if local:
        win = 3 * blk
        start = pl.multiple_of(jnp.clip(nq * blk - blk, 0, t_seq - win), blk)
        qpos = nq * blk + lax.broadcasted_iota(jnp.int32, (grp * blk, win), 0) % blk
        kpos = start + lax.broadcasted_iota(jnp.int32, (grp * blk, win), 1)
        near = jnp.abs(kpos - qpos) <= WINDOW
    rowgrp = lax.broadcasted_iota(jnp.int32, (grp * blk, 1), 0) // blk

    def with_ones(v):
        col = lax.broadcasted_iota(jnp.int32, (v.shape[0], dh), 1)
        return jnp.concatenate([v, (col == 0).astype(v.dtype)], axis=1)

    outs = []
    for h in range(A_KV_HEADS):
        cols = slice(h * dh, (h + 1) * dh)
        qs = jnp.concatenate([q_ref[:, (h * grp + g) * dh:(h * grp + g + 1) * dh] for g in range(grp)], axis=0)
        sink = jnp.zeros((grp * blk, 1), F32)
        for g in range(grp):
            sink = jnp.where(rowgrp == g, sink_ref[h * grp + g], sink)
        s_ctx = lax.dot_general(qs, kc_ref[:, cols], nt, preferred_element_type=F32)
        mx = jnp.maximum(jnp.max(s_ctx, axis=-1, keepdims=True), sink)
        if local:
            s_loc = lax.dot_general(qs, k_ref[pl.ds(start, win), cols], nt, preferred_element_type=F32)
            s_loc = jnp.where(near, s_loc, -jnp.inf)
            mx = jnp.maximum(mx, jnp.max(s_loc, axis=-1, keepdims=True))
        o = jnp.dot(jnp.exp(s_ctx - mx).astype(BF16), with_ones(vc_ref[:, cols]), preferred_element_type=F32)
        if local:
            o = o + jnp.dot(jnp.exp(s_loc - mx).astype(BF16), with_ones(v_ref[pl.ds(start, win), cols]),
                            preferred_element_type=F32)
        o = o[:, 0:dh] / (o[:, dh:dh + 1] + jnp.exp(sink - mx))
        outs.extend(o[g * blk:(g + 1) * blk, :] for g in range(grp))
    o_ref[...] = jnp.concatenate(outs, axis=1).astype(o_ref.dtype)


def _attention(sink, z, q_row0, n_batch, t_seq, t_ctx, ctx_row0, local):
    qw, kw = A_HEADS * A_HEAD_DIM, A_KV_HEADS * A_HEAD_DIM
    nqb = t_seq // A_BLOCK
    qb0, sb0, cb0 = q_row0 // A_BLOCK, q_row0 // t_seq, ctx_row0 // t_ctx
    qcol, kcol, vcol = Z_AQ // qw, Z_AK // kw, Z_AV // kw
    return pl.pallas_call(
        functools.partial(_attn_kernel, local=local, t_seq=t_seq),
        grid_spec=pltpu.PrefetchScalarGridSpec(
            num_scalar_prefetch=1,
            grid=(n_batch, nqb),
            in_specs=[pl.BlockSpec((A_BLOCK, qw), lambda b, n, s: (qb0 + b * nqb + n, qcol)),
                      pl.BlockSpec((t_seq, kw), lambda b, n, s: (sb0 + b, kcol)),
                      pl.BlockSpec((t_seq, kw), lambda b, n, s: (sb0 + b, vcol)),
                      pl.BlockSpec((t_ctx, kw), lambda b, n, s: (cb0 + b, kcol)),
                      pl.BlockSpec((t_ctx, kw), lambda b, n, s: (cb0 + b, vcol))],
            out_specs=pl.BlockSpec((A_BLOCK, qw), lambda b, n, s: (b * nqb + n, 0))),
        out_shape=jax.ShapeDtypeStruct((n_batch * t_seq, qw), BF16),
        compiler_params=_params(),
        name="attention_local" if local else "attention_ctx",
    )(sink, z, z, z, z, z)


def _merge_kernel(*refs, n_lat_tiles, has_ctx):
    n_br = 6 if has_ctx else 3
    o_ref, gl_ref, x_ref, mod_ref, wa_ref, wb_ref, wc_ref, wo_ref, xo_ref, m_scr = refs[n_br:]
    if has_ctx:
        is_ctx = pl.program_id(0) >= n_lat_tiles
        ya, yb, yc = (jnp.where(is_ctx, refs[2 * n + 1][...], refs[2 * n][...]) for n in range(3))
    else:
        ya, yb, yc = (refs[n][...] for n in range(3))
    d = x_ref.shape[1]
    ya = (ya.astype(F32) * _sigmoid(o_ref[...].astype(F32))).astype(BF16)
    for c0, cw in _chunks(d, 256):
        cs = slice(c0, c0 + cw)
        acc = _sigmoid(gl_ref[:, c0:c0 + cw].astype(F32)) * jnp.dot(ya, wa_ref[:, cs], preferred_element_type=F32)
        acc = acc + _sigmoid(gl_ref[:, d + c0:d + c0 + cw].astype(F32)) * jnp.dot(
            yb, wb_ref[:, cs], preferred_element_type=F32)
        acc = acc + _sigmoid(gl_ref[:, 2 * d + c0:2 * d + c0 + cw].astype(F32)) * jnp.dot(
            yc, wc_ref[:, cs], preferred_element_type=F32)
        m_scr[:, cs] = acc.astype(BF16)
    mm = m_scr[...]
    for c0, cw in _chunks(d, 256):
        cs = slice(c0, c0 + cw)
        xo_ref[:, cs] = x_ref[:, cs] + mod_ref[0, 2:3, cs] * jnp.dot(mm, wo_ref[:, cs], preferred_element_type=F32)


def _merge(branches, z, x, modtab, wa, wb, wc, wo, tm, n_rows, n_lat_rows, t_lat):
    d = x.shape[1]
    n_batch = modtab.shape[0] - 1
    n_lat_tiles, tpb = n_lat_rows // tm, t_lat // tm
    has_ctx = branches[0][1] is not None
    row = pl.BlockSpec((tm, d), lambda i: (i, 0))
    lat = pl.BlockSpec((tm, d), lambda i: (jnp.minimum(i, n_lat_tiles - 1), 0))
    ctx = pl.BlockSpec((tm, d), lambda i: (jnp.maximum(i - n_lat_tiles, 0), 0))
    wsp = pl.BlockSpec((d, d), lambda i: (0, 0))
    br_args = [a for pair in branches for a in (pair if has_ctx else pair[:1])]
    br_specs = [lat, ctx] * 3 if has_ctx else [lat] * 3
    return pl.pallas_call(
        functools.partial(_merge_kernel, n_lat_tiles=n_lat_tiles, has_ctx=has_ctx),
        grid=(n_rows // tm,),
        in_specs=br_specs + [pl.BlockSpec((tm, d), lambda i: (i, Z_MO // d)),
                             pl.BlockSpec((tm, 3 * d), lambda i: (i, Z_GATE // (3 * d))), row,
                             pl.BlockSpec((1, 6, d), lambda i: (_mod_index(i, n_lat_tiles, tpb, n_batch), 0, 0)),
                             wsp, wsp, wsp, wsp],
        out_specs=row,
        out_shape=jax.ShapeDtypeStruct(x.shape, F32),
        scratch_shapes=[pltpu.VMEM((tm, d), BF16)],
        input_output_aliases={len(br_args) + 2: 0},
        compiler_params=_params(),
        name="merge_out_proj",
    )(*br_args, z, z, x, modtab, wa, wb, wc, wo)


def _ffn_kernel(x_ref, mod_ref, g_ref, w1_ref, w3_ref, w2_ref, xo_ref, u_scr, *, f_chunks):
    h = _norm_mod(x_ref[...], g_ref[...], mod_ref[0, 3:4, :], mod_ref[0, 4:5, :]).astype(BF16)
    for c0, cw in f_chunks:
        a = jnp.dot(h, w1_ref[:, c0:c0 + cw], preferred_element_type=F32)
        b = jnp.dot(h, w3_ref[:, c0:c0 + cw], preferred_element_type=F32)
        u_scr[:, c0:c0 + cw] = (_silu(a) * b).astype(BF16)
    d = x_ref.shape[1]
    for c0, cw in _chunks(d, 256):
        o = jnp.dot(u_scr[...], w2_ref[:, c0:c0 + cw], preferred_element_type=F32)
        xo_ref[:, c0:c0 + cw] = x_ref[:, c0:c0 + cw] + mod_ref[0, 5:6, c0:c0 + cw] * o


def _dense_ffn(x, modtab, g2, w1, w3, w2, tm, n_rows, n_lat_rows, t_lat):
    d, f = w1.shape
    n_batch = modtab.shape[0] - 1
    n_lat_tiles, tpb = n_lat_rows // tm, t_lat // tm
    row = pl.BlockSpec((tm, d), lambda i: (i, 0))
    return pl.pallas_call(
        functools.partial(_ffn_kernel, f_chunks=_chunks(f, 256)),
        grid=(n_rows // tm,),
        in_specs=[row, pl.BlockSpec((1, 6, d), lambda i: (_mod_index(i, n_lat_tiles, tpb, n_batch), 0, 0)),
                  pl.BlockSpec((1, d), lambda i: (0, 0)),
                  pl.BlockSpec((d, f), lambda i: (0, 0)), pl.BlockSpec((d, f), lambda i: (0, 0)),
                  pl.BlockSpec((f, d), lambda i: (0, 0))],
        out_specs=row,
        out_shape=jax.ShapeDtypeStruct(x.shape, F32),
        scratch_shapes=[pltpu.VMEM((tm, f), BF16)],
        input_output_aliases={0: 0},
        compiler_params=_params(),
        name="dense_ffn",
    )(x, modtab, g2, w1, w3, w2)


R_E1, R_E2, R_R1, R_R2, R_P1, R_P2 = range(6)
HI16 = 0xFFFF0000


def _pack_bf16_pair(lo, hi):
    lo_bits = lax.bitcast_convert_type(lo.astype(BF16).astype(F32), jnp.uint32)
    hi_bits = lax.bitcast_convert_type(hi.astype(BF16).astype(F32), jnp.uint32)
    return (hi_bits & jnp.uint32(HI16)) | (lo_bits >> 16)


def _unpack_bf16_pair(u):
    lo = lax.bitcast_convert_type(u << 16, F32).astype(BF16)
    hi = lax.bitcast_convert_type(u & jnp.uint32(HI16), F32).astype(BF16)
    return lo, hi


def _router_kernel(x_ref, mod_ref, g_ref, rw_ref, hp_ref, r_ref, cnt_ref, carry_scr):
    tm, d = x_ref.shape

    @pl.when(pl.program_id(0) == 0)
    def _():
        carry_scr[...] = jnp.zeros_like(carry_scr)

    h = _norm_mod(x_ref[...], g_ref[...], mod_ref[0, 3:4, :], mod_ref[0, 4:5, :])
    hp_ref[...] = _pack_bf16_pair(h[:, :d // 2], h[:, d // 2:])
    logits = jnp.dot(h, rw_ref[...], preferred_element_type=F32, precision=lax.Precision.HIGHEST)
    lane = lax.broadcasted_iota(jnp.int32, (tm, LANES), 1).astype(F32)
    logits = jnp.where(lane < N_EXPERTS, logits, -jnp.inf)
    v1 = jnp.max(logits, axis=-1, keepdims=True)
    e1 = jnp.min(jnp.where(logits == v1, lane, float(LANES)), axis=-1, keepdims=True)
    rest = jnp.where(lane == e1, -jnp.inf, logits)
    v2 = jnp.max(rest, axis=-1, keepdims=True)
    e2 = jnp.min(jnp.where(rest == v2, lane, float(LANES)), axis=-1, keepdims=True)
    t = jnp.exp(v2 - v1)
    p1 = 1.0 / (1.0 + t)
    p2 = t / (1.0 + t)
    hot1 = lane == e1
    hot2 = lane == e2
    hot = jnp.where(hot1 | hot2, 1.0, 0.0)
    rr = lax.broadcasted_iota(jnp.int32, (tm, tm), 0)
    cc = lax.broadcasted_iota(jnp.int32, (tm, tm), 1)
    before = jnp.where(cc < rr, 1.0, 0.0).astype(BF16)
    prefix = jnp.dot(before, hot.astype(BF16), preferred_element_type=F32) + carry_scr[...]
    r1 = jnp.sum(jnp.where(hot1, prefix, 0.0), axis=-1, keepdims=True)
    r2 = jnp.sum(jnp.where(hot2, prefix, 0.0), axis=-1, keepdims=True)
    carry_scr[...] += jnp.sum(hot, axis=0, keepdims=True)
    cnt_ref[...] = carry_scr[...]
    out = jnp.zeros((tm, LANES), F32)
    for idx, val in ((R_E1, e1), (R_E2, e2), (R_R1, r1), (R_R2, r2), (R_P1, p1), (R_P2, p2)):
        out = jnp.where(lane == float(idx), val, out)
    r_ref[...] = out


def _router(x, modtab, g2, rw, tm, n_rows, t_lat):
    d = x.shape[1]
    tpb = t_lat // tm
    return pl.pallas_call(
        _router_kernel,
        grid=(n_rows // tm,),
        in_specs=[pl.BlockSpec((tm, d), lambda i: (i, 0)), pl.BlockSpec((1, 6, d), lambda i: (i // tpb, 0, 0)),
                  pl.BlockSpec((1, d), lambda i: (0, 0)), pl.BlockSpec((d, LANES), lambda i: (0, 0))],
        out_specs=[pl.BlockSpec((tm, d // 2), lambda i: (i, 0)), pl.BlockSpec((tm, LANES), lambda i: (i, 0)),
                   pl.BlockSpec((1, LANES), lambda i: (0, 0))],
        out_shape=[jax.ShapeDtypeStruct((n_rows, d // 2), jnp.uint32), jax.ShapeDtypeStruct((n_rows, LANES), F32),
                   jax.ShapeDtypeStruct((1, LANES), F32)],
        scratch_shapes=[pltpu.VMEM((1, LANES), F32)],
        compiler_params=_params(),
        name="moe_router",
    )(x, modtab, g2, rw)


def _row_copies(n, make, unroll=8):
    def start(g, c):
        for u in range(unroll):
            for cp in make(g * unroll + u):
                cp.start(priority=u % 2)
        return c
    lax.fori_loop(0, n // unroll, start, 0)

    def wait(g, c):
        for u in range(unroll):
            for cp in make(g * unroll + u):
                cp.wait()
        return c
    lax.fori_loop(0, n // unroll, wait, 0)


def _dispatch_kernel(d1_ref, d2_ref, hp_ref, xs_in, xs_hbm, sem):
    del xs_in

    def make(r):
        src = hp_ref.at[pl.ds(r, 1), :]
        return (pltpu.make_async_copy(src, xs_hbm.at[pl.ds(d1_ref[0, 0, r], 1), :], sem),
                pltpu.make_async_copy(src, xs_hbm.at[pl.ds(d2_ref[0, 0, r], 1), :], sem))
    _row_copies(hp_ref.shape[0], make)


def _dispatch(d1, d2, hp, n_sorted, tm):
    n_rows, w = hp.shape
    idx = pl.BlockSpec((1, 1, tm), lambda i: (i, 0, 0), memory_space=pltpu.SMEM)
    return pl.pallas_call(
        _dispatch_kernel,
        grid=(n_rows // tm,),
        in_specs=[idx, idx, pl.BlockSpec((tm, w), lambda i: (i, 0)), pl.BlockSpec(memory_space=pl.ANY)],
        out_specs=pl.BlockSpec(memory_space=pl.ANY),
        out_shape=jax.ShapeDtypeStruct((n_sorted, w), jnp.uint32),
        scratch_shapes=[pltpu.SemaphoreType.DMA(())],
        input_output_aliases={3: 0},
        compiler_params=_params(),
        name="moe_dispatch",
    )(d1.reshape(-1, 1, tm), d2.reshape(-1, 1, tm), hp, jnp.zeros((n_sorted, w), jnp.uint32))


def _expert_kernel(te_ref, nv_ref, xs_ref, w1_ref, w3_ref, w2_ref, y_ref, xb_scr, acc_scr, u_scr, *, n_f):
    i, f = pl.program_id(0), pl.program_id(1)
    valid = i < nv_ref[0]
    half = xs_ref.shape[1]

    @pl.when(valid & (f == 0))
    def _():
        lo, hi = _unpack_bf16_pair(xs_ref[...])
        xb_scr[:, :half] = lo
        xb_scr[:, half:] = hi

    @pl.when(valid)
    def _():
        xb = xb_scr[...]

        @pl.when(f == 0)
        def _():
            acc_scr[...] = jnp.zeros_like(acc_scr)

        for c0, cw in _chunks(w1_ref.shape[2], 256):
            a = jnp.dot(xb, w1_ref[0, :, c0:c0 + cw].astype(BF16), preferred_element_type=F32)
            b = jnp.dot(xb, w3_ref[0, :, c0:c0 + cw].astype(BF16), preferred_element_type=F32)
            u_scr[:, c0:c0 + cw] = (_silu(a) * b).astype(BF16)
        acc_scr[...] += jnp.dot(u_scr[...], w2_ref[0].astype(BF16), preferred_element_type=F32)

        @pl.when(f == n_f - 1)
        def _():
            y_ref[...] = acc_scr[...]

    @pl.when(jnp.logical_not(valid) & (f == n_f - 1))
    def _():
        y_ref[...] = jnp.zeros_like(y_ref)


def _experts(tile_expert, n_valid, xs, w1, w3, w2, tme, tf):
    ne, d, f = w1.shape
    n_tiles = xs.shape[0] // tme
    n_f = f // tf
    return pl.pallas_call(
        functools.partial(_expert_kernel, n_f=n_f),
        grid_spec=pltpu.PrefetchScalarGridSpec(
            num_scalar_prefetch=2,
            grid=(n_tiles, n_f),
            in_specs=[pl.BlockSpec((tme, d // 2), lambda i, j, te, nv: (i, 0)),
                      pl.BlockSpec((1, d, tf), lambda i, j, te, nv: (te[i], 0, j)),
                      pl.BlockSpec((1, d, tf), lambda i, j, te, nv: (te[i], 0, j)),
                      pl.BlockSpec((1, tf, d), lambda i, j, te, nv: (te[i], j, 0))],
            out_specs=pl.BlockSpec((tme, d), lambda i, j, te, nv: (i, 0)),
            scratch_shapes=[pltpu.VMEM((tme, d), BF16), pltpu.VMEM((tme, d), F32), pltpu.VMEM((tme, tf), BF16)]),
        out_shape=jax.ShapeDtypeStruct((xs.shape[0], d), F32),
        compiler_params=_params(),
        name="moe_experts",
    )(tile_expert, n_valid, xs, w1, w3, w2)


def _combine_kernel(d1_ref, d2_ref, x_ref, r_ref, mod_ref, y_hbm, xo_ref, y1_scr, y2_scr, sem):
    def make(r):
        return (pltpu.make_async_copy(y_hbm.at[pl.ds(d1_ref[0, 0, r], 1), :], y1_scr.at[pl.ds(r, 1), :], sem),
                pltpu.make_async_copy(y_hbm.at[pl.ds(d2_ref[0, 0, r], 1), :], y2_scr.at[pl.ds(r, 1), :], sem))
    _row_copies(x_ref.shape[0], make)
    r = r_ref[...]
    p1 = r[:, R_P1:R_P1 + 1]
    p2 = r[:, R_P2:R_P2 + 1]
    xo_ref[...] = x_ref[...] + mod_ref[0, 5:6, :] * (p1 * y1_scr[...] + p2 * y2_scr[...])


def _combine(d1, d2, x, y, r, modtab, tm, n_rows, t_lat):
    d = x.shape[1]
    tpb = t_lat // tm
    idx = pl.BlockSpec((1, 1, tm), lambda i: (i, 0, 0), memory_space=pltpu.SMEM)
    return pl.pallas_call(
        _combine_kernel,
        grid=(n_rows // tm,),
        in_specs=[idx, idx, pl.BlockSpec((tm, d), lambda i: (i, 0)), pl.BlockSpec((tm, LANES), lambda i: (i, 0)),
                  pl.BlockSpec((1, 6, d), lambda i: (i // tpb, 0, 0)), pl.BlockSpec(memory_space=pl.ANY)],
        out_specs=pl.BlockSpec((tm, d), lambda i: (i, 0)),
        out_shape=jax.ShapeDtypeStruct((n_rows, d), F32),
        scratch_shapes=[pltpu.VMEM((tm, d), F32), pltpu.VMEM((tm, d), F32), pltpu.SemaphoreType.DMA(())],
        compiler_params=_params(),
        name="moe_combine",
    )(d1.reshape(-1, 1, tm), d2.reshape(-1, 1, tm), x, r, modtab, y)


def _moe(x, modtab, g2, router_w, w1, w3, w2, n_rows, t_lat, tme=1024, tf=512, tm=512):
    d = x.shape[1]
    rw = jnp.zeros((d, LANES), F32).at[:, :N_EXPERTS].set(router_w)
    hp, r, cnt = _router(x, modtab, g2, rw, tm, n_rows, t_lat)
    e1, e2 = r[:, R_E1].astype(jnp.int32), r[:, R_E2].astype(jnp.int32)
    r1, r2 = r[:, R_R1].astype(jnp.int32), r[:, R_R2].astype(jnp.int32)
    counts = cnt[0, :N_EXPERTS].astype(jnp.int32)
    padded = (counts + tme - 1) // tme * tme
    ends = jnp.cumsum(padded)
    offs = ends - padded
    n_sorted = 2 * n_rows + N_EXPERTS * tme
    d1, d2 = offs[e1] + r1, offs[e2] + r2
    tile_start = jnp.arange(n_sorted // tme, dtype=jnp.int32) * tme
    n_valid = (ends[-1] // tme).astype(jnp.int32).reshape(1)
    tile_expert = jnp.minimum(jnp.sum(tile_start[:, None] >= ends[None, :], axis=1), N_EXPERTS - 1).astype(jnp.int32)
    tile_expert = jnp.where(tile_start < ends[-1], tile_expert, tile_expert[jnp.maximum(n_valid[0] - 1, 0)])
    xs = _dispatch(d1, d2, hp, n_sorted, tm)
    y = _experts(tile_expert, n_valid, xs, w1, w3, w2, tme, tf)
    return _combine(d1, d2, x, y, r, modtab, tm, n_rows, t_lat)


def _rope_tables(t_lat, pad_rows):
    rows = t_lat // GRID_W
    pos_r = jnp.repeat(jnp.arange(rows, dtype=F32), GRID_W)
    pos_c = jnp.tile(jnp.arange(GRID_W, dtype=F32), rows)
    inv = 1.0 / (ROPE_BASE ** (jnp.arange(ROPE_FREQS, dtype=F32) * 2.0 / (A_HEAD_DIM // 2)))
    ar, ac = pos_r[:, None] * inv, pos_c[:, None] * inv
    cos = jnp.concatenate([jnp.cos(ar), jnp.cos(ar), jnp.cos(ac), jnp.cos(ac)], axis=1)
    sin = jnp.concatenate([-jnp.sin(ar), jnp.sin(ar), -jnp.sin(ac), jnp.sin(ac)], axis=1)
    cos = jnp.concatenate([jnp.tile(cos, (1, A_KV_HEADS)), jnp.ones((pad_rows, A_KV_HEADS * A_HEAD_DIM), F32)])
    sin = jnp.concatenate([jnp.tile(sin, (1, A_KV_HEADS)), jnp.zeros((pad_rows, A_KV_HEADS * A_HEAD_DIM), F32)])
    return cos, sin


def _split_w_in(w, b):
    o = [0, 512, 1024, 2048, 3072, 3088, 5136, 6160, 6416, 6672, 9744]
    order = [(o[0], o[4]), (o[5], o[6]), (o[6], o[7]), (o[9], o[10]), (o[7], o[8]), (o[8], o[9])]
    w_z = jnp.concatenate([w[:, a:e] for a, e in order], axis=1).astype(BF16)
    b_z = jnp.concatenate([b[a:e] for a, e in order]).reshape(1, NZ)
    ng = 4 * M_HEADS
    w_g = jnp.zeros((w.shape[0], LANES), BF16).at[:, :ng].set(w[:, o[4]:o[5]].astype(BF16))
    b_g = jnp.zeros((1, LANES), F32).at[0, :ng].set(b[o[4]:o[5]])
    return w_z, b_z, w_g, b_g


def kernel(x, c, ctx, c_ctx, w_mod, b_mod, norm1_g, norm2_g, w_in, b_in, m_norm_g, conv_w, conv_b, conv_ln_g,
           conv_ln_b, q_norm_g, k_norm_g, attn_sink, w_branch_a, w_branch_b, w_branch_c, w_out, ffn_w1, ffn_w3,
           ffn_w2, router_w, moe_w1, moe_w3, moe_w2):
    n_batch, t_lat, d = x.shape
    t_ctx = ctx.shape[1]
    depth = w_mod.shape[0]
    n_lat, n_ctx = n_batch * t_lat, n_batch * t_ctx
    m = n_lat + n_ctx
    tm = min(1024, n_ctx)
    assert t_lat % tm == 0 and n_ctx % tm == 0 and t_lat >= 3 * A_BLOCK

    xs = jnp.concatenate([x.reshape(n_lat, d), ctx.reshape(n_ctx, d)], axis=0)
    cvec = jnp.zeros((16, d), F32).at[:n_batch].set(c).at[n_batch].set(c_ctx)
    mod = _modulation(cvec, w_mod, b_mod)[:, :n_batch + 1].reshape(depth, n_batch + 1, 6, d)
    tmm = min(512, tm)
    cos_tab, sin_tab = _rope_tables(t_lat, tmm)

    for l in range(depth):
        last = l == depth - 1
        modtab = mod[l]
        w_z, b_z, w_g, b_g = _split_w_in(w_in[l], b_in[l])
        z, gates, qt, vt = _in_projection(xs, modtab, norm1_g[l].reshape(1, d), w_z, b_z, w_g, b_g, cos_tab, sin_tab,
                                          q_norm_g[l], k_norm_g[l], tmm, n_lat, t_lat)

        aux = _gate_prep(gates, tm).reshape(4, M_HEADS, m)
        auxr = jnp.pad(aux.transpose(1, 0, 2), ((0, 0), (0, 4), (0, 0)))
        ya_lat, ya_ctx = _mlstm(z, qt, vt, auxr, m_norm_g[l], n_batch, t_lat, t_ctx)

        conv_args = (conv_w[l], conv_b[l], conv_ln_g[l], conv_ln_b[l])
        yb_lat = _conv_branch(z, 0, n_batch, t_lat, min(512, t_lat), *conv_args)

        yc_lat = _attention(attn_sink[l], z, 0, n_batch, t_lat, t_ctx, n_lat, True)

        wa, wb, wc, wo = (w.astype(BF16) for w in (w_branch_a[l], w_branch_b[l], w_branch_c[l], w_out[l]))
        if not last:
            yb_ctx = _conv_branch(z, n_lat, n_batch, t_ctx, t_ctx, *conv_args)
            yc_ctx = _attention(attn_sink[l], z, n_lat, n_batch, t_ctx, t_ctx, n_lat, False)
            branches = ((ya_lat, ya_ctx), (yb_lat, yb_ctx), (yc_lat, yc_ctx))
            n_rows = m
        else:
            branches = ((ya_lat, None), (yb_lat, None), (yc_lat, None))
            n_rows = n_lat
        xs = _merge(branches, z, xs, modtab, wa, wb, wc, wo, tmm, n_rows, n_lat, t_lat)

        g2 = norm2_g[l].reshape(1, d)
        if l % 2 == 0:
            i = l // 2
            xs = _dense_ffn(xs, modtab, g2, ffn_w1[i].astype(BF16), ffn_w3[i].astype(BF16), ffn_w2[i].astype(BF16),
                            tmm, n_rows, n_lat, t_lat)
        else:
            i = l // 2
            assert last, "expert mixer is implemented for latent rows only"
            xs = _moe(xs, modtab, g2, router_w[i], moe_w1[i], moe_w3[i], moe_w2[i], n_lat, t_lat)
    return xs[:n_lat].reshape(n_batch, t_lat, d)
```

```python
import functools

import jax
import jax.numpy as jnp
from jax import lax
from jax.experimental import pallas as pl
from jax.experimental.pallas import tpu as pltpu

F32 = jnp.float32
BF16 = jnp.bfloat16

EPS = 1e-6
GRID_W = 64
M_HEADS = 4
M_QK_DIM = 128
M_V_DIM = 256
M_CHUNK = 128
CONV_WIDTH = 31
CONV_HALO = 16
A_HEADS = 16
A_KV_HEADS = 4
A_GROUP = A_HEADS // A_KV_HEADS
A_HEAD_DIM = 64
A_BLOCK = 128
WINDOW = 128
ROPE_BASE = 10000.0
ROPE_FREQS = A_HEAD_DIM // 4
N_EXPERTS = 8
LANES = 128
VMEM_LIMIT = 56 * 1024 * 1024

Z_MQ, Z_MK, Z_MV, Z_MO = 0, 512, 1024, 2048
Z_CA, Z_CG, Z_AQ, Z_GATE, Z_AK, Z_AV = 3072, 4096, 5120, 6144, 9216, 9472
NZ = 9728


def _params(**kw):
    return pltpu.CompilerParams(vmem_limit_bytes=VMEM_LIMIT, **kw)


def _sigmoid(x):
    return 1.0 / (1.0 + jnp.exp(-x))


def _silu(x):
    return x * _sigmoid(x)


def _log_sigmoid(x):
    return jnp.minimum(x, 0.0) - jnp.log(1.0 + jnp.exp(-jnp.abs(x)))


def _norm_mod(x, g, shift, scale):
    ms = jnp.mean(x * x, axis=-1, keepdims=True)
    return (x * lax.rsqrt(ms + EPS) * g) * (1.0 + scale) + shift


def _chunks(n, w):
    out, c = [], 0
    while c < n:
        out.append((c, min(w, n - c)))
        c += w
    return out


def _mod_kernel(c_ref, w_ref, b_ref, o_ref):
    s = _silu(c_ref[...]).astype(BF16)
    o_ref[0] = jnp.dot(s, w_ref[0].astype(BF16), preferred_element_type=F32) + b_ref[0]


def _modulation(cvec, w_mod, b_mod):
    depth, d, n = w_mod.shape
    tn = 1536
    return pl.pallas_call(
        _mod_kernel,
        grid=(depth, n // tn),
        in_specs=[pl.BlockSpec((cvec.shape[0], d), lambda l, j: (0, 0)),
                  pl.BlockSpec((1, d, tn), lambda l, j: (l, 0, j)),
                  pl.BlockSpec((1, 1, tn), lambda l, j: (l, 0, j))],
        out_specs=pl.BlockSpec((1, cvec.shape[0], tn), lambda l, j: (l, 0, j)),
        out_shape=jax.ShapeDtypeStruct((depth, cvec.shape[0], n), F32),
        compiler_params=_params(),
        name="adaln_mod",
    )(cvec, w_mod, b_mod.reshape(depth, 1, n))


def _inproj_kernel(x0_ref, xn_ref, mod0_ref, modn_ref, g_ref, w_ref, b_ref, wg_ref, bg_ref, cos_ref, sin_ref, qg_ref,
                   kg_ref, z_ref, gates_ref, qt_ref, vt_ref, h_scr, *, tn, n_tiles):
    gw = A_GROUP * A_HEAD_DIM
    slot = pl.program_id(0) % 2

    def normalise(x_ref, mod_ref, dst_slot):
        h_scr[dst_slot] = _norm_mod(x_ref[...], g_ref[...], mod_ref[0, 0:1, :], mod_ref[0, 1:2, :]).astype(BF16)

    @pl.when((pl.program_id(0) == 0) & (pl.program_id(1) == 0))
    def _():
        normalise(x0_ref, mod0_ref, 0)

    def project(tile):
        h = h_scr[slot]
        if tile == 0:
            gates_ref[...] = jnp.dot(h, wg_ref[...], preferred_element_type=F32) + bg_ref[...]
        if tile == n_tiles - 1:
            normalise(xn_ref, modn_ref, 1 - slot)
        def epilogue(acc, c0, cw):
            col = tile * tn + c0
            if Z_AQ <= col < Z_GATE:
                acc = _headnorm_rope(acc, _head_block_diag(), qg_ref[...], cos_ref[...], sin_ref[...],
                                     A_HEAD_DIM ** -0.5)
            elif Z_AK <= col < Z_AV:
                acc = _headnorm_rope(acc, _head_block_diag(), kg_ref[...], cos_ref[...], sin_ref[...])
            z_ref[:, c0:c0 + cw] = acc.astype(BF16)
            if Z_MQ <= col < Z_MK:
                qt_ref[col - Z_MQ:col - Z_MQ + cw, :] = acc.T.astype(BF16)
            if Z_MV <= col < Z_MO:
                vt_ref[col - Z_MV:col - Z_MV + cw, :] = acc.T.astype(BF16)

        pending = None
        for c0, cw in _chunks(tn, gw):
            acc = jnp.dot(h, w_ref[:, c0:c0 + cw], preferred_element_type=F32) + b_ref[:, c0:c0 + cw]
            if pending is not None:
                epilogue(*pending)
            pending = (acc, c0, cw)
        epilogue(*pending)

    for tile in range(n_tiles):
        @pl.when(pl.program_id(1) == tile)
        def _(tile=tile):
            project(tile)


def _mod_index(i, n_lat_tiles, tiles_per_batch, n_batch):
    return jnp.where(i < n_lat_tiles, i // tiles_per_batch, n_batch)


def _in_projection(x, modtab, g1, w_z, b_z, w_g, b_g, cos_tab, sin_tab, q_g, k_g, tm, n_lat_rows, t_lat):
    m, d = x.shape
    n_batch = modtab.shape[0] - 1
    n_tiles = 2
    tn = NZ // n_tiles
    gw = A_GROUP * A_HEAD_DIM
    assert tn % gw == 0 and Z_MO <= tn, "chunks must align with head groups; tile 0 holds the mLSTM q and v"
    n_lat_tiles, tpb = n_lat_rows // tm, t_lat // tm
    nxt = lambda i: jnp.minimum(i + 1, m // tm - 1)
    tab = pl.BlockSpec((tm, gw), lambda i, j: (jnp.where(i < n_lat_tiles, i % tpb, tpb), 0))
    vec = pl.BlockSpec((1, gw), lambda i, j: (0, 0))
    return pl.pallas_call(
        functools.partial(_inproj_kernel, tn=tn, n_tiles=n_tiles),
        grid=(m // tm, n_tiles),
        in_specs=[pl.BlockSpec((tm, d), lambda i, j: (0, 0)),
                  pl.BlockSpec((tm, d), lambda i, j: (nxt(i), 0)),
                  pl.BlockSpec((1, 6, d), lambda i, j: (_mod_index(0, n_lat_tiles, tpb, n_batch), 0, 0)),
                  pl.BlockSpec((1, 6, d), lambda i, j: (_mod_index(nxt(i), n_lat_tiles, tpb, n_batch), 0, 0)),
                  pl.BlockSpec((1, d), lambda i, j: (0, 0)),
                  pl.BlockSpec((d, tn), lambda i, j: (0, j)),
                  pl.BlockSpec((1, tn), lambda i, j: (0, j)),
                  pl.BlockSpec((d, LANES), lambda i, j: (0, 0)),
                  pl.BlockSpec((1, LANES), lambda i, j: (0, 0)),
                  tab, tab, vec, vec],
        out_specs=[pl.BlockSpec((tm, tn), lambda i, j: (i, j)),
                   pl.BlockSpec((tm, LANES), lambda i, j: (i, 0)),
                   pl.BlockSpec((Z_MK - Z_MQ, tm), lambda i, j: (0, i)),
                   pl.BlockSpec((Z_MO - Z_MV, tm), lambda i, j: (0, i))],
        out_shape=[jax.ShapeDtypeStruct((m, NZ), BF16), jax.ShapeDtypeStruct((m, LANES), F32),
                   jax.ShapeDtypeStruct((Z_MK - Z_MQ, m), BF16), jax.ShapeDtypeStruct((Z_MO - Z_MV, m), BF16)],
        scratch_shapes=[pltpu.VMEM((2, tm, d), BF16)],
        compiler_params=_params(),
        name="in_projection",
    )(x, x, modtab, modtab, g1, w_z, b_z, w_g, b_g, cos_tab, sin_tab, jnp.tile(q_g, A_GROUP).reshape(1, gw),
      jnp.tile(k_g, A_GROUP).reshape(1, gw))


def _gateprep_kernel(g_ref, o_ref, *, n_chunks):
    r = lax.broadcasted_iota(jnp.int32, (M_CHUNK, M_CHUNK), 0)
    c = lax.broadcasted_iota(jnp.int32, (M_CHUNK, M_CHUNK), 1)
    tri_lo = (c <= r).astype(F32)
    tri_up = (c >= r).astype(F32)
    lane = lax.broadcasted_iota(jnp.int32, (M_CHUNK, LANES), 1)
    ng = 4 * M_HEADS
    for n in range(n_chunks):
        g = g_ref[n * M_CHUNK:(n + 1) * M_CHUNK, :]
        lf = _log_sigmoid(g)
        cf = jnp.dot(tri_lo, lf, preferred_element_type=F32, precision=lax.Precision.HIGHEST)
        cb = jnp.dot(tri_up, lf, preferred_element_type=F32, precision=lax.Precision.HIGHEST)
        fwd_f = (lane >= M_HEADS) & (lane < 2 * M_HEADS)
        bwd_f = (lane >= 3 * M_HEADS) & (lane < 4 * M_HEADS)
        res = jnp.where(fwd_f, cf, jnp.where(bwd_f, cb, g))
        o_ref[:, n * M_CHUNK:(n + 1) * M_CHUNK] = res.T[0:ng, :]


def _gate_prep(gates, tm):
    m = gates.shape[0]
    ng = 4 * M_HEADS
    return pl.pallas_call(
        functools.partial(_gateprep_kernel, n_chunks=tm // M_CHUNK),
        grid=(m // tm,),
        in_specs=[pl.BlockSpec((tm, LANES), lambda i: (i, 0))],
        out_specs=pl.BlockSpec((ng, tm), lambda i: (0, i)),
        out_shape=jax.ShapeDtypeStruct((ng, m), F32),
        compiler_params=_params(),
        name="mlstm_gate_prep",
    )(gates)


M_AUG = 16


def _mlstm_prepare(k, qt, vta, i_row, b_row, c_col, btot, mask):
    dmat = jnp.where(mask, b_row + c_col, -jnp.inf)
    a_row = btot - b_row + i_row
    m_loc = jnp.max(a_row, axis=-1, keepdims=True)
    w_row = jnp.exp(a_row - m_loc)
    return dict(dmat=dmat, dmax=jnp.max(dmat, axis=0, keepdims=True), m_loc=m_loc,
                st=jnp.dot(k, qt, preferred_element_type=F32),
                c_loc=jnp.dot((vta.astype(F32) * w_row).astype(BF16), k, preferred_element_type=F32))


def _mlstm_apply(p, qt, vta, b_row, btot, c_prev, m_prev):
    scale = M_QK_DIM ** -0.5
    dv = M_V_DIM
    g_row = b_row + m_prev
    m_j = jnp.maximum(g_row, p["dmax"])
    inter = jnp.exp(g_row - m_j)
    st = p["st"] * scale * jnp.exp(p["dmat"] - m_j)
    lhs = jnp.concatenate([c_prev.astype(BF16), vta], axis=1)
    rhs = jnp.concatenate([(qt.astype(F32) * (inter * scale)).astype(BF16), st.astype(BF16)], axis=0)
    num = jnp.dot(lhs, rhs, preferred_element_type=F32)
    h = num[0:dv, :] / jnp.maximum(jnp.abs(num[dv:dv + 1, :]), jnp.exp(-m_j))
    m_new = jnp.maximum(btot + m_prev, p["m_loc"])
    s_old = jnp.exp(btot + m_prev - m_new)
    s_new = jnp.exp(p["m_loc"] - m_new)
    return h, s_old * c_prev + s_new * p["c_loc"], m_new


M_HEAD_GROUP = 1
M_UNROLL = 4


def _mlstm_kernel(kl, qtl, vtl, arl, kc, qtc, vtc, arc, ng_ref, out_l, out_c, hf_l, hb_l, hf_c, hb_c,
                  *, n_lat, n_ctx):
    L = M_CHUNK
    dk, dv, hg = M_QK_DIM, M_V_DIM, M_HEAD_GROUP
    r = lax.broadcasted_iota(jnp.int32, (L, L), 0)
    c = lax.broadcasted_iota(jnp.int32, (L, L), 1)
    mask_f = r <= c
    mask_b = r >= c
    aug = (lax.broadcasted_iota(jnp.int32, (M_AUG, L), 0) == 0).astype(BF16)
    ng = [jnp.broadcast_to(ng_ref[hh], (dv, L)) for hh in range(hg)]

    def group(k_r, qt_r, vt_r, ar_r, hf_r, hb_r, chunk_pairs, states):
        work = []
        for n_f, n_b in chunk_pairs:
            for hh in range(hg):
                for d, (n, first, tot_lane, mask, h_r) in enumerate(((n_f, 0, L - 1, mask_f, hf_r),
                                                                     (n_b, 2, 0, mask_b, hb_r))):
                    rows = pl.ds(pl.multiple_of(n * L, L), L)
                    ar = ar_r[hh, :, rows]
                    i_row, b_row = ar[first:first + 1, :], ar[first + 1:first + 2, :]
                    c_col = jnp.broadcast_to(i_row - b_row, (8, L)).T[:, 0:1]
                    btot = b_row[:, tot_lane:tot_lane + 1]
                    vta = jnp.concatenate([vt_r[hh * dv:(hh + 1) * dv, rows], aug], axis=0)
                    qt = qt_r[hh * dk:(hh + 1) * dk, rows]
                    p = _mlstm_prepare(k_r[rows, hh * dk:(hh + 1) * dk], qt, vta, i_row, b_row, c_col, btot, mask)
                    work.append((hh, d, h_r, rows, p, qt, vta, b_row, btot))
        states = [list(st) for st in states]
        for hh, d, h_r, rows, p, qt, vta, b_row, btot in work:
            h, c_new, m_new = _mlstm_apply(p, qt, vta, b_row, btot, *states[hh][d])
            h_r[hh, :, rows] = h
            states[hh][d] = (c_new, m_new)
        return tuple(tuple(st) for st in states)

    def finish(hf_r, hb_r, out_r, n):
        rows = pl.ds(pl.multiple_of(n * L, L), L)
        for hh in range(hg):
            h = hf_r[hh, :, rows] + hb_r[hh, :, rows]
            y = h * lax.rsqrt(jnp.mean(h * h, axis=0, keepdims=True) + EPS) * ng[hh]
            out_r[rows, hh * dv:(hh + 1) * dv] = y.T.astype(out_r.dtype)

    empty = (jnp.zeros((dv + M_AUG, dk), F32), jnp.full((1, 1), -jnp.inf, F32))
    states = tuple((empty, empty) for _ in range(hg))
    states = group(kc, qtc, vtc, arc, hf_c, hb_c, [(n, n_ctx - 1 - n) for n in range(n_ctx)], states)

    def body(g, states):
        pairs = [(g * M_UNROLL + u, n_lat - 1 - (g * M_UNROLL + u)) for u in range(M_UNROLL)]
        return group(kl, qtl, vtl, arl, hf_l, hb_l, pairs, states)
    lax.fori_loop(0, n_lat // M_UNROLL, body, states)

    for n in range(n_ctx):
        finish(hf_c, hb_c, out_c, n)

    def fin_body(t, carry):
        finish(hf_l, hb_l, out_l, t)
        return carry
    lax.fori_loop(0, n_lat, fin_body, 0, unroll=4)


def _mlstm(z, qt, vt, auxr, norm_g, n_batch, t_lat, t_ctx):
    dk, dv, nh, hg = M_QK_DIM, M_V_DIM, M_HEADS, M_HEAD_GROUP
    gk, gv = hg * dk, hg * dv
    cb = n_batch * t_lat // t_ctx
    in_specs = [
        pl.BlockSpec((t_lat, gk), lambda b, h: (b, Z_MK // gk + h)),
        pl.BlockSpec((gk, t_lat), lambda b, h: (h, b)), pl.BlockSpec((gv, t_lat), lambda b, h: (h, b)),
        pl.BlockSpec((hg, 8, t_lat), lambda b, h: (h, 0, b)),
        pl.BlockSpec((t_ctx, gk), lambda b, h: (cb + b, Z_MK // gk + h)),
        pl.BlockSpec((gk, t_ctx), lambda b, h: (h, cb + b)), pl.BlockSpec((gv, t_ctx), lambda b, h: (h, cb + b)),
        pl.BlockSpec((hg, 8, t_ctx), lambda b, h: (h, 0, cb + b)),
        pl.BlockSpec((hg, dv, 1), lambda b, h: (h, 0, 0)),
    ]
    out_specs = [pl.BlockSpec((t_lat, gv), lambda b, h: (b, h)), pl.BlockSpec((t_ctx, gv), lambda b, h: (b, h))]
    out_shape = [jax.ShapeDtypeStruct((n_batch * t_lat, nh * dv), BF16),
                 jax.ShapeDtypeStruct((n_batch * t_ctx, nh * dv), BF16)]
    scratch = [pltpu.VMEM((hg, dv, t_lat), F32), pltpu.VMEM((hg, dv, t_lat), F32),
               pltpu.VMEM((hg, dv, t_ctx), F32), pltpu.VMEM((hg, dv, t_ctx), F32)]
    return pl.pallas_call(
        functools.partial(_mlstm_kernel, n_lat=t_lat // M_CHUNK, n_ctx=t_ctx // M_CHUNK),
        grid=(n_batch, nh // hg),
        in_specs=in_specs, out_specs=out_specs, out_shape=out_shape, scratch_shapes=scratch,
        compiler_params=_params(),
        name="mlstm",
    )(z, qt, vt, auxr, z, qt, vt, auxr, norm_g.reshape(nh, dv, 1))


def _conv_kernel(a_ref, g_ref, ap_ref, gp_ref, an_ref, gn_ref, w_ref, b_ref, lg_ref, lb_ref, o_ref, y_scr, c_scr,
                 sh_scr, *, tt, tiles_per_seq, row_sub, col_w):
    i = pl.program_id(1)
    hl = CONV_HALO
    glu = lambda a, g: a.astype(F32) * _sigmoid(g.astype(F32))
    prev_ok = (i > 0).astype(F32)
    next_ok = (i < tiles_per_seq - 1).astype(F32)
    y_scr[0:hl, :] = glu(ap_ref[...], gp_ref[...]) * prev_ok
    y_scr[hl:hl + tt, :] = glu(a_ref[...], g_ref[...])
    y_scr[hl + tt:hl + tt + hl, :] = glu(an_ref[...], gn_ref[...]) * next_ok
    d = a_ref.shape[1]
    sub = 8
    taps = [[j for j in range(CONV_WIDTH) if (j + hl - CONV_WIDTH // 2) % sub == r] for r in range(sub)]
    span = row_sub + 2 * hl

    def col_body(cb, carry):
        cols = pl.ds(pl.multiple_of(cb * col_w, col_w), col_w)
        wc = w_ref[:, cols]
        for n, r0 in enumerate(range(0, tt, row_sub)):
            acc = jnp.zeros((row_sub, col_w), F32)
            for r in range(sub):
                slot = (n % 2) * sub + r
                sh_scr[slot] = y_scr[r0 + r:r0 + r + span - sub, cols]
                for j in taps[r]:
                    a0 = j + hl - CONV_WIDTH // 2 - r
                    acc = acc + sh_scr[slot, a0:a0 + row_sub, :] * wc[j:j + 1, :]
            c_scr[r0:r0 + row_sub, cols] = acc
        return carry
    lax.fori_loop(0, d // col_w, col_body, 0)
    y = c_scr[...] + b_ref[...]
    mu = jnp.mean(y, axis=-1, keepdims=True)
    yc = y - mu
    var = jnp.mean(yc * yc, axis=-1, keepdims=True)
    o_ref[...] = _silu(yc * lax.rsqrt(var + EPS) * lg_ref[...] + lb_ref[...]).astype(o_ref.dtype)


def _conv_branch(z, row0, n_seq, t_seq, tt, conv_w, conv_b, ln_g, ln_b):
    d = conv_w.shape[1]
    hl = CONV_HALO
    tps = t_seq // tt
    rb0, hb0, hps = row0 // tt, row0 // hl, t_seq // hl
    ca, cg = Z_CA // d, Z_CG // d
    cur = lambda col: pl.BlockSpec((tt, d), lambda s, i: (rb0 + s * tps + i, col))
    prv = lambda col: pl.BlockSpec(
        (hl, d), lambda s, i: (hb0 + s * hps + jnp.maximum(i * (tt // hl) - 1, 0), col))
    nxt = lambda col: pl.BlockSpec(
        (hl, d), lambda s, i: (hb0 + s * hps + jnp.minimum((i + 1) * (tt // hl), hps - 1), col))
    vec = pl.BlockSpec((1, d), lambda s, i: (0, 0))
    wpad = jnp.zeros((32, d), F32).at[:CONV_WIDTH].set(conv_w)
    row_sub, col_w = 128, LANES
    return pl.pallas_call(
        functools.partial(_conv_kernel, tt=tt, tiles_per_seq=tps, row_sub=row_sub, col_w=col_w),
        grid=(n_seq, tps),
        in_specs=[cur(ca), cur(cg), prv(ca), prv(cg), nxt(ca), nxt(cg),
                  pl.BlockSpec((32, d), lambda s, i: (0, 0)), vec, vec, vec],
        out_specs=pl.BlockSpec((tt, d), lambda s, i: (s * tps + i, 0)),
        out_shape=jax.ShapeDtypeStruct((n_seq * t_seq, d), BF16),
        scratch_shapes=[pltpu.VMEM((tt + 2 * hl, d), F32), pltpu.VMEM((tt, d), F32),
                        pltpu.VMEM((16, row_sub + 2 * hl - 8, col_w), F32)],
        compiler_params=_params(),
        name="conv_branch",
    )(z, z, z, z, z, z, wpad, conv_b.reshape(1, d), ln_g.reshape(1, d), ln_b.reshape(1, d))


def _head_block_diag():
    gw = A_GROUP * A_HEAD_DIM
    r = lax.broadcasted_iota(jnp.int32, (gw, gw), 0) // A_HEAD_DIM
    c = lax.broadcasted_iota(jnp.int32, (gw, gw), 1) // A_HEAD_DIM
    return (r == c).astype(BF16)


def _headnorm_rope(x, bd, gain, cos, sin, out_scale=1.0):
    ss = jnp.dot((x * x).astype(BF16), bd, preferred_element_type=F32)
    xn = x * lax.rsqrt(ss * (1.0 / A_HEAD_DIM) + EPS) * gain
    w = x.shape[1]
    lane = lax.broadcasted_iota(jnp.int32, x.shape, 1)
    first = (lane % (2 * ROPE_FREQS)) < ROPE_FREQS
    swapped = jnp.where(first, pltpu.roll(xn, w - ROPE_FREQS, 1), pltpu.roll(xn, ROPE_FREQS, 1))
    return (xn * cos + swapped * sin) * out_scale


def _attn_kernel(sink_ref, q_ref, k_ref, v_ref, kc_ref, vc_ref, o_ref, *, local, t_seq):
    nq = pl.program_id(1)
    blk, dh, grp = A_BLOCK, A_HEAD_DIM, A_GROUP
    nt = (((1,), (1,)), ((), ()))
    if local:
        win = 3 * blk
        start = pl.multiple_of(jnp.clip(nq * blk - blk, 0, t_seq - win), blk)
        qpos = nq * blk + lax.broadcasted_iota(jnp.int32, (grp * blk, win), 0) % blk
        kpos = start + lax.broadcasted_iota(jnp.int32, (grp * blk, win), 1)
        near = jnp.abs(kpos - qpos) <= WINDOW
    rowgrp = lax.broadcasted_iota(jnp.int32, (grp * blk, 1), 0) // blk

    def with_ones(v):
        col = lax.broadcasted_iota(jnp.int32, (v.shape[0], dh), 1)
        return jnp.concatenate([v, (col == 0).astype(v.dtype)], axis=1)

    outs = []
    for h in range(A_KV_HEADS):
        cols = slice(h * dh, (h + 1) * dh)
        qs = jnp.concatenate([q_ref[:, (h * grp + g) * dh:(h * grp + g + 1) * dh] for g in range(grp)], axis=0)
        sink = jnp.zeros((grp * blk, 1), F32)
        for g in range(grp):
            sink = jnp.where(rowgrp == g, sink_ref[h * grp + g], sink)
        s_ctx = lax.dot_general(qs, kc_ref[:, cols], nt, preferred_element_type=F32)
        mx = jnp.maximum(jnp.max(s_ctx, axis=-1, keepdims=True), sink)
        if local:
            s_loc = lax.dot_general(qs, k_ref[pl.ds(start, win), cols], nt, preferred_element_type=F32)
            s_loc = jnp.where(near, s_loc, -jnp.inf)
            mx = jnp.maximum(mx, jnp.max(s_loc, axis=-1, keepdims=True))
        o = jnp.dot(jnp.exp(s_ctx - mx).astype(BF16), with_ones(vc_ref[:, cols]), preferred_element_type=F32)
        if local:
            o = o + jnp.dot(jnp.exp(s_loc - mx).astype(BF16), with_ones(v_ref[pl.ds(start, win), cols]),
                            preferred_element_type=F32)
        o = o[:, 0:dh] / (o[:, dh:dh + 1] + jnp.exp(sink - mx))
        outs.extend(o[g * blk:(g + 1) * blk, :] for g in range(grp))
    o_ref[...] = jnp.concatenate(outs, axis=1).astype(o_ref.dtype)


def _attention(sink, z, q_row0, n_batch, t_seq, t_ctx, ctx_row0, local):
    qw, kw = A_HEADS * A_HEAD_DIM, A_KV_HEADS * A_HEAD_DIM
    nqb = t_seq // A_BLOCK
    qb0, sb0, cb0 = q_row0 // A_BLOCK, q_row0 // t_seq, ctx_row0 // t_ctx
    qcol, kcol, vcol = Z_AQ // qw, Z_AK // kw, Z_AV // kw
    return pl.pallas_call(
        functools.partial(_attn_kernel, local=local, t_seq=t_seq),
        grid_spec=pltpu.PrefetchScalarGridSpec(
            num_scalar_prefetch=1,
            grid=(n_batch, nqb),
            in_specs=[pl.BlockSpec((A_BLOCK, qw), lambda b, n, s: (qb0 + b * nqb + n, qcol)),
                      pl.BlockSpec((t_seq, kw), lambda b, n, s: (sb0 + b, kcol)),
                      pl.BlockSpec((t_seq, kw), lambda b, n, s: (sb0 + b, vcol)),
                      pl.BlockSpec((t_ctx, kw), lambda b, n, s: (cb0 + b, kcol)),
                      pl.BlockSpec((t_ctx, kw), lambda b, n, s: (cb0 + b, vcol))],
            out_specs=pl.BlockSpec((A_BLOCK, qw), lambda b, n, s: (b * nqb + n, 0))),
        out_shape=jax.ShapeDtypeStruct((n_batch * t_seq, qw), BF16),
        compiler_params=_params(),
        name="attention_local" if local else "attention_ctx",
    )(sink, z, z, z, z, z)


def _merge_kernel(*refs, n_lat_tiles, has_ctx):
    n_br = 6 if has_ctx else 3
    o_ref, gl_ref, x_ref, mod_ref, wa_ref, wb_ref, wc_ref, wo_ref, xo_ref, m_scr = refs[n_br:]
    if has_ctx:
        is_ctx = pl.program_id(0) >= n_lat_tiles
        ya, yb, yc = (jnp.where(is_ctx, refs[2 * n + 1][...], refs[2 * n][...]) for n in range(3))
    else:
        ya, yb, yc = (refs[n][...] for n in range(3))
    d = x_ref.shape[1]
    ya = (ya.astype(F32) * _sigmoid(o_ref[...].astype(F32))).astype(BF16)
    for c0, cw in _chunks(d, 256):
        cs = slice(c0, c0 + cw)
        acc = _sigmoid(gl_ref[:, c0:c0 + cw].astype(F32)) * jnp.dot(ya, wa_ref[:, cs], preferred_element_type=F32)
        acc = acc + _sigmoid(gl_ref[:, d + c0:d + c0 + cw].astype(F32)) * jnp.dot(
            yb, wb_ref[:, cs], preferred_element_type=F32)
        acc = acc + _sigmoid(gl_ref[:, 2 * d + c0:2 * d + c0 + cw].astype(F32)) * jnp.dot(
            yc, wc_ref[:, cs], preferred_element_type=F32)
        m_scr[:, cs] = acc.astype(BF16)
    mm = m_scr[...]
    for c0, cw in _chunks(d, 256):
        cs = slice(c0, c0 + cw)
        xo_ref[:, cs] = x_ref[:, cs] + mod_ref[0, 2:3, cs] * jnp.dot(mm, wo_ref[:, cs], preferred_element_type=F32)


def _merge(branches, z, x, modtab, wa, wb, wc, wo, tm, n_rows, n_lat_rows, t_lat):
    d = x.shape[1]
    n_batch = modtab.shape[0] - 1
    n_lat_tiles, tpb = n_lat_rows // tm, t_lat // tm
    has_ctx = branches[0][1] is not None
    row = pl.BlockSpec((tm, d), lambda i: (i, 0))
    lat = pl.BlockSpec((tm, d), lambda i: (jnp.minimum(i, n_lat_tiles - 1), 0))
    ctx = pl.BlockSpec((tm, d), lambda i: (jnp.maximum(i - n_lat_tiles, 0), 0))
    wsp = pl.BlockSpec((d, d), lambda i: (0, 0))
    br_args = [a for pair in branches for a in (pair if has_ctx else pair[:1])]
    br_specs = [lat, ctx] * 3 if has_ctx else [lat] * 3
    return pl.pallas_call(
        functools.partial(_merge_kernel, n_lat_tiles=n_lat_tiles, has_ctx=has_ctx),
        grid=(n_rows // tm,),
        in_specs=br_specs + [pl.BlockSpec((tm, d), lambda i: (i, Z_MO // d)),
                             pl.BlockSpec((tm, 3 * d), lambda i: (i, Z_GATE // (3 * d))), row,
                             pl.BlockSpec((1, 6, d), lambda i: (_mod_index(i, n_lat_tiles, tpb, n_batch), 0, 0)),
                             wsp, wsp, wsp, wsp],
        out_specs=row,
        out_shape=jax.ShapeDtypeStruct(x.shape, F32),
        scratch_shapes=[pltpu.VMEM((tm, d), BF16)],
        input_output_aliases={len(br_args) + 2: 0},
        compiler_params=_params(),
        name="merge_out_proj",
    )(*br_args, z, z, x, modtab, wa, wb, wc, wo)


def _ffn_kernel(x_ref, mod_ref, g_ref, w1_ref, w3_ref, w2_ref, xo_ref, u_scr, *, f_chunks):
    h = _norm_mod(x_ref[...], g_ref[...], mod_ref[0, 3:4, :], mod_ref[0, 4:5, :]).astype(BF16)
    for c0, cw in f_chunks:
        a = jnp.dot(h, w1_ref[:, c0:c0 + cw], preferred_element_type=F32)
        b = jnp.dot(h, w3_ref[:, c0:c0 + cw], preferred_element_type=F32)
        u_scr[:, c0:c0 + cw] = (_silu(a) * b).astype(BF16)
    d = x_ref.shape[1]
    for c0, cw in _chunks(d, 256):
        o = jnp.dot(u_scr[...], w2_ref[:, c0:c0 + cw], preferred_element_type=F32)
        xo_ref[:, c0:c0 + cw] = x_ref[:, c0:c0 + cw] + mod_ref[0, 5:6, c0:c0 + cw] * o


def _dense_ffn(x, modtab, g2, w1, w3, w2, tm, n_rows, n_lat_rows, t_lat):
    d, f = w1.shape
    n_batch = modtab.shape[0] - 1
    n_lat_tiles, tpb = n_lat_rows // tm, t_lat // tm
    row = pl.BlockSpec((tm, d), lambda i: (i, 0))
    return pl.pallas_call(
        functools.partial(_ffn_kernel, f_chunks=_chunks(f, 256)),
        grid=(n_rows // tm,),
        in_specs=[row, pl.BlockSpec((1, 6, d), lambda i: (_mod_index(i, n_lat_tiles, tpb, n_batch), 0, 0)),
                  pl.BlockSpec((1, d), lambda i: (0, 0)),
                  pl.BlockSpec((d, f), lambda i: (0, 0)), pl.BlockSpec((d, f), lambda i: (0, 0)),
                  pl.BlockSpec((f, d), lambda i: (0, 0))],
        out_specs=row,
        out_shape=jax.ShapeDtypeStruct(x.shape, F32),
        scratch_shapes=[pltpu.VMEM((tm, f), BF16)],
        input_output_aliases={0: 0},
        compiler_params=_params(),
        name="dense_ffn",
    )(x, modtab, g2, w1, w3, w2)


R_E1, R_E2, R_R1, R_R2, R_P1, R_P2 = range(6)
HI16 = 0xFFFF0000


def _pack_bf16_pair(lo, hi):
    lo_bits = lax.bitcast_convert_type(lo.astype(BF16).astype(F32), jnp.uint32)
    hi_bits = lax.bitcast_convert_type(hi.astype(BF16).astype(F32), jnp.uint32)
    return (hi_bits & jnp.uint32(HI16)) | (lo_bits >> 16)


def _unpack_bf16_pair(u):
    lo = lax.bitcast_convert_type(u << 16, F32).astype(BF16)
    hi = lax.bitcast_convert_type(u & jnp.uint32(HI16), F32).astype(BF16)
    return lo, hi


def _router_kernel(x_ref, mod_ref, g_ref, rw_ref, hp_ref, r_ref, cnt_ref, carry_scr):
    tm, d = x_ref.shape

    @pl.when(pl.program_id(0) == 0)
    def _():
        carry_scr[...] = jnp.zeros_like(carry_scr)

    h = _norm_mod(x_ref[...], g_ref[...], mod_ref[0, 3:4, :], mod_ref[0, 4:5, :])
    hp_ref[...] = _pack_bf16_pair(h[:, :d // 2], h[:, d // 2:])
    logits = jnp.dot(h, rw_ref[...], preferred_element_type=F32, precision=lax.Precision.HIGHEST)
    lane = lax.broadcasted_iota(jnp.int32, (tm, LANES), 1).astype(F32)
    logits = jnp.where(lane < N_EXPERTS, logits, -jnp.inf)
    v1 = jnp.max(logits, axis=-1, keepdims=True)
    e1 = jnp.min(jnp.where(logits == v1, lane, float(LANES)), axis=-1, keepdims=True)
    rest = jnp.where(lane == e1, -jnp.inf, logits)
    v2 = jnp.max(rest, axis=-1, keepdims=True)
    e2 = jnp.min(jnp.where(rest == v2, lane, float(LANES)), axis=-1, keepdims=True)
    t = jnp.exp(v2 - v1)
    p1 = 1.0 / (1.0 + t)
    p2 = t / (1.0 + t)
    hot1 = lane == e1
    hot2 = lane == e2
    hot = jnp.where(hot1 | hot2, 1.0, 0.0)
    rr = lax.broadcasted_iota(jnp.int32, (tm, tm), 0)
    cc = lax.broadcasted_iota(jnp.int32, (tm, tm), 1)
    before = jnp.where(cc < rr, 1.0, 0.0).astype(BF16)
    prefix = jnp.dot(before, hot.astype(BF16), preferred_element_type=F32) + carry_scr[...]
    r1 = jnp.sum(jnp.where(hot1, prefix, 0.0), axis=-1, keepdims=True)
    r2 = jnp.sum(jnp.where(hot2, prefix, 0.0), axis=-1, keepdims=True)
    carry_scr[...] += jnp.sum(hot, axis=0, keepdims=True)
    cnt_ref[...] = carry_scr[...]
    out = jnp.zeros((tm, LANES), F32)
    for idx, val in ((R_E1, e1), (R_E2, e2), (R_R1, r1), (R_R2, r2), (R_P1, p1), (R_P2, p2)):
        out = jnp.where(lane == float(idx), val, out)
    r_ref[...] = out


def _router(x, modtab, g2, rw, tm, n_rows, t_lat):
    d = x.shape[1]
    tpb = t_lat // tm
    return pl.pallas_call(
        _router_kernel,
        grid=(n_rows // tm,),
        in_specs=[pl.BlockSpec((tm, d), lambda i: (i, 0)), pl.BlockSpec((1, 6, d), lambda i: (i // tpb, 0, 0)),
                  pl.BlockSpec((1, d), lambda i: (0, 0)), pl.BlockSpec((d, LANES), lambda i: (0, 0))],
        out_specs=[pl.BlockSpec((tm, d // 2), lambda i: (i, 0)), pl.BlockSpec((tm, LANES), lambda i: (i, 0)),
                   pl.BlockSpec((1, LANES), lambda i: (0, 0))],
        out_shape=[jax.ShapeDtypeStruct((n_rows, d // 2), jnp.uint32), jax.ShapeDtypeStruct((n_rows, LANES), F32),
                   jax.ShapeDtypeStruct((1, LANES), F32)],
        scratch_shapes=[pltpu.VMEM((1, LANES), F32)],
        compiler_params=_params(),
        name="moe_router",
    )(x, modtab, g2, rw)


def _row_copies(n, make, unroll=8):
    def start(g, c):
        for u in range(unroll):
            for cp in make(g * unroll + u):
                cp.start(priority=u % 2)
        return c
    lax.fori_loop(0, n // unroll, start, 0)

    def wait(g, c):
        for u in range(unroll):
            for cp in make(g * unroll + u):
                cp.wait()
        return c
    lax.fori_loop(0, n // unroll, wait, 0)


def _dispatch_kernel(d1_ref, d2_ref, hp_ref, xs_in, xs_hbm, sem):
    del xs_in

    def make(r):
        src = hp_ref.at[pl.ds(r, 1), :]
        return (pltpu.make_async_copy(src, xs_hbm.at[pl.ds(d1_ref[0, 0, r], 1), :], sem),
                pltpu.make_async_copy(src, xs_hbm.at[pl.ds(d2_ref[0, 0, r], 1), :], sem))
    _row_copies(hp_ref.shape[0], make)


def _dispatch(d1, d2, hp, n_sorted, tm):
    n_rows, w = hp.shape
    idx = pl.BlockSpec((1, 1, tm), lambda i: (i, 0, 0), memory_space=pltpu.SMEM)
    return pl.pallas_call(
        _dispatch_kernel,
        grid=(n_rows // tm,),
        in_specs=[idx, idx, pl.BlockSpec((tm, w), lambda i: (i, 0)), pl.BlockSpec(memory_space=pl.ANY)],
        out_specs=pl.BlockSpec(memory_space=pl.ANY),
        out_shape=jax.ShapeDtypeStruct((n_sorted, w), jnp.uint32),
        scratch_shapes=[pltpu.SemaphoreType.DMA(())],
        input_output_aliases={3: 0},
        compiler_params=_params(),
        name="moe_dispatch",
    )(d1.reshape(-1, 1, tm), d2.reshape(-1, 1, tm), hp, jnp.zeros((n_sorted, w), jnp.uint32))


def _expert_kernel(te_ref, nv_ref, xs_ref, w1_ref, w3_ref, w2_ref, y_ref, xb_scr, acc_scr, u_scr, *, n_f):
    i, f = pl.program_id(0), pl.program_id(1)
    valid = i < nv_ref[0]
    half = xs_ref.shape[1]

    @pl.when(valid & (f == 0))
    def _():
        lo, hi = _unpack_bf16_pair(xs_ref[...])
        xb_scr[:, :half] = lo
        xb_scr[:, half:] = hi

    @pl.when(valid)
    def _():
        xb = xb_scr[...]

        @pl.when(f == 0)
        def _():
            acc_scr[...] = jnp.zeros_like(acc_scr)

        for c0, cw in _chunks(w1_ref.shape[2], 256):
            a = jnp.dot(xb, w1_ref[0, :, c0:c0 + cw].astype(BF16), preferred_element_type=F32)
            b = jnp.dot(xb, w3_ref[0, :, c0:c0 + cw].astype(BF16), preferred_element_type=F32)
            u_scr[:, c0:c0 + cw] = (_silu(a) * b).astype(BF16)
        acc_scr[...] += jnp.dot(u_scr[...], w2_ref[0].astype(BF16), preferred_element_type=F32)

        @pl.when(f == n_f - 1)
        def _():
            y_ref[...] = acc_scr[...]

    @pl.when(jnp.logical_not(valid) & (f == n_f - 1))
    def _():
        y_ref[...] = jnp.zeros_like(y_ref)


def _experts(tile_expert, n_valid, xs, w1, w3, w2, tme, tf):
    ne, d, f = w1.shape
    n_tiles = xs.shape[0] // tme
    n_f = f // tf
    return pl.pallas_call(
        functools.partial(_expert_kernel, n_f=n_f),
        grid_spec=pltpu.PrefetchScalarGridSpec(
            num_scalar_prefetch=2,
            grid=(n_tiles, n_f),
            in_specs=[pl.BlockSpec((tme, d // 2), lambda i, j, te, nv: (i, 0)),
                      pl.BlockSpec((1, d, tf), lambda i, j, te, nv: (te[i], 0, j)),
                      pl.BlockSpec((1, d, tf), lambda i, j, te, nv: (te[i], 0, j)),
                      pl.BlockSpec((1, tf, d), lambda i, j, te, nv: (te[i], j, 0))],
            out_specs=pl.BlockSpec((tme, d), lambda i, j, te, nv: (i, 0)),
            scratch_shapes=[pltpu.VMEM((tme, d), BF16), pltpu.VMEM((tme, d), F32), pltpu.VMEM((tme, tf), BF16)]),
        out_shape=jax.ShapeDtypeStruct((xs.shape[0], d), F32),
        compiler_params=_params(),
        name="moe_experts",
    )(tile_expert, n_valid, xs, w1, w3, w2)


def _combine_kernel(d1_ref, d2_ref, x_ref, r_ref, mod_ref, y_hbm, xo_ref, y1_scr, y2_scr, sem):
    def make(r):
        return (pltpu.make_async_copy(y_hbm.at[pl.ds(d1_ref[0, 0, r], 1), :], y1_scr.at[pl.ds(r, 1), :], sem),
                pltpu.make_async_copy(y_hbm.at[pl.ds(d2_ref[0, 0, r], 1), :], y2_scr.at[pl.ds(r, 1), :], sem))
    _row_copies(x_ref.shape[0], make)
    r = r_ref[...]
    p1 = r[:, R_P1:R_P1 + 1]
    p2 = r[:, R_P2:R_P2 + 1]
    xo_ref[...] = x_ref[...] + mod_ref[0, 5:6, :] * (p1 * y1_scr[...] + p2 * y2_scr[...])


def _combine(d1, d2, x, y, r, modtab, tm, n_rows, t_lat):
    d = x.shape[1]
    tpb = t_lat // tm
    idx = pl.BlockSpec((1, 1, tm), lambda i: (i, 0, 0), memory_space=pltpu.SMEM)
    return pl.pallas_call(
        _combine_kernel,
        grid=(n_rows // tm,),
        in_specs=[idx, idx, pl.BlockSpec((tm, d), lambda i: (i, 0)), pl.BlockSpec((tm, LANES), lambda i: (i, 0)),
                  pl.BlockSpec((1, 6, d), lambda i: (i // tpb, 0, 0)), pl.BlockSpec(memory_space=pl.ANY)],
        out_specs=pl.BlockSpec((tm, d), lambda i: (i, 0)),
        out_shape=jax.ShapeDtypeStruct((n_rows, d), F32),
        scratch_shapes=[pltpu.VMEM((tm, d), F32), pltpu.VMEM((tm, d), F32), pltpu.SemaphoreType.DMA(())],
        compiler_params=_params(),
        name="moe_combine",
    )(d1.reshape(-1, 1, tm), d2.reshape(-1, 1, tm), x, r, modtab, y)


def _moe(x, modtab, g2, router_w, w1, w3, w2, n_rows, t_lat, tme=1024, tf=512, tm=512):
    d = x.shape[1]
    rw = jnp.zeros((d, LANES), F32).at[:, :N_EXPERTS].set(router_w)
    hp, r, cnt = _router(x, modtab, g2, rw, tm, n_rows, t_lat)
    e1, e2 = r[:, R_E1].astype(jnp.int32), r[:, R_E2].astype(jnp.int32)
    r1, r2 = r[:, R_R1].astype(jnp.int32), r[:, R_R2].astype(jnp.int32)
    counts = cnt[0, :N_EXPERTS].astype(jnp.int32)
    padded = (counts + tme - 1) // tme * tme
    ends = jnp.cumsum(padded)
    offs = ends - padded
    n_sorted = -(-(2 * n_rows + N_EXPERTS * tme) // tme) * tme
    d1, d2 = offs[e1] + r1, offs[e2] + r2
    tile_start = jnp.arange(n_sorted // tme, dtype=jnp.int32) * tme
    n_valid = (ends[-1] // tme).astype(jnp.int32).reshape(1)
    tile_expert = jnp.minimum(jnp.sum(tile_start[:, None] >= ends[None, :], axis=1), N_EXPERTS - 1).astype(jnp.int32)
    tile_expert = jnp.where(tile_start < ends[-1], tile_expert, tile_expert[jnp.maximum(n_valid[0] - 1, 0)])
    xs = _dispatch(d1, d2, hp, n_sorted, tm)
    y = _experts(tile_expert, n_valid, xs, w1, w3, w2, tme, tf)
    return _combine(d1, d2, x, y, r, modtab, tm, n_rows, t_lat)


def _rope_tables(t_lat, pad_rows):
    rows = t_lat // GRID_W
    pos_r = jnp.repeat(jnp.arange(rows, dtype=F32), GRID_W)
    pos_c = jnp.tile(jnp.arange(GRID_W, dtype=F32), rows)
    inv = 1.0 / (ROPE_BASE ** (jnp.arange(ROPE_FREQS, dtype=F32) * 2.0 / (A_HEAD_DIM // 2)))
    ar, ac = pos_r[:, None] * inv, pos_c[:, None] * inv
    cos = jnp.concatenate([jnp.cos(ar), jnp.cos(ar), jnp.cos(ac), jnp.cos(ac)], axis=1)
    sin = jnp.concatenate([-jnp.sin(ar), jnp.sin(ar), -jnp.sin(ac), jnp.sin(ac)], axis=1)
    cos = jnp.concatenate([jnp.tile(cos, (1, A_KV_HEADS)), jnp.ones((pad_rows, A_KV_HEADS * A_HEAD_DIM), F32)])
    sin = jnp.concatenate([jnp.tile(sin, (1, A_KV_HEADS)), jnp.zeros((pad_rows, A_KV_HEADS * A_HEAD_DIM), F32)])
    return cos, sin


def _split_w_in(w, b):
    o = [0, 512, 1024, 2048, 3072, 3088, 5136, 6160, 6416, 6672, 9744]
    order = [(o[0], o[4]), (o[5], o[6]), (o[6], o[7]), (o[9], o[10]), (o[7], o[8]), (o[8], o[9])]
    w_z = jnp.concatenate([w[:, a:e] for a, e in order], axis=1).astype(BF16)
    b_z = jnp.concatenate([b[a:e] for a, e in order]).reshape(1, NZ)
    ng = 4 * M_HEADS
    w_g = jnp.zeros((w.shape[0], LANES), BF16).at[:, :ng].set(w[:, o[4]:o[5]].astype(BF16))
    b_g = jnp.zeros((1, LANES), F32).at[0, :ng].set(b[o[4]:o[5]])
    return w_z, b_z, w_g, b_g


def kernel(x, c, ctx, c_ctx, w_mod, b_mod, norm1_g, norm2_g, w_in, b_in, m_norm_g, conv_w, conv_b, conv_ln_g,
           conv_ln_b, q_norm_g, k_norm_g, attn_sink, w_branch_a, w_branch_b, w_branch_c, w_out, ffn_w1, ffn_w3,
           ffn_w2, router_w, moe_w1, moe_w3, moe_w2):
    n_batch, t_lat, d = x.shape
    t_ctx = ctx.shape[1]
    depth = w_mod.shape[0]
    n_lat, n_ctx = n_batch * t_lat, n_batch * t_ctx
    m = n_lat + n_ctx
    tm = min(1024, n_ctx)
    assert t_lat % tm == 0 and n_ctx % tm == 0 and t_lat >= 3 * A_BLOCK

    xs = jnp.concatenate([x.reshape(n_lat, d), ctx.reshape(n_ctx, d)], axis=0)
    cvec = jnp.zeros((16, d), F32).at[:n_batch].set(c).at[n_batch].set(c_ctx)
    mod = _modulation(cvec, w_mod, b_mod)[:, :n_batch + 1].reshape(depth, n_batch + 1, 6, d)
    tmm = min(512, tm)
    cos_tab, sin_tab = _rope_tables(t_lat, tmm)

    for l in range(depth):
        last = l == depth - 1
        modtab = mod[l]
        w_z, b_z, w_g, b_g = _split_w_in(w_in[l], b_in[l])
        z, gates, qt, vt = _in_projection(xs, modtab, norm1_g[l].reshape(1, d), w_z, b_z, w_g, b_g, cos_tab, sin_tab,
                                          q_norm_g[l], k_norm_g[l], tmm, n_lat, t_lat)

        aux = _gate_prep(gates, tm).reshape(4, M_HEADS, m)
        auxr = jnp.pad(aux.transpose(1, 0, 2), ((0, 0), (0, 4), (0, 0)))
        ya_lat, ya_ctx = _mlstm(z, qt, vt, auxr, m_norm_g[l], n_batch, t_lat, t_ctx)

        conv_args = (conv_w[l], conv_b[l], conv_ln_g[l], conv_ln_b[l])
        yb_lat = _conv_branch(z, 0, n_batch, t_lat, min(512, t_lat), *conv_args)

        yc_lat = _attention(attn_sink[l], z, 0, n_batch, t_lat, t_ctx, n_lat, True)

        wa, wb, wc, wo = (w.astype(BF16) for w in (w_branch_a[l], w_branch_b[l], w_branch_c[l], w_out[l]))
        if not last:
            yb_ctx = _conv_branch(z, n_lat, n_batch, t_ctx, t_ctx, *conv_args)
            yc_ctx = _attention(attn_sink[l], z, n_lat, n_batch, t_ctx, t_ctx, n_lat, False)
            branches = ((ya_lat, ya_ctx), (yb_lat, yb_ctx), (yc_lat, yc_ctx))
            n_rows = m
        else:
            branches = ((ya_lat, None), (yb_lat, None), (yc_lat, None))
            n_rows = n_lat
        xs = _merge(branches, z, xs, modtab, wa, wb, wc, wo, tmm, n_rows, n_lat, t_lat)

        g2 = norm2_g[l].reshape(1, d)
        if l % 2 == 0:
            i = l // 2
            xs = _dense_ffn(xs, modtab, g2, ffn_w1[i].astype(BF16), ffn_w3[i].astype(BF16), ffn_w2[i].astype(BF16),
                            tmm, n_rows, n_lat, t_lat)
        else:
            i = l // 2
            assert last, "expert mixer is implemented for latent rows only"
            xs = _moe(xs, modtab, g2, router_w[i], moe_w1[i], moe_w3[i], moe_w2[i], n_lat, t_lat)
    return xs[:n_lat].reshape(n_batch, t_lat, d)
```

```python
import functools

import jax
import jax.numpy as jnp
from jax import lax
from jax.experimental import pallas as pl
from jax.experimental.pallas import tpu as pltpu

F32 = jnp.float32
BF16 = jnp.bfloat16

EPS = 1e-6
GRID_W = 64
M_HEADS = 4
M_QK_DIM = 128
M_V_DIM = 256
M_CHUNK = 128
CONV_WIDTH = 31
CONV_HALO = 16
A_HEADS = 16
A_KV_HEADS = 4
A_GROUP = A_HEADS // A_KV_HEADS
A_HEAD_DIM = 64
A_BLOCK = 128
WINDOW = 128
ROPE_BASE = 10000.0
ROPE_FREQS = A_HEAD_DIM // 4
N_EXPERTS = 8
LANES = 128
VMEM_LIMIT = 56 * 1024 * 1024

Z_MQ, Z_MK, Z_MV, Z_MO = 0, 512, 1024, 2048
Z_CA, Z_CG, Z_AQ, Z_GATE, Z_AK, Z_AV = 3072, 4096, 5120, 6144, 9216, 9472
NZ = 9728


def _params(**kw):
    return pltpu.CompilerParams(vmem_limit_bytes=VMEM_LIMIT, **kw)


def _sigmoid(x):
    return 1.0 / (1.0 + jnp.exp(-x))


def _silu(x):
    return x * _sigmoid(x)


def _log_sigmoid(x):
    return jnp.minimum(x, 0.0) - jnp.log(1.0 + jnp.exp(-jnp.abs(x)))


def _norm_mod(x, g, shift, scale):
    ms = jnp.mean(x * x, axis=-1, keepdims=True)
    return (x * lax.rsqrt(ms + EPS) * g) * (1.0 + scale) + shift


def _chunks(n, w):
    out, c = [], 0
    while c < n:
        out.append((c, min(w, n - c)))
        c += w
    return out


def _mod_kernel(c_ref, w_ref, b_ref, o_ref):
    s = _silu(c_ref[...]).astype(BF16)
    o_ref[0] = jnp.dot(s, w_ref[0].astype(BF16), preferred_element_type=F32) + b_ref[0]


def _modulation(cvec, w_mod, b_mod):
    depth, d, n = w_mod.shape
    tn = 1536
    return pl.pallas_call(
        _mod_kernel,
        grid=(depth, n // tn),
        in_specs=[pl.BlockSpec((cvec.shape[0], d), lambda l, j: (0, 0)),
                  pl.BlockSpec((1, d, tn), lambda l, j: (l, 0, j)),
                  pl.BlockSpec((1, 1, tn), lambda l, j: (l, 0, j))],
        out_specs=pl.BlockSpec((1, cvec.shape[0], tn), lambda l, j: (l, 0, j)),
        out_shape=jax.ShapeDtypeStruct((depth, cvec.shape[0], n), F32),
        compiler_params=_params(),
        name="adaln_mod",
    )(cvec, w_mod, b_mod.reshape(depth, 1, n))


def _inproj_kernel(x0_ref, xn_ref, xc_ref, mod0_ref, modn_ref, g_ref, w_ref, b_ref, wg_ref, bg_ref, cos_ref, sin_ref,
                   qg_ref, kg_ref, z_ref, gates_ref, qt_ref, vt_ref, h_scr, *, tn, n_tiles, n_first):
    gw = A_GROUP * A_HEAD_DIM
    slot = pl.program_id(0) % 2

    def normalise(x, mod_ref, dst_slot):
        h_scr[dst_slot] = _norm_mod(x, g_ref[...], mod_ref[0, 0:1, :], mod_ref[0, 1:2, :]).astype(BF16)

    @pl.when((pl.program_id(0) == 0) & (pl.program_id(1) == 0))
    def _():
        normalise(x0_ref[...], mod0_ref, 0)

    def project(tile):
        h = h_scr[slot]
        if tile == 0:
            gates_ref[...] = jnp.dot(h, wg_ref[...], preferred_element_type=F32) + bg_ref[...]
        if tile == n_tiles - 1:
            normalise(jnp.where(pl.program_id(0) + 1 >= n_first, xc_ref[...], xn_ref[...]), modn_ref, 1 - slot)
        def epilogue(acc, c0, cw):
            col = tile * tn + c0
            if Z_AQ <= col < Z_GATE:
                acc = _headnorm_rope(acc, _head_block_diag(), qg_ref[...], cos_ref[...], sin_ref[...],
                                     A_HEAD_DIM ** -0.5)
            elif Z_AK <= col < Z_AV:
                acc = _headnorm_rope(acc, _head_block_diag(), kg_ref[...], cos_ref[...], sin_ref[...])
            z_ref[:, c0:c0 + cw] = acc.astype(BF16)
            if Z_MQ <= col < Z_MK:
                qt_ref[col - Z_MQ:col - Z_MQ + cw, :] = acc.T.astype(BF16)
            if Z_MV <= col < Z_MO:
                vt_ref[col - Z_MV:col - Z_MV + cw, :] = acc.T.astype(BF16)

        pending = None
        for c0, cw in _chunks(tn, gw):
            acc = jnp.dot(h, w_ref[:, c0:c0 + cw], preferred_element_type=F32) + b_ref[:, c0:c0 + cw]
            if pending is not None:
                epilogue(*pending)
            pending = (acc, c0, cw)
        epilogue(*pending)

    for tile in range(n_tiles):
        @pl.when(pl.program_id(1) == tile)
        def _(tile=tile):
            project(tile)


def _mod_index(i, n_lat_tiles, tiles_per_batch, n_batch):
    return jnp.where(i < n_lat_tiles, i // tiles_per_batch, n_batch)


def _in_projection(x_lat, x_ctx, modtab, g1, w_z, b_z, w_g, b_g, cos_tab, sin_tab, q_g, k_g, tm, n_lat_rows, n_ctx_rows,
                   t_lat):
    d = x_lat.shape[1]
    m = n_lat_rows + n_ctx_rows
    n_batch = modtab.shape[0] - 1
    n_tiles = 2
    tn = NZ // n_tiles
    gw = A_GROUP * A_HEAD_DIM
    assert tn % gw == 0 and Z_MO <= tn, "chunks must align with head groups; tile 0 holds the mLSTM q and v"
    n_lat_tiles, n_ctx_tiles, tpb = n_lat_rows // tm, n_ctx_rows // tm, t_lat // tm
    ctx_off = x_ctx.shape[0] // tm - n_ctx_tiles
    nxt = lambda i: jnp.minimum(i + 1, m // tm - 1)
    tab = pl.BlockSpec((tm, gw), lambda i, j: (jnp.where(i < n_lat_tiles, i % tpb, tpb), 0))
    vec = pl.BlockSpec((1, gw), lambda i, j: (0, 0))
    return pl.pallas_call(
        functools.partial(_inproj_kernel, tn=tn, n_tiles=n_tiles, n_first=n_lat_tiles),
        grid=(m // tm, n_tiles),
        in_specs=[pl.BlockSpec((tm, d), lambda i, j: (0, 0)),
                  pl.BlockSpec((tm, d), lambda i, j: (jnp.minimum(i + 1, n_lat_tiles - 1), 0)),
                  pl.BlockSpec((tm, d), lambda i, j: (ctx_off + jnp.clip(i + 1 - n_lat_tiles, 0, n_ctx_tiles - 1), 0)),
                  pl.BlockSpec((1, 6, d), lambda i, j: (_mod_index(0, n_lat_tiles, tpb, n_batch), 0, 0)),
                  pl.BlockSpec((1, 6, d), lambda i, j: (_mod_index(nxt(i), n_lat_tiles, tpb, n_batch), 0, 0)),
                  pl.BlockSpec((1, d), lambda i, j: (0, 0)),
                  pl.BlockSpec((d, tn), lambda i, j: (0, j)),
                  pl.BlockSpec((1, tn), lambda i, j: (0, j)),
                  pl.BlockSpec((d, LANES), lambda i, j: (0, 0)),
                  pl.BlockSpec((1, LANES), lambda i, j: (0, 0)),
                  tab, tab, vec, vec],
        out_specs=[pl.BlockSpec((tm, tn), lambda i, j: (i, j)),
                   pl.BlockSpec((tm, LANES), lambda i, j: (i, 0)),
                   pl.BlockSpec((Z_MK - Z_MQ, tm), lambda i, j: (0, i)),
                   pl.BlockSpec((Z_MO - Z_MV, tm), lambda i, j: (0, i))],
        out_shape=[jax.ShapeDtypeStruct((m, NZ), BF16), jax.ShapeDtypeStruct((m, LANES), F32),
                   jax.ShapeDtypeStruct((Z_MK - Z_MQ, m), BF16), jax.ShapeDtypeStruct((Z_MO - Z_MV, m), BF16)],
        scratch_shapes=[pltpu.VMEM((2, tm, d), BF16)],
        compiler_params=_params(),
        name="in_projection",
    )(x_lat, x_lat, x_ctx, modtab, modtab, g1, w_z, b_z, w_g, b_g, cos_tab, sin_tab,
      jnp.tile(q_g, A_GROUP).reshape(1, gw), jnp.tile(k_g, A_GROUP).reshape(1, gw))


def _gateprep_kernel(g_ref, o_ref, *, n_chunks):
    r = lax.broadcasted_iota(jnp.int32, (M_CHUNK, M_CHUNK), 0)
    c = lax.broadcasted_iota(jnp.int32, (M_CHUNK, M_CHUNK), 1)
    tri_lo = (c <= r).astype(F32)
    tri_up = (c >= r).astype(F32)
    lane = lax.broadcasted_iota(jnp.int32, (M_CHUNK, LANES), 1)
    ng = 4 * M_HEADS
    for n in range(n_chunks):
        g = g_ref[n * M_CHUNK:(n + 1) * M_CHUNK, :]
        lf = _log_sigmoid(g)
        cf = jnp.dot(tri_lo, lf, preferred_element_type=F32, precision=lax.Precision.HIGHEST)
        cb = jnp.dot(tri_up, lf, preferred_element_type=F32, precision=lax.Precision.HIGHEST)
        fwd_f = (lane >= M_HEADS) & (lane < 2 * M_HEADS)
        bwd_f = (lane >= 3 * M_HEADS) & (lane < 4 * M_HEADS)
        res = jnp.where(fwd_f, cf, jnp.where(bwd_f, cb, g))
        o_ref[:, n * M_CHUNK:(n + 1) * M_CHUNK] = res.T[0:ng, :]


def _gate_prep(gates, tm):
    m = gates.shape[0]
    ng = 4 * M_HEADS
    return pl.pallas_call(
        functools.partial(_gateprep_kernel, n_chunks=tm // M_CHUNK),
        grid=(m // tm,),
        in_specs=[pl.BlockSpec((tm, LANES), lambda i: (i, 0))],
        out_specs=pl.BlockSpec((ng, tm), lambda i: (0, i)),
        out_shape=jax.ShapeDtypeStruct((ng, m), F32),
        compiler_params=_params(),
        name="mlstm_gate_prep",
    )(gates)


M_AUG = 16


def _mlstm_prepare(k, qt, vta, i_row, b_row, c_col, btot, mask):
    dmat = jnp.where(mask, b_row + c_col, -jnp.inf)
    a_row = btot - b_row + i_row
    m_loc = jnp.max(a_row, axis=-1, keepdims=True)
    w_row = jnp.exp(a_row - m_loc)
    return dict(dmat=dmat, dmax=jnp.max(dmat, axis=0, keepdims=True), m_loc=m_loc,
                st=jnp.dot(k, qt, preferred_element_type=F32),
                c_loc=jnp.dot((vta.astype(F32) * w_row).astype(BF16), k, preferred_element_type=F32))


def _mlstm_apply(p, qt, vta, b_row, btot, c_prev, m_prev):
    scale = M_QK_DIM ** -0.5
    dv = M_V_DIM
    g_row = b_row + m_prev
    m_j = jnp.maximum(g_row, p["dmax"])
    inter = jnp.exp(g_row - m_j)
    st = p["st"] * scale * jnp.exp(p["dmat"] - m_j)
    lhs = jnp.concatenate([c_prev.astype(BF16), vta], axis=1)
    rhs = jnp.concatenate([(qt.astype(F32) * (inter * scale)).astype(BF16), st.astype(BF16)], axis=0)
    num = jnp.dot(lhs, rhs, preferred_element_type=F32)
    h = num[0:dv, :] / jnp.maximum(jnp.abs(num[dv:dv + 1, :]), jnp.exp(-m_j))
    m_new = jnp.maximum(btot + m_prev, p["m_loc"])
    s_old = jnp.exp(btot + m_prev - m_new)
    s_new = jnp.exp(p["m_loc"] - m_new)
    return h, s_old * c_prev + s_new * p["c_loc"], m_new


M_HEAD_GROUP = 1
M_UNROLL = 4


def _mlstm_kernel(kl, qtl, vtl, arl, kc, qtc, vtc, arc, ng_ref, out_l, out_c, hf_l, hb_l, hf_c, hb_c,
                  *, n_lat, n_ctx):
    L = M_CHUNK
    dk, dv, hg = M_QK_DIM, M_V_DIM, M_HEAD_GROUP
    r = lax.broadcasted_iota(jnp.int32, (L, L), 0)
    c = lax.broadcasted_iota(jnp.int32, (L, L), 1)
    mask_f = r <= c
    mask_b = r >= c
    aug = (lax.broadcasted_iota(jnp.int32, (M_AUG, L), 0) == 0).astype(BF16)
    ng = [jnp.broadcast_to(ng_ref[hh], (dv, L)) for hh in range(hg)]

    def group(k_r, qt_r, vt_r, ar_r, hf_r, hb_r, chunk_pairs, states):
        work = []
        for n_f, n_b in chunk_pairs:
            for hh in range(hg):
                for d, (n, first, tot_lane, mask, h_r) in enumerate(((n_f, 0, L - 1, mask_f, hf_r),
                                                                     (n_b, 2, 0, mask_b, hb_r))):
                    rows = pl.ds(pl.multiple_of(n * L, L), L)
                    ar = ar_r[hh, :, rows]
                    i_row, b_row = ar[first:first + 1, :], ar[first + 1:first + 2, :]
                    c_col = jnp.broadcast_to(i_row - b_row, (8, L)).T[:, 0:1]
                    btot = b_row[:, tot_lane:tot_lane + 1]
                    vta = jnp.concatenate([vt_r[hh * dv:(hh + 1) * dv, rows], aug], axis=0)
                    qt = qt_r[hh * dk:(hh + 1) * dk, rows]
                    p = _mlstm_prepare(k_r[rows, hh * dk:(hh + 1) * dk], qt, vta, i_row, b_row, c_col, btot, mask)
                    work.append((hh, d, h_r, rows, p, qt, vta, b_row, btot))
        states = [list(st) for st in states]
        for hh, d, h_r, rows, p, qt, vta, b_row, btot in work:
            h, c_new, m_new = _mlstm_apply(p, qt, vta, b_row, btot, *states[hh][d])
            h_r[hh, :, rows] = h
            states[hh][d] = (c_new, m_new)
        return tuple(tuple(st) for st in states)

    def finish(hf_r, hb_r, out_r, n):
        rows = pl.ds(pl.multiple_of(n * L, L), L)
        for hh in range(hg):
            h = hf_r[hh, :, rows] + hb_r[hh, :, rows]
            y = h * lax.rsqrt(jnp.mean(h * h, axis=0, keepdims=True) + EPS) * ng[hh]
            out_r[rows, hh * dv:(hh + 1) * dv] = y.T.astype(out_r.dtype)

    empty = (jnp.zeros((dv + M_AUG, dk), F32), jnp.full((1, 1), -jnp.inf, F32))
    states = tuple((empty, empty) for _ in range(hg))
    states = group(kc, qtc, vtc, arc, hf_c, hb_c, [(n, n_ctx - 1 - n) for n in range(n_ctx)], states)

    def body(g, states):
        pairs = [(g * M_UNROLL + u, n_lat - 1 - (g * M_UNROLL + u)) for u in range(M_UNROLL)]
        return group(kl, qtl, vtl, arl, hf_l, hb_l, pairs, states)
    lax.fori_loop(0, n_lat // M_UNROLL, body, states)

    for n in range(n_ctx):
        finish(hf_c, hb_c, out_c, n)

    def fin_body(t, carry):
        finish(hf_l, hb_l, out_l, t)
        return carry
    lax.fori_loop(0, n_lat, fin_body, 0, unroll=4)


def _mlstm(z, qt, vt, auxr, norm_g, n_batch, t_lat, t_ctx):
    dk, dv, nh, hg = M_QK_DIM, M_V_DIM, M_HEADS, M_HEAD_GROUP
    gk, gv = hg * dk, hg * dv
    cb = n_batch * t_lat // t_ctx
    in_specs = [
        pl.BlockSpec((t_lat, gk), lambda b, h: (b, Z_MK // gk + h)),
        pl.BlockSpec((gk, t_lat), lambda b, h: (h, b)), pl.BlockSpec((gv, t_lat), lambda b, h: (h, b)),
        pl.BlockSpec((hg, 8, t_lat), lambda b, h: (h, 0, b)),
        pl.BlockSpec((t_ctx, gk), lambda b, h: (cb + b, Z_MK // gk + h)),
        pl.BlockSpec((gk, t_ctx), lambda b, h: (h, cb + b)), pl.BlockSpec((gv, t_ctx), lambda b, h: (h, cb + b)),
        pl.BlockSpec((hg, 8, t_ctx), lambda b, h: (h, 0, cb + b)),
        pl.BlockSpec((hg, dv, 1), lambda b, h: (h, 0, 0)),
    ]
    out_specs = [pl.BlockSpec((t_lat, gv), lambda b, h: (b, h)), pl.BlockSpec((t_ctx, gv), lambda b, h: (b, h))]
    out_shape = [jax.ShapeDtypeStruct((n_batch * t_lat, nh * dv), BF16),
                 jax.ShapeDtypeStruct((n_batch * t_ctx, nh * dv), BF16)]
    scratch = [pltpu.VMEM((hg, dv, t_lat), F32), pltpu.VMEM((hg, dv, t_lat), F32),
               pltpu.VMEM((hg, dv, t_ctx), F32), pltpu.VMEM((hg, dv, t_ctx), F32)]
    return pl.pallas_call(
        functools.partial(_mlstm_kernel, n_lat=t_lat // M_CHUNK, n_ctx=t_ctx // M_CHUNK),
        grid=(n_batch, nh // hg),
        in_specs=in_specs, out_specs=out_specs, out_shape=out_shape, scratch_shapes=scratch,
        compiler_params=_params(),
        name="mlstm",
    )(z, qt, vt, auxr, z, qt, vt, auxr, norm_g.reshape(nh, dv, 1))


def _conv_kernel(a_ref, g_ref, ap_ref, gp_ref, an_ref, gn_ref, w_ref, b_ref, lg_ref, lb_ref, o_ref, y_scr, c_scr,
                 sh_scr, *, tt, tiles_per_seq, row_sub, col_w):
    i = pl.program_id(1)
    hl = CONV_HALO
    glu = lambda a, g: a.astype(F32) * _sigmoid(g.astype(F32))
    prev_ok = (i > 0).astype(F32)
    next_ok = (i < tiles_per_seq - 1).astype(F32)
    y_scr[0:hl, :] = glu(ap_ref[...], gp_ref[...]) * prev_ok
    y_scr[hl:hl + tt, :] = glu(a_ref[...], g_ref[...])
    y_scr[hl + tt:hl + tt + hl, :] = glu(an_ref[...], gn_ref[...]) * next_ok
    d = a_ref.shape[1]
    sub = 8
    taps = [[j for j in range(CONV_WIDTH) if (j + hl - CONV_WIDTH // 2) % sub == r] for r in range(sub)]
    span = row_sub + 2 * hl

    def col_body(cb, carry):
        cols = pl.ds(pl.multiple_of(cb * col_w, col_w), col_w)
        wc = w_ref[:, cols]
        for n, r0 in enumerate(range(0, tt, row_sub)):
            acc = jnp.zeros((row_sub, col_w), F32)
            for r in range(sub):
                slot = (n % 2) * sub + r
                sh_scr[slot] = y_scr[r0 + r:r0 + r + span - sub, cols]
                for j in taps[r]:
                    a0 = j + hl - CONV_WIDTH // 2 - r
                    acc = acc + sh_scr[slot, a0:a0 + row_sub, :] * wc[j:j + 1, :]
            c_scr[r0:r0 + row_sub, cols] = acc
        return carry
    lax.fori_loop(0, d // col_w, col_body, 0)
    y = c_scr[...] + b_ref[...]
    mu = jnp.mean(y, axis=-1, keepdims=True)
    yc = y - mu
    var = jnp.mean(yc * yc, axis=-1, keepdims=True)
    o_ref[...] = _silu(yc * lax.rsqrt(var + EPS) * lg_ref[...] + lb_ref[...]).astype(o_ref.dtype)


def _conv_branch(z, row0, n_seq, t_seq, tt, conv_w, conv_b, ln_g, ln_b):
    d = conv_w.shape[1]
    hl = CONV_HALO
    tps = t_seq // tt
    rb0, hb0, hps = row0 // tt, row0 // hl, t_seq // hl
    ca, cg = Z_CA // d, Z_CG // d
    cur = lambda col: pl.BlockSpec((tt, d), lambda s, i: (rb0 + s * tps + i, col))
    prv = lambda col: pl.BlockSpec(
        (hl, d), lambda s, i: (hb0 + s * hps + jnp.maximum(i * (tt // hl) - 1, 0), col))
    nxt = lambda col: pl.BlockSpec(
        (hl, d), lambda s, i: (hb0 + s * hps + jnp.minimum((i + 1) * (tt // hl), hps - 1), col))
    vec = pl.BlockSpec((1, d), lambda s, i: (0, 0))
    wpad = jnp.zeros((32, d), F32).at[:CONV_WIDTH].set(conv_w)
    row_sub, col_w = 128, LANES
    return pl.pallas_call(
        functools.partial(_conv_kernel, tt=tt, tiles_per_seq=tps, row_sub=row_sub, col_w=col_w),
        grid=(n_seq, tps),
        in_specs=[cur(ca), cur(cg), prv(ca), prv(cg), nxt(ca), nxt(cg),
                  pl.BlockSpec((32, d), lambda s, i: (0, 0)), vec, vec, vec],
        out_specs=pl.BlockSpec((tt, d), lambda s, i: (s * tps + i, 0)),
        out_shape=jax.ShapeDtypeStruct((n_seq * t_seq, d), BF16),
        scratch_shapes=[pltpu.VMEM((tt + 2 * hl, d), F32), pltpu.VMEM((tt, d), F32),
                        pltpu.VMEM((16, row_sub + 2 * hl - 8, col_w), F32)],
        compiler_params=_params(),
        name="conv_branch",
    )(z, z, z, z, z, z, wpad, conv_b.reshape(1, d), ln_g.reshape(1, d), ln_b.reshape(1, d))


def _head_block_diag():
    gw = A_GROUP * A_HEAD_DIM
    r = lax.broadcasted_iota(jnp.int32, (gw, gw), 0) // A_HEAD_DIM
    c = lax.broadcasted_iota(jnp.int32, (gw, gw), 1) // A_HEAD_DIM
    return (r == c).astype(BF16)


def _headnorm_rope(x, bd, gain, cos, sin, out_scale=1.0):
    ss = jnp.dot((x * x).astype(BF16), bd, preferred_element_type=F32)
    xn = x * lax.rsqrt(ss * (1.0 / A_HEAD_DIM) + EPS) * gain
    w = x.shape[1]
    lane = lax.broadcasted_iota(jnp.int32, x.shape, 1)
    first = (lane % (2 * ROPE_FREQS)) < ROPE_FREQS
    swapped = jnp.where(first, pltpu.roll(xn, w - ROPE_FREQS, 1), pltpu.roll(xn, ROPE_FREQS, 1))
    return (xn * cos + swapped * sin) * out_scale


def _attn_kernel(sink_ref, q_ref, k_ref, v_ref, kc_ref, vc_ref, o_ref, *, local, t_seq):
    nq = pl.program_id(1)
    blk, dh, grp = A_BLOCK, A_HEAD_DIM, A_GROUP
    nt = (((1,), (1,)), ((), ()))
    if local:
        win = 3 * blk
        start = pl.multiple_of(jnp.clip(nq * blk - blk, 0, t_seq - win), blk)
        qpos = nq * blk + lax.broadcasted_iota(jnp.int32, (grp * blk, win), 0) % blk
        kpos = start + lax.broadcasted_iota(jnp.int32, (grp * blk, win), 1)
        near = jnp.abs(kpos - qpos) <= WINDOW
    rowgrp = lax.broadcasted_iota(jnp.int32, (grp * blk, 1), 0) // blk

    def with_ones(v):
        col = lax.broadcasted_iota(jnp.int32, (v.shape[0], dh), 1)
        return jnp.concatenate([v, (col == 0).astype(v.dtype)], axis=1)

    def scores(h):
        cols = slice(h * dh, (h + 1) * dh)
        qs = jnp.concatenate([q_ref[:, (h * grp + g) * dh:(h * grp + g + 1) * dh] for g in range(grp)], axis=0)
        s_ctx = lax.dot_general(qs, kc_ref[:, cols], nt, preferred_element_type=F32)
        s_loc = lax.dot_general(qs, k_ref[pl.ds(start, win), cols], nt, preferred_element_type=F32) if local else None
        return s_ctx, s_loc

    def attend(h, s_ctx, s_loc):
        cols = slice(h * dh, (h + 1) * dh)
        sink = jnp.zeros((grp * blk, 1), F32)
        for g in range(grp):
            sink = jnp.where(rowgrp == g, sink_ref[h * grp + g], sink)
        mx = jnp.maximum(jnp.max(s_ctx, axis=-1, keepdims=True), sink)
        if local:
            s_loc = jnp.where(near, s_loc, -jnp.inf)
            mx = jnp.maximum(mx, jnp.max(s_loc, axis=-1, keepdims=True))
        o = jnp.dot(jnp.exp(s_ctx - mx).astype(BF16), with_ones(vc_ref[:, cols]), preferred_element_type=F32)
        if local:
            o = o + jnp.dot(jnp.exp(s_loc - mx).astype(BF16), with_ones(v_ref[pl.ds(start, win), cols]),
                            preferred_element_type=F32)
        o = o[:, 0:dh] / (o[:, dh:dh + 1] + jnp.exp(sink - mx))
        return [o[g * blk:(g + 1) * blk, :] for g in range(grp)]

    outs = []
    for h in range(A_KV_HEADS):
        outs.extend(attend(h, *scores(h)))
    o_ref[...] = jnp.concatenate(outs, axis=1).astype(o_ref.dtype)


def _attention(sink, z, q_row0, n_batch, t_seq, t_ctx, ctx_row0, local):
    qw, kw = A_HEADS * A_HEAD_DIM, A_KV_HEADS * A_HEAD_DIM
    nqb = t_seq // A_BLOCK
    qb0, sb0, cb0 = q_row0 // A_BLOCK, q_row0 // t_seq, ctx_row0 // t_ctx
    qcol, kcol, vcol = Z_AQ // qw, Z_AK // kw, Z_AV // kw
    return pl.pallas_call(
        functools.partial(_attn_kernel, local=local, t_seq=t_seq),
        grid_spec=pltpu.PrefetchScalarGridSpec(
            num_scalar_prefetch=1,
            grid=(n_batch, nqb),
            in_specs=[pl.BlockSpec((A_BLOCK, qw), lambda b, n, s: (qb0 + b * nqb + n, qcol)),
                      pl.BlockSpec((t_seq, kw), lambda b, n, s: (sb0 + b, kcol)),
                      pl.BlockSpec((t_seq, kw), lambda b, n, s: (sb0 + b, vcol)),
                      pl.BlockSpec((t_ctx, kw), lambda b, n, s: (cb0 + b, kcol)),
                      pl.BlockSpec((t_ctx, kw), lambda b, n, s: (cb0 + b, vcol))],
            out_specs=pl.BlockSpec((A_BLOCK, qw), lambda b, n, s: (b * nqb + n, 0))),
        out_shape=jax.ShapeDtypeStruct((n_batch * t_seq, qw), BF16),
        compiler_params=_params(),
        name="attention_local" if local else "attention_ctx",
    )(sink, z, z, z, z, z)


def _merge_kernel(*refs, n_lat_tiles, has_ctx, split_x):
    n_br = 6 if has_ctx else 3
    n_x = 2 if split_x else 1
    o_ref, gl_ref = refs[n_br:n_br + 2]
    x_refs = refs[n_br + 2:n_br + 2 + n_x]
    mod_ref, wa_ref, wb_ref, wc_ref, wo_ref, xo_ref, m_scr = refs[n_br + 2 + n_x:]
    is_ctx = pl.program_id(0) >= n_lat_tiles
    if has_ctx:
        ya, yb, yc = (jnp.where(is_ctx, refs[2 * n + 1][...], refs[2 * n][...]) for n in range(3))
    else:
        ya, yb, yc = (refs[n][...] for n in range(3))
    x = jnp.where(is_ctx, x_refs[1][...], x_refs[0][...]) if split_x else x_refs[0][...]
    d = x.shape[1]
    ya = (ya.astype(F32) * _sigmoid(o_ref[...].astype(F32))).astype(BF16)
    for c0, cw in _chunks(d, 256):
        cs = slice(c0, c0 + cw)
        acc = _sigmoid(gl_ref[:, c0:c0 + cw].astype(F32)) * jnp.dot(ya, wa_ref[:, cs], preferred_element_type=F32)
        acc = acc + _sigmoid(gl_ref[:, d + c0:d + c0 + cw].astype(F32)) * jnp.dot(
            yb, wb_ref[:, cs], preferred_element_type=F32)
        acc = acc + _sigmoid(gl_ref[:, 2 * d + c0:2 * d + c0 + cw].astype(F32)) * jnp.dot(
            yc, wc_ref[:, cs], preferred_element_type=F32)
        m_scr[:, cs] = acc.astype(BF16)
    mm = m_scr[...]
    for c0, cw in _chunks(d, 256):
        cs = slice(c0, c0 + cw)
        xo_ref[:, cs] = x[:, cs] + mod_ref[0, 2:3, cs] * jnp.dot(mm, wo_ref[:, cs], preferred_element_type=F32)


def _merge(branches, z, x, modtab, wa, wb, wc, wo, tm, n_rows, n_lat_rows, t_lat):
    split_x = isinstance(x, tuple)
    d = wa.shape[1]
    n_batch = modtab.shape[0] - 1
    n_lat_tiles, tpb = n_lat_rows // tm, t_lat // tm
    has_ctx = branches[0][1] is not None
    assert has_ctx or not split_x
    row = pl.BlockSpec((tm, d), lambda i: (i, 0))
    lat = pl.BlockSpec((tm, d), lambda i: (jnp.minimum(i, n_lat_tiles - 1), 0))
    ctx = pl.BlockSpec((tm, d), lambda i: (jnp.maximum(i - n_lat_tiles, 0), 0))
    wsp = pl.BlockSpec((d, d), lambda i: (0, 0))
    br_args = [a for pair in branches for a in (pair if has_ctx else pair[:1])]
    br_specs = [lat, ctx] * 3 if has_ctx else [lat] * 3
    x_args, x_specs = (list(x), [lat, ctx]) if split_x else ([x], [row])
    m = z.shape[0]
    return pl.pallas_call(
        functools.partial(_merge_kernel, n_lat_tiles=n_lat_tiles, has_ctx=has_ctx, split_x=split_x),
        grid=(n_rows // tm,),
        in_specs=br_specs + [pl.BlockSpec((tm, d), lambda i: (i, Z_MO // d)),
                             pl.BlockSpec((tm, 3 * d), lambda i: (i, Z_GATE // (3 * d)))] + x_specs + [
                                 pl.BlockSpec((1, 6, d), lambda i: (_mod_index(i, n_lat_tiles, tpb, n_batch), 0, 0)),
                                 wsp, wsp, wsp, wsp],
        out_specs=row,
        out_shape=jax.ShapeDtypeStruct((m, d), F32),
        scratch_shapes=[pltpu.VMEM((tm, d), BF16)],
        input_output_aliases={} if split_x else {len(br_args) + 2: 0},
        compiler_params=_params(),
        name="merge_out_proj",
    )(*br_args, z, z, *x_args, modtab, wa, wb, wc, wo)


def _ffn_kernel(x_ref, mod_ref, g_ref, w1_ref, w3_ref, w2_ref, xo_ref, u_scr, *, f_chunks):
    h = _norm_mod(x_ref[...], g_ref[...], mod_ref[0, 3:4, :], mod_ref[0, 4:5, :]).astype(BF16)
    for c0, cw in f_chunks:
        a = jnp.dot(h, w1_ref[:, c0:c0 + cw], preferred_element_type=F32)
        b = jnp.dot(h, w3_ref[:, c0:c0 + cw], preferred_element_type=F32)
        u_scr[:, c0:c0 + cw] = (_silu(a) * b).astype(BF16)
    d = x_ref.shape[1]
    for c0, cw in _chunks(d, 256):
        o = jnp.dot(u_scr[...], w2_ref[:, c0:c0 + cw], preferred_element_type=F32)
        xo_ref[:, c0:c0 + cw] = x_ref[:, c0:c0 + cw] + mod_ref[0, 5:6, c0:c0 + cw] * o


def _dense_ffn(x, modtab, g2, w1, w3, w2, tm, n_rows, n_lat_rows, t_lat):
    d, f = w1.shape
    n_batch = modtab.shape[0] - 1
    n_lat_tiles, tpb = n_lat_rows // tm, t_lat // tm
    row = pl.BlockSpec((tm, d), lambda i: (i, 0))
    return pl.pallas_call(
        functools.partial(_ffn_kernel, f_chunks=_chunks(f, 256)),
        grid=(n_rows // tm,),
        in_specs=[row, pl.BlockSpec((1, 6, d), lambda i: (_mod_index(i, n_lat_tiles, tpb, n_batch), 0, 0)),
                  pl.BlockSpec((1, d), lambda i: (0, 0)),
                  pl.BlockSpec((d, f), lambda i: (0, 0)), pl.BlockSpec((d, f), lambda i: (0, 0)),
                  pl.BlockSpec((f, d), lambda i: (0, 0))],
        out_specs=row,
        out_shape=jax.ShapeDtypeStruct(x.shape, F32),
        scratch_shapes=[pltpu.VMEM((tm, f), BF16)],
        input_output_aliases={0: 0},
        compiler_params=_params(),
        name="dense_ffn",
    )(x, modtab, g2, w1, w3, w2)


R_E1, R_E2, R_R1, R_R2, R_P1, R_P2 = range(6)
HI16 = 0xFFFF0000


def _pack_bf16_pair(lo, hi):
    lo_bits = lax.bitcast_convert_type(lo.astype(BF16).astype(F32), jnp.uint32)
    hi_bits = lax.bitcast_convert_type(hi.astype(BF16).astype(F32), jnp.uint32)
    return (hi_bits & jnp.uint32(HI16)) | (lo_bits >> 16)


def _unpack_bf16_pair(u):
    lo = lax.bitcast_convert_type(u << 16, F32).astype(BF16)
    hi = lax.bitcast_convert_type(u & jnp.uint32(HI16), F32).astype(BF16)
    return lo, hi


def _router_kernel(x_ref, mod_ref, g_ref, rw_ref, hp_ref, r_ref, cnt_ref, carry_scr):
    tm, d = x_ref.shape

    @pl.when(pl.program_id(0) == 0)
    def _():
        carry_scr[...] = jnp.zeros_like(carry_scr)

    h = _norm_mod(x_ref[...], g_ref[...], mod_ref[0, 3:4, :], mod_ref[0, 4:5, :])
    hp_ref[...] = _pack_bf16_pair(h[:, :d // 2], h[:, d // 2:])
    logits = jnp.dot(h, rw_ref[...], preferred_element_type=F32, precision=lax.Precision.HIGHEST)
    lane = lax.broadcasted_iota(jnp.int32, (tm, LANES), 1).astype(F32)
    logits = jnp.where(lane < N_EXPERTS, logits, -jnp.inf)
    v1 = jnp.max(logits, axis=-1, keepdims=True)
    e1 = jnp.min(jnp.where(logits == v1, lane, float(LANES)), axis=-1, keepdims=True)
    rest = jnp.where(lane == e1, -jnp.inf, logits)
    v2 = jnp.max(rest, axis=-1, keepdims=True)
    e2 = jnp.min(jnp.where(rest == v2, lane, float(LANES)), axis=-1, keepdims=True)
    t = jnp.exp(v2 - v1)
    p1 = 1.0 / (1.0 + t)
    p2 = t / (1.0 + t)
    hot1 = lane == e1
    hot2 = lane == e2
    hot = jnp.where(hot1 | hot2, 1.0, 0.0)
    rr = lax.broadcasted_iota(jnp.int32, (tm, tm), 0)
    cc = lax.broadcasted_iota(jnp.int32, (tm, tm), 1)
    before = jnp.where(cc < rr, 1.0, 0.0).astype(BF16)
    prefix = jnp.dot(before, hot.astype(BF16), preferred_element_type=F32) + carry_scr[...]
    r1 = jnp.sum(jnp.where(hot1, prefix, 0.0), axis=-1, keepdims=True)
    r2 = jnp.sum(jnp.where(hot2, prefix, 0.0), axis=-1, keepdims=True)
    carry_scr[...] += jnp.sum(hot, axis=0, keepdims=True)
    cnt_ref[...] = carry_scr[...]
    out = jnp.zeros((tm, LANES), F32)
    for idx, val in ((R_E1, e1), (R_E2, e2), (R_R1, r1), (R_R2, r2), (R_P1, p1), (R_P2, p2)):
        out = jnp.where(lane == float(idx), val, out)
    r_ref[...] = out


def _router(x, modtab, g2, rw, tm, n_rows, t_lat):
    d = x.shape[1]
    tpb = t_lat // tm
    return pl.pallas_call(
        _router_kernel,
        grid=(n_rows // tm,),
        in_specs=[pl.BlockSpec((tm, d), lambda i: (i, 0)), pl.BlockSpec((1, 6, d), lambda i: (i // tpb, 0, 0)),
                  pl.BlockSpec((1, d), lambda i: (0, 0)), pl.BlockSpec((d, LANES), lambda i: (0, 0))],
        out_specs=[pl.BlockSpec((tm, d // 2), lambda i: (i, 0)), pl.BlockSpec((tm, LANES), lambda i: (i, 0)),
                   pl.BlockSpec((1, LANES), lambda i: (0, 0))],
        out_shape=[jax.ShapeDtypeStruct((n_rows, d // 2), jnp.uint32), jax.ShapeDtypeStruct((n_rows, LANES), F32),
                   jax.ShapeDtypeStruct((1, LANES), F32)],
        scratch_shapes=[pltpu.VMEM((1, LANES), F32)],
        compiler_params=_params(),
        name="moe_router",
    )(x, modtab, g2, rw)


def _row_copies(n, make, unroll=8):
    def start(g, c):
        for u in range(unroll):
            for cp in make(g * unroll + u):
                cp.start(priority=u % 2)
        return c
    lax.fori_loop(0, n // unroll, start, 0)

    def wait(g, c):
        for u in range(unroll):
            for cp in make(g * unroll + u):
                cp.wait()
        return c
    lax.fori_loop(0, n // unroll, wait, 0)


def _dispatch_kernel(d1_ref, d2_ref, hp_ref, xs_in, xs_hbm, sem):
    del xs_in

    def make(r):
        src = hp_ref.at[pl.ds(r, 1), :]
        return (pltpu.make_async_copy(src, xs_hbm.at[pl.ds(d1_ref[0, 0, r], 1), :], sem),
                pltpu.make_async_copy(src, xs_hbm.at[pl.ds(d2_ref[0, 0, r], 1), :], sem))
    _row_copies(hp_ref.shape[0], make)


def _dispatch(d1, d2, hp, n_sorted, tm):
    n_rows, w = hp.shape
    idx = pl.BlockSpec((1, 1, tm), lambda i: (i, 0, 0), memory_space=pltpu.SMEM)
    return pl.pallas_call(
        _dispatch_kernel,
        grid=(n_rows // tm,),
        in_specs=[idx, idx, pl.BlockSpec((tm, w), lambda i: (i, 0)), pl.BlockSpec(memory_space=pl.ANY)],
        out_specs=pl.BlockSpec(memory_space=pl.ANY),
        out_shape=jax.ShapeDtypeStruct((n_sorted, w), jnp.uint32),
        scratch_shapes=[pltpu.SemaphoreType.DMA(())],
        input_output_aliases={3: 0},
        compiler_params=_params(),
        name="moe_dispatch",
    )(d1.reshape(-1, 1, tm), d2.reshape(-1, 1, tm), hp, jnp.zeros((n_sorted, w), jnp.uint32))


def _expert_kernel(te_ref, nv_ref, xs_ref, w1_ref, w3_ref, w2_ref, y_ref, xb_scr, acc_scr, u_scr, *, n_f):
    i, f = pl.program_id(0), pl.program_id(1)
    valid = i < nv_ref[0]
    half = xs_ref.shape[1]

    @pl.when(valid & (f == 0))
    def _():
        lo, hi = _unpack_bf16_pair(xs_ref[...])
        xb_scr[:, :half] = lo
        xb_scr[:, half:] = hi

    @pl.when(valid)
    def _():
        xb = xb_scr[...]

        @pl.when(f == 0)
        def _():
            acc_scr[...] = jnp.zeros_like(acc_scr)

        for c0, cw in _chunks(w1_ref.shape[2], 256):
            a = jnp.dot(xb, w1_ref[0, :, c0:c0 + cw].astype(BF16), preferred_element_type=F32)
            b = jnp.dot(xb, w3_ref[0, :, c0:c0 + cw].astype(BF16), preferred_element_type=F32)
            u_scr[:, c0:c0 + cw] = (_silu(a) * b).astype(BF16)
        acc_scr[...] += jnp.dot(u_scr[...], w2_ref[0].astype(BF16), preferred_element_type=F32)

        @pl.when(f == n_f - 1)
        def _():
            y_ref[...] = acc_scr[...]

    @pl.when(jnp.logical_not(valid) & (f == n_f - 1))
    def _():
        y_ref[...] = jnp.zeros_like(y_ref)


def _experts(tile_expert, n_valid, xs, w1, w3, w2, tme, tf):
    ne, d, f = w1.shape
    n_tiles = xs.shape[0] // tme
    n_f = f // tf
    return pl.pallas_call(
        functools.partial(_expert_kernel, n_f=n_f),
        grid_spec=pltpu.PrefetchScalarGridSpec(
            num_scalar_prefetch=2,
            grid=(n_tiles, n_f),
            in_specs=[pl.BlockSpec((tme, d // 2), lambda i, j, te, nv: (i, 0)),
                      pl.BlockSpec((1, d, tf), lambda i, j, te, nv: (te[i], 0, j)),
                      pl.BlockSpec((1, d, tf), lambda i, j, te, nv: (te[i], 0, j)),
                      pl.BlockSpec((1, tf, d), lambda i, j, te, nv: (te[i], j, 0))],
            out_specs=pl.BlockSpec((tme, d), lambda i, j, te, nv: (i, 0)),
            scratch_shapes=[pltpu.VMEM((tme, d), BF16), pltpu.VMEM((tme, d), F32), pltpu.VMEM((tme, tf), BF16)]),
        out_shape=jax.ShapeDtypeStruct((xs.shape[0], d), F32),
        compiler_params=_params(),
        name="moe_experts",
    )(tile_expert, n_valid, xs, w1, w3, w2)


def _combine_kernel(d1_ref, d2_ref, x_ref, r_ref, mod_ref, y_hbm, xo_ref, y1_scr, y2_scr, sem):
    def make(r):
        return (pltpu.make_async_copy(y_hbm.at[pl.ds(d1_ref[0, 0, r], 1), :], y1_scr.at[pl.ds(r, 1), :], sem),
                pltpu.make_async_copy(y_hbm.at[pl.ds(d2_ref[0, 0, r], 1), :], y2_scr.at[pl.ds(r, 1), :], sem))
    _row_copies(x_ref.shape[0], make)
    r = r_ref[...]
    p1 = r[:, R_P1:R_P1 + 1]
    p2 = r[:, R_P2:R_P2 + 1]
    xo_ref[...] = x_ref[...] + mod_ref[0, 5:6, :] * (p1 * y1_scr[...] + p2 * y2_scr[...])


def _combine(d1, d2, x, y, r, modtab, tm, n_rows, t_lat):
    d = x.shape[1]
    tpb = t_lat // tm
    idx = pl.BlockSpec((1, 1, tm), lambda i: (i, 0, 0), memory_space=pltpu.SMEM)
    return pl.pallas_call(
        _combine_kernel,
        grid=(n_rows // tm,),
        in_specs=[idx, idx, pl.BlockSpec((tm, d), lambda i: (i, 0)), pl.BlockSpec((tm, LANES), lambda i: (i, 0)),
                  pl.BlockSpec((1, 6, d), lambda i: (i // tpb, 0, 0)), pl.BlockSpec(memory_space=pl.ANY)],
        out_specs=pl.BlockSpec((tm, d), lambda i: (i, 0)),
        out_shape=jax.ShapeDtypeStruct((n_rows, d), F32),
        scratch_shapes=[pltpu.VMEM((tm, d), F32), pltpu.VMEM((tm, d), F32), pltpu.SemaphoreType.DMA(())],
        compiler_params=_params(),
        name="moe_combine",
    )(d1.reshape(-1, 1, tm), d2.reshape(-1, 1, tm), x, r, modtab, y)


def _moe(x, modtab, g2, router_w, w1, w3, w2, n_rows, t_lat, tme=1024, tf=512, tm=512):
    d = x.shape[1]
    rw = jnp.zeros((d, LANES), F32).at[:, :N_EXPERTS].set(router_w)
    hp, r, cnt = _router(x, modtab, g2, rw, tm, n_rows, t_lat)
    e1, e2 = r[:, R_E1].astype(jnp.int32), r[:, R_E2].astype(jnp.int32)
    r1, r2 = r[:, R_R1].astype(jnp.int32), r[:, R_R2].astype(jnp.int32)
    counts = cnt[0, :N_EXPERTS].astype(jnp.int32)
    padded = (counts + tme - 1) // tme * tme
    ends = jnp.cumsum(padded)
    offs = ends - padded
    n_sorted = -(-(2 * n_rows + N_EXPERTS * tme) // tme) * tme
    d1, d2 = offs[e1] + r1, offs[e2] + r2
    tile_start = jnp.arange(n_sorted // tme, dtype=jnp.int32) * tme
    n_valid = (ends[-1] // tme).astype(jnp.int32).reshape(1)
    tile_expert = jnp.minimum(jnp.sum(tile_start[:, None] >= ends[None, :], axis=1), N_EXPERTS - 1).astype(jnp.int32)
    tile_expert = jnp.where(tile_start < ends[-1], tile_expert, tile_expert[jnp.maximum(n_valid[0] - 1, 0)])
    xs = _dispatch(d1, d2, hp, n_sorted, tm)
    y = _experts(tile_expert, n_valid, xs, w1, w3, w2, tme, tf)
    return _combine(d1, d2, x, y, r, modtab, tm, n_rows, t_lat)


def _rope_tables(t_lat, pad_rows):
    rows = t_lat // GRID_W
    pos_r = jnp.repeat(jnp.arange(rows, dtype=F32), GRID_W)
    pos_c = jnp.tile(jnp.arange(GRID_W, dtype=F32), rows)
    inv = 1.0 / (ROPE_BASE ** (jnp.arange(ROPE_FREQS, dtype=F32) * 2.0 / (A_HEAD_DIM // 2)))
    ar, ac = pos_r[:, None] * inv, pos_c[:, None] * inv
    cos = jnp.concatenate([jnp.cos(ar), jnp.cos(ar), jnp.cos(ac), jnp.cos(ac)], axis=1)
    sin = jnp.concatenate([-jnp.sin(ar), jnp.sin(ar), -jnp.sin(ac), jnp.sin(ac)], axis=1)
    cos = jnp.concatenate([jnp.tile(cos, (1, A_KV_HEADS)), jnp.ones((pad_rows, A_KV_HEADS * A_HEAD_DIM), F32)])
    sin = jnp.concatenate([jnp.tile(sin, (1, A_KV_HEADS)), jnp.zeros((pad_rows, A_KV_HEADS * A_HEAD_DIM), F32)])
    return cos, sin


def _split_w_in(w, b):
    o = [0, 512, 1024, 2048, 3072, 3088, 5136, 6160, 6416, 6672, 9744]
    order = [(o[0], o[4]), (o[5], o[6]), (o[6], o[7]), (o[9], o[10]), (o[7], o[8]), (o[8], o[9])]
    w_z = jnp.concatenate([w[:, a:e] for a, e in order], axis=1).astype(BF16)
    b_z = jnp.concatenate([b[a:e] for a, e in order]).reshape(1, NZ)
    ng = 4 * M_HEADS
    w_g = jnp.zeros((w.shape[0], LANES), BF16).at[:, :ng].set(w[:, o[4]:o[5]].astype(BF16))
    b_g = jnp.zeros((1, LANES), F32).at[0, :ng].set(b[o[4]:o[5]])
    return w_z, b_z, w_g, b_g


def kernel(x, c, ctx, c_ctx, w_mod, b_mod, norm1_g, norm2_g, w_in, b_in, m_norm_g, conv_w, conv_b, conv_ln_g,
           conv_ln_b, q_norm_g, k_norm_g, attn_sink, w_branch_a, w_branch_b, w_branch_c, w_out, ffn_w1, ffn_w3,
           ffn_w2, router_w, moe_w1, moe_w3, moe_w2):
    n_batch, t_lat, d = x.shape
    t_ctx = ctx.shape[1]
    depth = w_mod.shape[0]
    n_lat, n_ctx = n_batch * t_lat, n_batch * t_ctx
    m = n_lat + n_ctx
    tm = min(1024, n_ctx)
    assert t_lat % tm == 0 and n_ctx % tm == 0 and t_lat >= 3 * A_BLOCK

    xs = (x.reshape(n_lat, d), ctx.reshape(n_ctx, d)) if depth > 1 else jnp.concatenate(
        [x.reshape(n_lat, d), ctx.reshape(n_ctx, d)], axis=0)
    cvec = jnp.zeros((16, d), F32).at[:n_batch].set(c).at[n_batch].set(c_ctx)
    mod = _modulation(cvec, w_mod, b_mod)[:, :n_batch + 1].reshape(depth, n_batch + 1, 6, d)
    tmm = min(512, tm)
    cos_tab, sin_tab = _rope_tables(t_lat, tmm)

    for l in range(depth):
        last = l == depth - 1
        modtab = mod[l]
        w_z, b_z, w_g, b_g = _split_w_in(w_in[l], b_in[l])
        x_lat, x_ctx = xs if isinstance(xs, tuple) else (xs, xs)
        z, gates, qt, vt = _in_projection(x_lat, x_ctx, modtab, norm1_g[l].reshape(1, d), w_z, b_z, w_g, b_g, cos_tab,
                                          sin_tab, q_norm_g[l], k_norm_g[l], tmm, n_lat, n_ctx, t_lat)

        aux = _gate_prep(gates, tm).reshape(4, M_HEADS, m)
        auxr = jnp.pad(aux.transpose(1, 0, 2), ((0, 0), (0, 4), (0, 0)))
        ya_lat, ya_ctx = _mlstm(z, qt, vt, auxr, m_norm_g[l], n_batch, t_lat, t_ctx)

        conv_args = (conv_w[l], conv_b[l], conv_ln_g[l], conv_ln_b[l])
        yb_lat = _conv_branch(z, 0, n_batch, t_lat, min(512, t_lat), *conv_args)

        yc_lat = _attention(attn_sink[l], z, 0, n_batch, t_lat, t_ctx, n_lat, True)

        wa, wb, wc, wo = (w.astype(BF16) for w in (w_branch_a[l], w_branch_b[l], w_branch_c[l], w_out[l]))
        if not last:
            yb_ctx = _conv_branch(z, n_lat, n_batch, t_ctx, t_ctx, *conv_args)
            yc_ctx = _attention(attn_sink[l], z, n_lat, n_batch, t_ctx, t_ctx, n_lat, False)
            branches = ((ya_lat, ya_ctx), (yb_lat, yb_ctx), (yc_lat, yc_ctx))
            n_rows = m
        else:
            branches = ((ya_lat, None), (yb_lat, None), (yc_lat, None))
            n_rows = n_lat
        xs = _merge(branches, z, xs, modtab, wa, wb, wc, wo, tmm, n_rows, n_lat, t_lat)

        g2 = norm2_g[l].reshape(1, d)
        if l % 2 == 0:
            i = l // 2
            xs = _dense_ffn(xs, modtab, g2, ffn_w1[i].astype(BF16), ffn_w3[i].astype(BF16), ffn_w2[i].astype(BF16),
                            tmm, n_rows, n_lat, t_lat)
        else:
            i = l // 2
            assert last, "expert mixer is implemented for latent rows only"
            xs = _moe(xs, modtab, g2, router_w[i], moe_w1[i], moe_w3[i], moe_w2[i], n_lat, t_lat)
    return xs[:n_lat].reshape(n_batch, t_lat, d)
```

```python
import functools

import jax
import jax.numpy as jnp
from jax import lax
from jax.experimental import pallas as pl
from jax.experimental.pallas import tpu as pltpu

F32 = jnp.float32
BF16 = jnp.bfloat16

EPS = 1e-6
GRID_W = 64
M_HEADS = 4
M_QK_DIM = 128
M_V_DIM = 256
M_CHUNK = 128
CONV_WIDTH = 31
CONV_HALO = 16
A_HEADS = 16
A_KV_HEADS = 4
A_GROUP = A_HEADS // A_KV_HEADS
A_HEAD_DIM = 64
A_BLOCK = 128
WINDOW = 128
ROPE_BASE = 10000.0
ROPE_FREQS = A_HEAD_DIM // 4
N_EXPERTS = 8
LANES = 128
VMEM_LIMIT = 56 * 1024 * 1024

Z_MQ, Z_MK, Z_MV, Z_MO = 0, 512, 1024, 2048
Z_CA, Z_CG, Z_AQ, Z_GATE, Z_AK, Z_AV = 3072, 4096, 5120, 6144, 9216, 9472
NZ = 9728


def _params(**kw):
    return pltpu.CompilerParams(vmem_limit_bytes=VMEM_LIMIT, **kw)


def _sigmoid(x):
    return 1.0 / (1.0 + jnp.exp(-x))


def _silu(x):
    return x * _sigmoid(x)


def _log_sigmoid(x):
    return jnp.minimum(x, 0.0) - jnp.log(1.0 + jnp.exp(-jnp.abs(x)))


def _norm_mod(x, g, shift, scale):
    ms = jnp.mean(x * x, axis=-1, keepdims=True)
    return (x * lax.rsqrt(ms + EPS) * g) * (1.0 + scale) + shift


def _chunks(n, w):
    out, c = [], 0
    while c < n:
        out.append((c, min(w, n - c)))
        c += w
    return out


def _mod_kernel(c_ref, w_ref, b_ref, o_ref):
    s = _silu(c_ref[...]).astype(BF16)
    o_ref[0] = jnp.dot(s, w_ref[0].astype(BF16), preferred_element_type=F32) + b_ref[0]


def _modulation(cvec, w_mod, b_mod):
    depth, d, n = w_mod.shape
    tn = 1536
    return pl.pallas_call(
        _mod_kernel,
        grid=(depth, n // tn),
        in_specs=[pl.BlockSpec((cvec.shape[0], d), lambda l, j: (0, 0)),
                  pl.BlockSpec((1, d, tn), lambda l, j: (l, 0, j)),
                  pl.BlockSpec((1, 1, tn), lambda l, j: (l, 0, j))],
        out_specs=pl.BlockSpec((1, cvec.shape[0], tn), lambda l, j: (l, 0, j)),
        out_shape=jax.ShapeDtypeStruct((depth, cvec.shape[0], n), F32),
        compiler_params=_params(),
        name="adaln_mod",
    )(cvec, w_mod, b_mod.reshape(depth, 1, n))


def _inproj_kernel(x0_ref, xn_ref, xc_ref, mod0_ref, modn_ref, g_ref, w_ref, b_ref, wg_ref, bg_ref, cos_ref, sin_ref,
                   qg_ref, kg_ref, z_ref, gates_ref, qt_ref, vt_ref, h_scr, *, tn, n_tiles, n_first):
    gw = A_GROUP * A_HEAD_DIM
    slot = pl.program_id(0) % 2

    def normalise(x, mod_ref, dst_slot):
        h_scr[dst_slot] = _norm_mod(x, g_ref[...], mod_ref[0, 0:1, :], mod_ref[0, 1:2, :]).astype(BF16)

    @pl.when((pl.program_id(0) == 0) & (pl.program_id(1) == 0))
    def _():
        normalise(x0_ref[...], mod0_ref, 0)

    def project(tile):
        h = h_scr[slot]
        if tile == 0:
            gates_ref[...] = jnp.dot(h, wg_ref[...], preferred_element_type=F32) + bg_ref[...]
        if tile == n_tiles - 1:
            normalise(jnp.where(pl.program_id(0) + 1 >= n_first, xc_ref[...], xn_ref[...]), modn_ref, 1 - slot)
        def epilogue(acc, c0, cw):
            col = tile * tn + c0
            if Z_AQ <= col < Z_GATE:
                acc = _headnorm_rope(acc, _head_block_diag(), qg_ref[...], cos_ref[...], sin_ref[...],
                                     A_HEAD_DIM ** -0.5)
            elif Z_AK <= col < Z_AV:
                acc = _headnorm_rope(acc, _head_block_diag(), kg_ref[...], cos_ref[...], sin_ref[...])
            z_ref[:, c0:c0 + cw] = acc.astype(BF16)
            if Z_MQ <= col < Z_MK:
                qt_ref[col - Z_MQ:col - Z_MQ + cw, :] = acc.T.astype(BF16)
            if Z_MV <= col < Z_MO:
                vt_ref[col - Z_MV:col - Z_MV + cw, :] = acc.T.astype(BF16)

        pending = None
        for c0, cw in _chunks(tn, gw):
            acc = jnp.dot(h, w_ref[:, c0:c0 + cw], preferred_element_type=F32) + b_ref[:, c0:c0 + cw]
            if pending is not None:
                epilogue(*pending)
            pending = (acc, c0, cw)
        epilogue(*pending)

    for tile in range(n_tiles):
        @pl.when(pl.program_id(1) == tile)
        def _(tile=tile):
            project(tile)


def _mod_index(i, n_lat_tiles, tiles_per_batch, n_batch):
    return jnp.where(i < n_lat_tiles, i // tiles_per_batch, n_batch)


def _in_projection(x_lat, x_ctx, modtab, g1, w_z, b_z, w_g, b_g, cos_tab, sin_tab, q_g, k_g, tm, n_lat_rows, n_ctx_rows,
                   t_lat):
    d = x_lat.shape[1]
    m = n_lat_rows + n_ctx_rows
    n_batch = modtab.shape[0] - 1
    n_tiles = 2
    tn = NZ // n_tiles
    gw = A_GROUP * A_HEAD_DIM
    assert tn % gw == 0 and Z_MO <= tn, "chunks must align with head groups; tile 0 holds the mLSTM q and v"
    n_lat_tiles, n_ctx_tiles, tpb = n_lat_rows // tm, n_ctx_rows // tm, t_lat // tm
    ctx_off = x_ctx.shape[0] // tm - n_ctx_tiles
    nxt = lambda i: jnp.minimum(i + 1, m // tm - 1)
    tab = pl.BlockSpec((tm, gw), lambda i, j: (jnp.where(i < n_lat_tiles, i % tpb, tpb), 0))
    vec = pl.BlockSpec((1, gw), lambda i, j: (0, 0))
    return pl.pallas_call(
        functools.partial(_inproj_kernel, tn=tn, n_tiles=n_tiles, n_first=n_lat_tiles),
        grid=(m // tm, n_tiles),
        in_specs=[pl.BlockSpec((tm, d), lambda i, j: (0, 0)),
                  pl.BlockSpec((tm, d), lambda i, j: (jnp.minimum(i + 1, n_lat_tiles - 1), 0)),
                  pl.BlockSpec((tm, d), lambda i, j: (ctx_off + jnp.clip(i + 1 - n_lat_tiles, 0, n_ctx_tiles - 1), 0)),
                  pl.BlockSpec((1, 6, d), lambda i, j: (_mod_index(0, n_lat_tiles, tpb, n_batch), 0, 0)),
                  pl.BlockSpec((1, 6, d), lambda i, j: (_mod_index(nxt(i), n_lat_tiles, tpb, n_batch), 0, 0)),
                  pl.BlockSpec((1, d), lambda i, j: (0, 0)),
                  pl.BlockSpec((d, tn), lambda i, j: (0, j)),
                  pl.BlockSpec((1, tn), lambda i, j: (0, j)),
                  pl.BlockSpec((d, LANES), lambda i, j: (0, 0)),
                  pl.BlockSpec((1, LANES), lambda i, j: (0, 0)),
                  tab, tab, vec, vec],
        out_specs=[pl.BlockSpec((tm, tn), lambda i, j: (i, j)),
                   pl.BlockSpec((tm, LANES), lambda i, j: (i, 0)),
                   pl.BlockSpec((Z_MK - Z_MQ, tm), lambda i, j: (0, i)),
                   pl.BlockSpec((Z_MO - Z_MV, tm), lambda i, j: (0, i))],
        out_shape=[jax.ShapeDtypeStruct((m, NZ), BF16), jax.ShapeDtypeStruct((m, LANES), F32),
                   jax.ShapeDtypeStruct((Z_MK - Z_MQ, m), BF16), jax.ShapeDtypeStruct((Z_MO - Z_MV, m), BF16)],
        scratch_shapes=[pltpu.VMEM((2, tm, d), BF16)],
        compiler_params=_params(),
        name="in_projection",
    )(x_lat, x_lat, x_ctx, modtab, modtab, g1, w_z, b_z, w_g, b_g, cos_tab, sin_tab,
      jnp.tile(q_g, A_GROUP).reshape(1, gw), jnp.tile(k_g, A_GROUP).reshape(1, gw))


def _gateprep_kernel(g_ref, o_ref, *, n_chunks):
    r = lax.broadcasted_iota(jnp.int32, (M_CHUNK, M_CHUNK), 0)
    c = lax.broadcasted_iota(jnp.int32, (M_CHUNK, M_CHUNK), 1)
    tri_lo = (c <= r).astype(F32)
    tri_up = (c >= r).astype(F32)
    lane = lax.broadcasted_iota(jnp.int32, (M_CHUNK, LANES), 1)
    ng = 4 * M_HEADS
    for n in range(n_chunks):
        g = g_ref[n * M_CHUNK:(n + 1) * M_CHUNK, :]
        lf = _log_sigmoid(g)
        cf = jnp.dot(tri_lo, lf, preferred_element_type=F32, precision=lax.Precision.HIGHEST)
        cb = jnp.dot(tri_up, lf, preferred_element_type=F32, precision=lax.Precision.HIGHEST)
        fwd_f = (lane >= M_HEADS) & (lane < 2 * M_HEADS)
        bwd_f = (lane >= 3 * M_HEADS) & (lane < 4 * M_HEADS)
        res = jnp.where(fwd_f, cf, jnp.where(bwd_f, cb, g))
        o_ref[:, n * M_CHUNK:(n + 1) * M_CHUNK] = res.T[0:ng, :]


def _gate_prep(gates, tm):
    m = gates.shape[0]
    ng = 4 * M_HEADS
    return pl.pallas_call(
        functools.partial(_gateprep_kernel, n_chunks=tm // M_CHUNK),
        grid=(m // tm,),
        in_specs=[pl.BlockSpec((tm, LANES), lambda i: (i, 0))],
        out_specs=pl.BlockSpec((ng, tm), lambda i: (0, i)),
        out_shape=jax.ShapeDtypeStruct((ng, m), F32),
        compiler_params=_params(),
        name="mlstm_gate_prep",
    )(gates)


M_AUG = 16


def _mlstm_prepare(k, qt, vta, i_row, b_row, c_col, btot, mask):
    dmat = jnp.where(mask, b_row + c_col, -jnp.inf)
    a_row = btot - b_row + i_row
    m_loc = jnp.max(a_row, axis=-1, keepdims=True)
    w_row = jnp.exp(a_row - m_loc)
    return dict(dmat=dmat, dmax=jnp.max(dmat, axis=0, keepdims=True), m_loc=m_loc,
                st=jnp.dot(k, qt, preferred_element_type=F32),
                c_loc=jnp.dot((vta.astype(F32) * w_row).astype(BF16), k, preferred_element_type=F32))


def _mlstm_apply(p, qt, vta, b_row, btot, c_prev, m_prev):
    scale = M_QK_DIM ** -0.5
    dv = M_V_DIM
    g_row = b_row + m_prev
    m_j = jnp.maximum(g_row, p["dmax"])
    inter = jnp.exp(g_row - m_j)
    st = p["st"] * scale * jnp.exp(p["dmat"] - m_j)
    lhs = jnp.concatenate([c_prev.astype(BF16), vta], axis=1)
    rhs = jnp.concatenate([(qt.astype(F32) * (inter * scale)).astype(BF16), st.astype(BF16)], axis=0)
    num = jnp.dot(lhs, rhs, preferred_element_type=F32)
    h = num[0:dv, :] / jnp.maximum(jnp.abs(num[dv:dv + 1, :]), jnp.exp(-m_j))
    m_new = jnp.maximum(btot + m_prev, p["m_loc"])
    s_old = jnp.exp(btot + m_prev - m_new)
    s_new = jnp.exp(p["m_loc"] - m_new)
    return h, s_old * c_prev + s_new * p["c_loc"], m_new


M_HEAD_GROUP = 1
M_UNROLL = 4


def _mlstm_kernel(kl, qtl, vtl, arl, kc, qtc, vtc, arc, ng_ref, out_l, out_c, hf_l, hb_l, hf_c, hb_c,
                  *, n_lat, n_ctx):
    L = M_CHUNK
    dk, dv, hg = M_QK_DIM, M_V_DIM, M_HEAD_GROUP
    r = lax.broadcasted_iota(jnp.int32, (L, L), 0)
    c = lax.broadcasted_iota(jnp.int32, (L, L), 1)
    mask_f = r <= c
    mask_b = r >= c
    aug = (lax.broadcasted_iota(jnp.int32, (M_AUG, L), 0) == 0).astype(BF16)
    ng = [jnp.broadcast_to(ng_ref[hh], (dv, L)) for hh in range(hg)]

    def group(k_r, qt_r, vt_r, ar_r, hf_r, hb_r, chunk_pairs, states):
        work = []
        for n_f, n_b in chunk_pairs:
            for hh in range(hg):
                for d, (n, first, tot_lane, mask, h_r) in enumerate(((n_f, 0, L - 1, mask_f, hf_r),
                                                                     (n_b, 2, 0, mask_b, hb_r))):
                    rows = pl.ds(pl.multiple_of(n * L, L), L)
                    ar = ar_r[hh, :, rows]
                    i_row, b_row = ar[first:first + 1, :], ar[first + 1:first + 2, :]
                    c_col = jnp.broadcast_to(i_row - b_row, (8, L)).T[:, 0:1]
                    btot = b_row[:, tot_lane:tot_lane + 1]
                    vta = jnp.concatenate([vt_r[hh * dv:(hh + 1) * dv, rows], aug], axis=0)
                    qt = qt_r[hh * dk:(hh + 1) * dk, rows]
                    p = _mlstm_prepare(k_r[rows, hh * dk:(hh + 1) * dk], qt, vta, i_row, b_row, c_col, btot, mask)
                    work.append((hh, d, h_r, rows, p, qt, vta, b_row, btot))
        states = [list(st) for st in states]
        for hh, d, h_r, rows, p, qt, vta, b_row, btot in work:
            h, c_new, m_new = _mlstm_apply(p, qt, vta, b_row, btot, *states[hh][d])
            h_r[hh, :, rows] = h
            states[hh][d] = (c_new, m_new)
        return tuple(tuple(st) for st in states)

    def finish(hf_r, hb_r, out_r, n):
        rows = pl.ds(pl.multiple_of(n * L, L), L)
        for hh in range(hg):
            h = hf_r[hh, :, rows] + hb_r[hh, :, rows]
            y = h * lax.rsqrt(jnp.mean(h * h, axis=0, keepdims=True) + EPS) * ng[hh]
            out_r[rows, hh * dv:(hh + 1) * dv] = y.T.astype(out_r.dtype)

    empty = (jnp.zeros((dv + M_AUG, dk), F32), jnp.full((1, 1), -jnp.inf, F32))
    states = tuple((empty, empty) for _ in range(hg))
    states = group(kc, qtc, vtc, arc, hf_c, hb_c, [(n, n_ctx - 1 - n) for n in range(n_ctx)], states)

    def body(g, states):
        pairs = [(g * M_UNROLL + u, n_lat - 1 - (g * M_UNROLL + u)) for u in range(M_UNROLL)]
        return group(kl, qtl, vtl, arl, hf_l, hb_l, pairs, states)
    lax.fori_loop(0, n_lat // M_UNROLL, body, states)

    for n in range(n_ctx):
        finish(hf_c, hb_c, out_c, n)

    def fin_body(t, carry):
        finish(hf_l, hb_l, out_l, t)
        return carry
    lax.fori_loop(0, n_lat, fin_body, 0, unroll=4)


def _mlstm(z, qt, vt, auxr, norm_g, n_batch, t_lat, t_ctx):
    dk, dv, nh, hg = M_QK_DIM, M_V_DIM, M_HEADS, M_HEAD_GROUP
    gk, gv = hg * dk, hg * dv
    cb = n_batch * t_lat // t_ctx
    in_specs = [
        pl.BlockSpec((t_lat, gk), lambda b, h: (b, Z_MK // gk + h)),
        pl.BlockSpec((gk, t_lat), lambda b, h: (h, b)), pl.BlockSpec((gv, t_lat), lambda b, h: (h, b)),
        pl.BlockSpec((hg, 8, t_lat), lambda b, h: (h, 0, b)),
        pl.BlockSpec((t_ctx, gk), lambda b, h: (cb + b, Z_MK // gk + h)),
        pl.BlockSpec((gk, t_ctx), lambda b, h: (h, cb + b)), pl.BlockSpec((gv, t_ctx), lambda b, h: (h, cb + b)),
        pl.BlockSpec((hg, 8, t_ctx), lambda b, h: (h, 0, cb + b)),
        pl.BlockSpec((hg, dv, 1), lambda b, h: (h, 0, 0)),
    ]
    out_specs = [pl.BlockSpec((t_lat, gv), lambda b, h: (b, h)), pl.BlockSpec((t_ctx, gv), lambda b, h: (b, h))]
    out_shape = [jax.ShapeDtypeStruct((n_batch * t_lat, nh * dv), BF16),
                 jax.ShapeDtypeStruct((n_batch * t_ctx, nh * dv), BF16)]
    scratch = [pltpu.VMEM((hg, dv, t_lat), F32), pltpu.VMEM((hg, dv, t_lat), F32),
               pltpu.VMEM((hg, dv, t_ctx), F32), pltpu.VMEM((hg, dv, t_ctx), F32)]
    return pl.pallas_call(
        functools.partial(_mlstm_kernel, n_lat=t_lat // M_CHUNK, n_ctx=t_ctx // M_CHUNK),
        grid=(n_batch, nh // hg),
        in_specs=in_specs, out_specs=out_specs, out_shape=out_shape, scratch_shapes=scratch,
        compiler_params=_params(),
        name="mlstm",
    )(z, qt, vt, auxr, z, qt, vt, auxr, norm_g.reshape(nh, dv, 1))


def _conv_kernel(a_ref, g_ref, ap_ref, gp_ref, an_ref, gn_ref, w_ref, b_ref, lg_ref, lb_ref, o_ref, y_scr, c_scr,
                 sh_scr, *, tt, tiles_per_seq, row_sub, col_w):
    i = pl.program_id(1)
    hl = CONV_HALO
    glu = lambda a, g: a.astype(F32) * _sigmoid(g.astype(F32))
    prev_ok = (i > 0).astype(F32)
    next_ok = (i < tiles_per_seq - 1).astype(F32)
    y_scr[0:hl, :] = glu(ap_ref[...], gp_ref[...]) * prev_ok
    y_scr[hl:hl + tt, :] = glu(a_ref[...], g_ref[...])
    y_scr[hl + tt:hl + tt + hl, :] = glu(an_ref[...], gn_ref[...]) * next_ok
    d = a_ref.shape[1]
    sub = 8
    taps = [[j for j in range(CONV_WIDTH) if (j + hl - CONV_WIDTH // 2) % sub == r] for r in range(sub)]
    span = row_sub + 2 * hl

    def col_body(cb, carry):
        cols = pl.ds(pl.multiple_of(cb * col_w, col_w), col_w)
        wc = w_ref[:, cols]
        for n, r0 in enumerate(range(0, tt, row_sub)):
            acc = jnp.zeros((row_sub, col_w), F32)
            for r in range(sub):
                slot = (n % 2) * sub + r
                sh_scr[slot] = y_scr[r0 + r:r0 + r + span - sub, cols]
                for j in taps[r]:
                    a0 = j + hl - CONV_WIDTH // 2 - r
                    acc = acc + sh_scr[slot, a0:a0 + row_sub, :] * wc[j:j + 1, :]
            c_scr[r0:r0 + row_sub, cols] = acc
        return carry
    lax.fori_loop(0, d // col_w, col_body, 0)
    y = c_scr[...] + b_ref[...]
    mu = jnp.mean(y, axis=-1, keepdims=True)
    yc = y - mu
    var = jnp.mean(yc * yc, axis=-1, keepdims=True)
    o_ref[...] = _silu(yc * lax.rsqrt(var + EPS) * lg_ref[...] + lb_ref[...]).astype(o_ref.dtype)


def _conv_branch(z, row0, n_seq, t_seq, tt, conv_w, conv_b, ln_g, ln_b):
    d = conv_w.shape[1]
    hl = CONV_HALO
    tps = t_seq // tt
    rb0, hb0, hps = row0 // tt, row0 // hl, t_seq // hl
    ca, cg = Z_CA // d, Z_CG // d
    cur = lambda col: pl.BlockSpec((tt, d), lambda s, i: (rb0 + s * tps + i, col))
    prv = lambda col: pl.BlockSpec(
        (hl, d), lambda s, i: (hb0 + s * hps + jnp.maximum(i * (tt // hl) - 1, 0), col))
    nxt = lambda col: pl.BlockSpec(
        (hl, d), lambda s, i: (hb0 + s * hps + jnp.minimum((i + 1) * (tt // hl), hps - 1), col))
    vec = pl.BlockSpec((1, d), lambda s, i: (0, 0))
    wpad = jnp.zeros((32, d), F32).at[:CONV_WIDTH].set(conv_w)
    row_sub, col_w = 128, LANES
    return pl.pallas_call(
        functools.partial(_conv_kernel, tt=tt, tiles_per_seq=tps, row_sub=row_sub, col_w=col_w),
        grid=(n_seq, tps),
        in_specs=[cur(ca), cur(cg), prv(ca), prv(cg), nxt(ca), nxt(cg),
                  pl.BlockSpec((32, d), lambda s, i: (0, 0)), vec, vec, vec],
        out_specs=pl.BlockSpec((tt, d), lambda s, i: (s * tps + i, 0)),
        out_shape=jax.ShapeDtypeStruct((n_seq * t_seq, d), BF16),
        scratch_shapes=[pltpu.VMEM((tt + 2 * hl, d), F32), pltpu.VMEM((tt, d), F32),
                        pltpu.VMEM((16, row_sub + 2 * hl - 8, col_w), F32)],
        compiler_params=_params(),
        name="conv_branch",
    )(z, z, z, z, z, z, wpad, conv_b.reshape(1, d), ln_g.reshape(1, d), ln_b.reshape(1, d))


def _head_block_diag():
    gw = A_GROUP * A_HEAD_DIM
    r = lax.broadcasted_iota(jnp.int32, (gw, gw), 0) // A_HEAD_DIM
    c = lax.broadcasted_iota(jnp.int32, (gw, gw), 1) // A_HEAD_DIM
    return (r == c).astype(BF16)


def _headnorm_rope(x, bd, gain, cos, sin, out_scale=1.0):
    ss = jnp.dot((x * x).astype(BF16), bd, preferred_element_type=F32)
    xn = x * lax.rsqrt(ss * (1.0 / A_HEAD_DIM) + EPS) * gain
    w = x.shape[1]
    lane = lax.broadcasted_iota(jnp.int32, x.shape, 1)
    first = (lane % (2 * ROPE_FREQS)) < ROPE_FREQS
    swapped = jnp.where(first, pltpu.roll(xn, w - ROPE_FREQS, 1), pltpu.roll(xn, ROPE_FREQS, 1))
    return (xn * cos + swapped * sin) * out_scale


def _attn_kernel(sink_ref, q_ref, k_ref, v_ref, kc_ref, vc_ref, o_ref, *, local, t_seq):
    nq = pl.program_id(1)
    blk, dh, grp = A_BLOCK, A_HEAD_DIM, A_GROUP
    nt = (((1,), (1,)), ((), ()))
    if local:
        win = 3 * blk
        start = pl.multiple_of(jnp.clip(nq * blk - blk, 0, t_seq - win), blk)
        qpos = nq * blk + lax.broadcasted_iota(jnp.int32, (grp * blk, win), 0) % blk
        kpos = start + lax.broadcasted_iota(jnp.int32, (grp * blk, win), 1)
        near = jnp.abs(kpos - qpos) <= WINDOW
    rowgrp = lax.broadcasted_iota(jnp.int32, (grp * blk, 1), 0) // blk

    def with_ones(v):
        col = lax.broadcasted_iota(jnp.int32, (v.shape[0], dh), 1)
        return jnp.concatenate([v, (col == 0).astype(v.dtype)], axis=1)

    def scores(h):
        cols = slice(h * dh, (h + 1) * dh)
        qs = jnp.concatenate([q_ref[:, (h * grp + g) * dh:(h * grp + g + 1) * dh] for g in range(grp)], axis=0)
        s_ctx = lax.dot_general(qs, kc_ref[:, cols], nt, preferred_element_type=F32)
        s_loc = lax.dot_general(qs, k_ref[pl.ds(start, win), cols], nt, preferred_element_type=F32) if local else None
        return s_ctx, s_loc

    def attend(h, s_ctx, s_loc):
        cols = slice(h * dh, (h + 1) * dh)
        sink = jnp.zeros((grp * blk, 1), F32)
        for g in range(grp):
            sink = jnp.where(rowgrp == g, sink_ref[h * grp + g], sink)
        mx = jnp.maximum(jnp.max(s_ctx, axis=-1, keepdims=True), sink)
        if local:
            s_loc = jnp.where(near, s_loc, -jnp.inf)
            mx = jnp.maximum(mx, jnp.max(s_loc, axis=-1, keepdims=True))
        o = jnp.dot(jnp.exp(s_ctx - mx).astype(BF16), with_ones(vc_ref[:, cols]), preferred_element_type=F32)
        if local:
            o = o + jnp.dot(jnp.exp(s_loc - mx).astype(BF16), with_ones(v_ref[pl.ds(start, win), cols]),
                            preferred_element_type=F32)
        o = o[:, 0:dh] / (o[:, dh:dh + 1] + jnp.exp(sink - mx))
        return [o[g * blk:(g + 1) * blk, :] for g in range(grp)]

    outs = []
    for h0 in range(0, A_KV_HEADS, 2):
        sc = [scores(h) for h in (h0, h0 + 1)]
        for h, s in zip((h0, h0 + 1), sc):
            outs.extend(attend(h, *s))
    o_ref[...] = jnp.concatenate(outs, axis=1).astype(o_ref.dtype)


def _attention(sink, z, q_row0, n_batch, t_seq, t_ctx, ctx_row0, local):
    qw, kw = A_HEADS * A_HEAD_DIM, A_KV_HEADS * A_HEAD_DIM
    nqb = t_seq // A_BLOCK
    qb0, sb0, cb0 = q_row0 // A_BLOCK, q_row0 // t_seq, ctx_row0 // t_ctx
    qcol, kcol, vcol = Z_AQ // qw, Z_AK // kw, Z_AV // kw
    return pl.pallas_call(
        functools.partial(_attn_kernel, local=local, t_seq=t_seq),
        grid_spec=pltpu.PrefetchScalarGridSpec(
            num_scalar_prefetch=1,
            grid=(n_batch, nqb),
            in_specs=[pl.BlockSpec((A_BLOCK, qw), lambda b, n, s: (qb0 + b * nqb + n, qcol)),
                      pl.BlockSpec((t_seq, kw), lambda b, n, s: (sb0 + b, kcol)),
                      pl.BlockSpec((t_seq, kw), lambda b, n, s: (sb0 + b, vcol)),
                      pl.BlockSpec((t_ctx, kw), lambda b, n, s: (cb0 + b, kcol)),
                      pl.BlockSpec((t_ctx, kw), lambda b, n, s: (cb0 + b, vcol))],
            out_specs=pl.BlockSpec((A_BLOCK, qw), lambda b, n, s: (b * nqb + n, 0))),
        out_shape=jax.ShapeDtypeStruct((n_batch * t_seq, qw), BF16),
        compiler_params=_params(),
        name="attention_local" if local else "attention_ctx",
    )(sink, z, z, z, z, z)


def _merge_kernel(*refs, n_lat_tiles, has_ctx, split_x):
    n_br = 6 if has_ctx else 3
    n_x = 2 if split_x else 1
    o_ref, gl_ref = refs[n_br:n_br + 2]
    x_refs = refs[n_br + 2:n_br + 2 + n_x]
    mod_ref, wa_ref, wb_ref, wc_ref, wo_ref, xo_ref, m_scr = refs[n_br + 2 + n_x:]
    is_ctx = pl.program_id(0) >= n_lat_tiles
    if has_ctx:
        ya, yb, yc = (jnp.where(is_ctx, refs[2 * n + 1][...], refs[2 * n][...]) for n in range(3))
    else:
        ya, yb, yc = (refs[n][...] for n in range(3))
    x = jnp.where(is_ctx, x_refs[1][...], x_refs[0][...]) if split_x else x_refs[0][...]
    d = x.shape[1]
    ya = (ya.astype(F32) * _sigmoid(o_ref[...].astype(F32))).astype(BF16)
    for c0, cw in _chunks(d, 256):
        cs = slice(c0, c0 + cw)
        acc = _sigmoid(gl_ref[:, c0:c0 + cw].astype(F32)) * jnp.dot(ya, wa_ref[:, cs], preferred_element_type=F32)
        acc = acc + _sigmoid(gl_ref[:, d + c0:d + c0 + cw].astype(F32)) * jnp.dot(
            yb, wb_ref[:, cs], preferred_element_type=F32)
        acc = acc + _sigmoid(gl_ref[:, 2 * d + c0:2 * d + c0 + cw].astype(F32)) * jnp.dot(
            yc, wc_ref[:, cs], preferred_element_type=F32)
        m_scr[:, cs] = acc.astype(BF16)
    mm = m_scr[...]
    for c0, cw in _chunks(d, 256):
        cs = slice(c0, c0 + cw)
        xo_ref[:, cs] = x[:, cs] + mod_ref[0, 2:3, cs] * jnp.dot(mm, wo_ref[:, cs], preferred_element_type=F32)


def _merge(branches, z, x, modtab, wa, wb, wc, wo, tm, n_rows, n_lat_rows, t_lat):
    split_x = isinstance(x, tuple)
    d = wa.shape[1]
    n_batch = modtab.shape[0] - 1
    n_lat_tiles, tpb = n_lat_rows // tm, t_lat // tm
    has_ctx = branches[0][1] is not None
    assert has_ctx or not split_x
    row = pl.BlockSpec((tm, d), lambda i: (i, 0))
    lat = pl.BlockSpec((tm, d), lambda i: (jnp.minimum(i, n_lat_tiles - 1), 0))
    ctx = pl.BlockSpec((tm, d), lambda i: (jnp.maximum(i - n_lat_tiles, 0), 0))
    wsp = pl.BlockSpec((d, d), lambda i: (0, 0))
    br_args = [a for pair in branches for a in (pair if has_ctx else pair[:1])]
    br_specs = [lat, ctx] * 3 if has_ctx else [lat] * 3
    x_args, x_specs = (list(x), [lat, ctx]) if split_x else ([x], [row])
    m = z.shape[0]
    return pl.pallas_call(
        functools.partial(_merge_kernel, n_lat_tiles=n_lat_tiles, has_ctx=has_ctx, split_x=split_x),
        grid=(n_rows // tm,),
        in_specs=br_specs + [pl.BlockSpec((tm, d), lambda i: (i, Z_MO // d)),
                             pl.BlockSpec((tm, 3 * d), lambda i: (i, Z_GATE // (3 * d)))] + x_specs + [
                                 pl.BlockSpec((1, 6, d), lambda i: (_mod_index(i, n_lat_tiles, tpb, n_batch), 0, 0)),
                                 wsp, wsp, wsp, wsp],
        out_specs=row,
        out_shape=jax.ShapeDtypeStruct((m, d), F32),
        scratch_shapes=[pltpu.VMEM((tm, d), BF16)],
        input_output_aliases={} if split_x else {len(br_args) + 2: 0},
        compiler_params=_params(),
        name="merge_out_proj",
    )(*br_args, z, z, *x_args, modtab, wa, wb, wc, wo)


def _ffn_kernel(x_ref, mod_ref, g_ref, w1_ref, w3_ref, w2_ref, xo_ref, u_scr, *, f_chunks):
    h = _norm_mod(x_ref[...], g_ref[...], mod_ref[0, 3:4, :], mod_ref[0, 4:5, :]).astype(BF16)
    for c0, cw in f_chunks:
        a = jnp.dot(h, w1_ref[:, c0:c0 + cw], preferred_element_type=F32)
        b = jnp.dot(h, w3_ref[:, c0:c0 + cw], preferred_element_type=F32)
        u_scr[:, c0:c0 + cw] = (_silu(a) * b).astype(BF16)
    d = x_ref.shape[1]
    for c0, cw in _chunks(d, 256):
        o = jnp.dot(u_scr[...], w2_ref[:, c0:c0 + cw], preferred_element_type=F32)
        xo_ref[:, c0:c0 + cw] = x_ref[:, c0:c0 + cw] + mod_ref[0, 5:6, c0:c0 + cw] * o


def _dense_ffn(x, modtab, g2, w1, w3, w2, tm, n_rows, n_lat_rows, t_lat):
    d, f = w1.shape
    n_batch = modtab.shape[0] - 1
    n_lat_tiles, tpb = n_lat_rows // tm, t_lat // tm
    row = pl.BlockSpec((tm, d), lambda i: (i, 0))
    return pl.pallas_call(
        functools.partial(_ffn_kernel, f_chunks=_chunks(f, 256)),
        grid=(n_rows // tm,),
        in_specs=[row, pl.BlockSpec((1, 6, d), lambda i: (_mod_index(i, n_lat_tiles, tpb, n_batch), 0, 0)),
                  pl.BlockSpec((1, d), lambda i: (0, 0)),
                  pl.BlockSpec((d, f), lambda i: (0, 0)), pl.BlockSpec((d, f), lambda i: (0, 0)),
                  pl.BlockSpec((f, d), lambda i: (0, 0))],
        out_specs=row,
        out_shape=jax.ShapeDtypeStruct(x.shape, F32),
        scratch_shapes=[pltpu.VMEM((tm, f), BF16)],
        input_output_aliases={0: 0},
        compiler_params=_params(),
        name="dense_ffn",
    )(x, modtab, g2, w1, w3, w2)


R_E1, R_E2, R_R1, R_R2, R_P1, R_P2 = range(6)
HI16 = 0xFFFF0000


def _pack_bf16_pair(lo, hi):
    lo_bits = lax.bitcast_convert_type(lo.astype(BF16).astype(F32), jnp.uint32)
    hi_bits = lax.bitcast_convert_type(hi.astype(BF16).astype(F32), jnp.uint32)
    return (hi_bits & jnp.uint32(HI16)) | (lo_bits >> 16)


def _unpack_bf16_pair(u):
    lo = lax.bitcast_convert_type(u << 16, F32).astype(BF16)
    hi = lax.bitcast_convert_type(u & jnp.uint32(HI16), F32).astype(BF16)
    return lo, hi


def _router_kernel(x_ref, mod_ref, g_ref, rw_ref, hp_ref, r_ref, cnt_ref, carry_scr):
    tm, d = x_ref.shape

    @pl.when(pl.program_id(0) == 0)
    def _():
        carry_scr[...] = jnp.zeros_like(carry_scr)

    h = _norm_mod(x_ref[...], g_ref[...], mod_ref[0, 3:4, :], mod_ref[0, 4:5, :])
    hp_ref[...] = _pack_bf16_pair(h[:, :d // 2], h[:, d // 2:])
    logits = jnp.dot(h, rw_ref[...], preferred_element_type=F32, precision=lax.Precision.HIGHEST)
    lane = lax.broadcasted_iota(jnp.int32, (tm, LANES), 1).astype(F32)
    logits = jnp.where(lane < N_EXPERTS, logits, -jnp.inf)
    v1 = jnp.max(logits, axis=-1, keepdims=True)
    e1 = jnp.min(jnp.where(logits == v1, lane, float(LANES)), axis=-1, keepdims=True)
    rest = jnp.where(lane == e1, -jnp.inf, logits)
    v2 = jnp.max(rest, axis=-1, keepdims=True)
    e2 = jnp.min(jnp.where(rest == v2, lane, float(LANES)), axis=-1, keepdims=True)
    t = jnp.exp(v2 - v1)
    p1 = 1.0 / (1.0 + t)
    p2 = t / (1.0 + t)
    hot1 = lane == e1
    hot2 = lane == e2
    hot = jnp.where(hot1 | hot2, 1.0, 0.0)
    rr = lax.broadcasted_iota(jnp.int32, (tm, tm), 0)
    cc = lax.broadcasted_iota(jnp.int32, (tm, tm), 1)
    before = jnp.where(cc < rr, 1.0, 0.0).astype(BF16)
    prefix = jnp.dot(before, hot.astype(BF16), preferred_element_type=F32) + carry_scr[...]
    r1 = jnp.sum(jnp.where(hot1, prefix, 0.0), axis=-1, keepdims=True)
    r2 = jnp.sum(jnp.where(hot2, prefix, 0.0), axis=-1, keepdims=True)
    carry_scr[...] += jnp.sum(hot, axis=0, keepdims=True)
    cnt_ref[...] = carry_scr[...]
    out = jnp.zeros((tm, LANES), F32)
    for idx, val in ((R_E1, e1), (R_E2, e2), (R_R1, r1), (R_R2, r2), (R_P1, p1), (R_P2, p2)):
        out = jnp.where(lane == float(idx), val, out)
    r_ref[...] = out


def _router(x, modtab, g2, rw, tm, n_rows, t_lat):
    d = x.shape[1]
    tpb = t_lat // tm
    return pl.pallas_call(
        _router_kernel,
        grid=(n_rows // tm,),
        in_specs=[pl.BlockSpec((tm, d), lambda i: (i, 0)), pl.BlockSpec((1, 6, d), lambda i: (i // tpb, 0, 0)),
                  pl.BlockSpec((1, d), lambda i: (0, 0)), pl.BlockSpec((d, LANES), lambda i: (0, 0))],
        out_specs=[pl.BlockSpec((tm, d // 2), lambda i: (i, 0)), pl.BlockSpec((tm, LANES), lambda i: (i, 0)),
                   pl.BlockSpec((1, LANES), lambda i: (0, 0))],
        out_shape=[jax.ShapeDtypeStruct((n_rows, d // 2), jnp.uint32), jax.ShapeDtypeStruct((n_rows, LANES), F32),
                   jax.ShapeDtypeStruct((1, LANES), F32)],
        scratch_shapes=[pltpu.VMEM((1, LANES), F32)],
        compiler_params=_params(),
        name="moe_router",
    )(x, modtab, g2, rw)


def _row_copies(n, make, unroll=8):
    def start(g, c):
        for u in range(unroll):
            for cp in make(g * unroll + u):
                cp.start(priority=u % 2)
        return c
    lax.fori_loop(0, n // unroll, start, 0)

    def wait(g, c):
        for u in range(unroll):
            for cp in make(g * unroll + u):
                cp.wait()
        return c
    lax.fori_loop(0, n // unroll, wait, 0)


def _dispatch_kernel(d1_ref, d2_ref, hp_ref, xs_in, xs_hbm, sem):
    del xs_in

    def make(r):
        src = hp_ref.at[pl.ds(r, 1), :]
        return (pltpu.make_async_copy(src, xs_hbm.at[pl.ds(d1_ref[0, 0, r], 1), :], sem),
                pltpu.make_async_copy(src, xs_hbm.at[pl.ds(d2_ref[0, 0, r], 1), :], sem))
    _row_copies(hp_ref.shape[0], make)


def _dispatch(d1, d2, hp, n_sorted, tm):
    n_rows, w = hp.shape
    idx = pl.BlockSpec((1, 1, tm), lambda i: (i, 0, 0), memory_space=pltpu.SMEM)
    return pl.pallas_call(
        _dispatch_kernel,
        grid=(n_rows // tm,),
        in_specs=[idx, idx, pl.BlockSpec((tm, w), lambda i: (i, 0)), pl.BlockSpec(memory_space=pl.ANY)],
        out_specs=pl.BlockSpec(memory_space=pl.ANY),
        out_shape=jax.ShapeDtypeStruct((n_sorted, w), jnp.uint32),
        scratch_shapes=[pltpu.SemaphoreType.DMA(())],
        input_output_aliases={3: 0},
        compiler_params=_params(),
        name="moe_dispatch",
    )(d1.reshape(-1, 1, tm), d2.reshape(-1, 1, tm), hp, jnp.zeros((n_sorted, w), jnp.uint32))


def _expert_kernel(te_ref, nv_ref, xs_ref, w1_ref, w3_ref, w2_ref, y_ref, xb_scr, acc_scr, u_scr, *, n_f):
    i, f = pl.program_id(0), pl.program_id(1)
    valid = i < nv_ref[0]
    half = xs_ref.shape[1]

    @pl.when(valid & (f == 0))
    def _():
        lo, hi = _unpack_bf16_pair(xs_ref[...])
        xb_scr[:, :half] = lo
        xb_scr[:, half:] = hi

    @pl.when(valid)
    def _():
        xb = xb_scr[...]

        @pl.when(f == 0)
        def _():
            acc_scr[...] = jnp.zeros_like(acc_scr)

        for c0, cw in _chunks(w1_ref.shape[2], 256):
            a = jnp.dot(xb, w1_ref[0, :, c0:c0 + cw].astype(BF16), preferred_element_type=F32)
            b = jnp.dot(xb, w3_ref[0, :, c0:c0 + cw].astype(BF16), preferred_element_type=F32)
            u_scr[:, c0:c0 + cw] = (_silu(a) * b).astype(BF16)
        acc_scr[...] += jnp.dot(u_scr[...], w2_ref[0].astype(BF16), preferred_element_type=F32)

        @pl.when(f == n_f - 1)
        def _():
            y_ref[...] = acc_scr[...]

    @pl.when(jnp.logical_not(valid) & (f == n_f - 1))
    def _():
        y_ref[...] = jnp.zeros_like(y_ref)


def _experts(tile_expert, n_valid, xs, w1, w3, w2, tme, tf):
    ne, d, f = w1.shape
    n_tiles = xs.shape[0] // tme
    n_f = f // tf
    return pl.pallas_call(
        functools.partial(_expert_kernel, n_f=n_f),
        grid_spec=pltpu.PrefetchScalarGridSpec(
            num_scalar_prefetch=2,
            grid=(n_tiles, n_f),
            in_specs=[pl.BlockSpec((tme, d // 2), lambda i, j, te, nv: (i, 0)),
                      pl.BlockSpec((1, d, tf), lambda i, j, te, nv: (te[i], 0, j)),
                      pl.BlockSpec((1, d, tf), lambda i, j, te, nv: (te[i], 0, j)),
                      pl.BlockSpec((1, tf, d), lambda i, j, te, nv: (te[i], j, 0))],
            out_specs=pl.BlockSpec((tme, d), lambda i, j, te, nv: (i, 0)),
            scratch_shapes=[pltpu.VMEM((tme, d), BF16), pltpu.VMEM((tme, d), F32), pltpu.VMEM((tme, tf), BF16)]),
        out_shape=jax.ShapeDtypeStruct((xs.shape[0], d), F32),
        compiler_params=_params(),
        name="moe_experts",
    )(tile_expert, n_valid, xs, w1, w3, w2)


def _combine_kernel(d1_ref, d2_ref, x_ref, r_ref, mod_ref, y_hbm, xo_ref, y1_scr, y2_scr, sem):
    def make(r):
        return (pltpu.make_async_copy(y_hbm.at[pl.ds(d1_ref[0, 0, r], 1), :], y1_scr.at[pl.ds(r, 1), :], sem),
                pltpu.make_async_copy(y_hbm.at[pl.ds(d2_ref[0, 0, r], 1), :], y2_scr.at[pl.ds(r, 1), :], sem))
    _row_copies(x_ref.shape[0], make)
    r = r_ref[...]
    p1 = r[:, R_P1:R_P1 + 1]
    p2 = r[:, R_P2:R_P2 + 1]
    xo_ref[...] = x_ref[...] + mod_ref[0, 5:6, :] * (p1 * y1_scr[...] + p2 * y2_scr[...])


def _combine(d1, d2, x, y, r, modtab, tm, n_rows, t_lat):
    d = x.shape[1]
    tpb = t_lat // tm
    idx = pl.BlockSpec((1, 1, tm), lambda i: (i, 0, 0), memory_space=pltpu.SMEM)
    return pl.pallas_call(
        _combine_kernel,
        grid=(n_rows // tm,),
        in_specs=[idx, idx, pl.BlockSpec((tm, d), lambda i: (i, 0)), pl.BlockSpec((tm, LANES), lambda i: (i, 0)),
                  pl.BlockSpec((1, 6, d), lambda i: (i // tpb, 0, 0)), pl.BlockSpec(memory_space=pl.ANY)],
        out_specs=pl.BlockSpec((tm, d), lambda i: (i, 0)),
        out_shape=jax.ShapeDtypeStruct((n_rows, d), F32),
        scratch_shapes=[pltpu.VMEM((tm, d), F32), pltpu.VMEM((tm, d), F32), pltpu.SemaphoreType.DMA(())],
        compiler_params=_params(),
        name="moe_combine",
    )(d1.reshape(-1, 1, tm), d2.reshape(-1, 1, tm), x, r, modtab, y)


def _moe(x, modtab, g2, router_w, w1, w3, w2, n_rows, t_lat, tme=1024, tf=512, tm=512):
    d = x.shape[1]
    rw = jnp.zeros((d, LANES), F32).at[:, :N_EXPERTS].set(router_w)
    hp, r, cnt = _router(x, modtab, g2, rw, tm, n_rows, t_lat)
    e1, e2 = r[:, R_E1].astype(jnp.int32), r[:, R_E2].astype(jnp.int32)
    r1, r2 = r[:, R_R1].astype(jnp.int32), r[:, R_R2].astype(jnp.int32)
    counts = cnt[0, :N_EXPERTS].astype(jnp.int32)
    padded = (counts + tme - 1) // tme * tme
    ends = jnp.cumsum(padded)
    offs = ends - padded
    n_sorted = -(-(2 * n_rows + N_EXPERTS * tme) // tme) * tme
    d1, d2 = offs[e1] + r1, offs[e2] + r2
    tile_start = jnp.arange(n_sorted // tme, dtype=jnp.int32) * tme
    n_valid = (ends[-1] // tme).astype(jnp.int32).reshape(1)
    tile_expert = jnp.minimum(jnp.sum(tile_start[:, None] >= ends[None, :], axis=1), N_EXPERTS - 1).astype(jnp.int32)
    tile_expert = jnp.where(tile_start < ends[-1], tile_expert, tile_expert[jnp.maximum(n_valid[0] - 1, 0)])
    xs = _dispatch(d1, d2, hp, n_sorted, tm)
    y = _experts(tile_expert, n_valid, xs, w1, w3, w2, tme, tf)
    return _combine(d1, d2, x, y, r, modtab, tm, n_rows, t_lat)


def _rope_tables(t_lat, pad_rows):
    rows = t_lat // GRID_W
    pos_r = jnp.repeat(jnp.arange(rows, dtype=F32), GRID_W)
    pos_c = jnp.tile(jnp.arange(GRID_W, dtype=F32), rows)
    inv = 1.0 / (ROPE_BASE ** (jnp.arange(ROPE_FREQS, dtype=F32) * 2.0 / (A_HEAD_DIM // 2)))
    ar, ac = pos_r[:, None] * inv, pos_c[:, None] * inv
    cos = jnp.concatenate([jnp.cos(ar), jnp.cos(ar), jnp.cos(ac), jnp.cos(ac)], axis=1)
    sin = jnp.concatenate([-jnp.sin(ar), jnp.sin(ar), -jnp.sin(ac), jnp.sin(ac)], axis=1)
    cos = jnp.concatenate([jnp.tile(cos, (1, A_KV_HEADS)), jnp.ones((pad_rows, A_KV_HEADS * A_HEAD_DIM), F32)])
    sin = jnp.concatenate([jnp.tile(sin, (1, A_KV_HEADS)), jnp.zeros((pad_rows, A_KV_HEADS * A_HEAD_DIM), F32)])
    return cos, sin


def _split_w_in(w, b):
    o = [0, 512, 1024, 2048, 3072, 3088, 5136, 6160, 6416, 6672, 9744]
    order = [(o[0], o[4]), (o[5], o[6]), (o[6], o[7]), (o[9], o[10]), (o[7], o[8]), (o[8], o[9])]
    w_z = jnp.concatenate([w[:, a:e] for a, e in order], axis=1).astype(BF16)
    b_z = jnp.concatenate([b[a:e] for a, e in order]).reshape(1, NZ)
    ng = 4 * M_HEADS
    w_g = jnp.zeros((w.shape[0], LANES), BF16).at[:, :ng].set(w[:, o[4]:o[5]].astype(BF16))
    b_g = jnp.zeros((1, LANES), F32).at[0, :ng].set(b[o[4]:o[5]])
    return w_z, b_z, w_g, b_g


def kernel(x, c, ctx, c_ctx, w_mod, b_mod, norm1_g, norm2_g, w_in, b_in, m_norm_g, conv_w, conv_b, conv_ln_g,
           conv_ln_b, q_norm_g, k_norm_g, attn_sink, w_branch_a, w_branch_b, w_branch_c, w_out, ffn_w1, ffn_w3,
           ffn_w2, router_w, moe_w1, moe_w3, moe_w2):
    n_batch, t_lat, d = x.shape
    t_ctx = ctx.shape[1]
    depth = w_mod.shape[0]
    n_lat, n_ctx = n_batch * t_lat, n_batch * t_ctx
    m = n_lat + n_ctx
    tm = min(1024, n_ctx)
    assert t_lat % tm == 0 and n_ctx % tm == 0 and t_lat >= 3 * A_BLOCK

    xs = (x.reshape(n_lat, d), ctx.reshape(n_ctx, d)) if depth > 1 else jnp.concatenate(
        [x.reshape(n_lat, d), ctx.reshape(n_ctx, d)], axis=0)
    cvec = jnp.zeros((16, d), F32).at[:n_batch].set(c).at[n_batch].set(c_ctx)
    mod = _modulation(cvec, w_mod, b_mod)[:, :n_batch + 1].reshape(depth, n_batch + 1, 6, d)
    tmm = min(512, tm)
    cos_tab, sin_tab = _rope_tables(t_lat, tmm)

    for l in range(depth):
        last = l == depth - 1
        modtab = mod[l]
        w_z, b_z, w_g, b_g = _split_w_in(w_in[l], b_in[l])
        x_lat, x_ctx = xs if isinstance(xs, tuple) else (xs, xs)
        z, gates, qt, vt = _in_projection(x_lat, x_ctx, modtab, norm1_g[l].reshape(1, d), w_z, b_z, w_g, b_g, cos_tab,
                                          sin_tab, q_norm_g[l], k_norm_g[l], tmm, n_lat, n_ctx, t_lat)

        aux = _gate_prep(gates, tm).reshape(4, M_HEADS, m)
        auxr = jnp.pad(aux.transpose(1, 0, 2), ((0, 0), (0, 4), (0, 0)))
        ya_lat, ya_ctx = _mlstm(z, qt, vt, auxr, m_norm_g[l], n_batch, t_lat, t_ctx)

        conv_args = (conv_w[l], conv_b[l], conv_ln_g[l], conv_ln_b[l])
        yb_lat = _conv_branch(z, 0, n_batch, t_lat, min(512, t_lat), *conv_args)

        yc_lat = _attention(attn_sink[l], z, 0, n_batch, t_lat, t_ctx, n_lat, True)

        wa, wb, wc, wo = (w.astype(BF16) for w in (w_branch_a[l], w_branch_b[l], w_branch_c[l], w_out[l]))
        if not last:
            yb_ctx = _conv_branch(z, n_lat, n_batch, t_ctx, t_ctx, *conv_args)
            yc_ctx = _attention(attn_sink[l], z, n_lat, n_batch, t_ctx, t_ctx, n_lat, False)
            branches = ((ya_lat, ya_ctx), (yb_lat, yb_ctx), (yc_lat, yc_ctx))
            n_rows = m
        else:
            branches = ((ya_lat, None), (yb_lat, None), (yc_lat, None))
            n_rows = n_lat
        xs = _merge(branches, z, xs, modtab, wa, wb, wc, wo, tmm, n_rows, n_lat, t_lat)

        g2 = norm2_g[l].reshape(1, d)
        if l % 2 == 0:
            i = l // 2
            xs = _dense_ffn(xs, modtab, g2, ffn_w1[i].astype(BF16), ffn_w3[i].astype(BF16), ffn_w2[i].astype(BF16),
                            tmm, n_rows, n_lat, t_lat)
        else:
            i = l // 2
            assert last, "expert mixer is implemented for latent rows only"
            xs = _moe(xs, modtab, g2, router_w[i], moe_w1[i], moe_w3[i], moe_w2[i], n_lat, t_lat)
    return xs[:n_lat].reshape(n_batch, t_lat, d)
```
